```python
import math
import jax
import jax.numpy as jnp
from jax import lax
import numpy as np

D_MODEL = 1024
BATCH = 4
SEQ = 4096
DEPTH = 1
DEC_BATCH = 128
DEC_SEQ = 1
PAST_LEN = 8192
PAGE_SIZE = 128

N_HEADS = 8
N_KV_HEADS = 2
GQA = N_HEADS // N_KV_HEADS
HEAD_DIM = 64
NSA_WIDTH = N_HEADS * HEAD_DIM
KV_WIDTH = 2 * N_KV_HEADS * HEAD_DIM
N_GATES = 3 * N_HEADS
CMP_LEN = 32
CMP_STRIDE = 16
CMP_RATIO = CMP_LEN // CMP_STRIDE
SEL_BLOCK = 64
N_SELECT = 16
WINDOW = 512
Q_BLOCK = 128
SSM_WIDTH = D_MODEL - NSA_WIDTH
SSM_GROUP = 16
SSM_GROUPS = SSM_WIDTH // SSM_GROUP
SSM_STATE = 64
MIX_WIDTH = NSA_WIDTH + SSM_WIDTH
IN_WIDTH = NSA_WIDTH + 3 * KV_WIDTH + N_GATES + SSM_WIDTH
D_FF = (D_MODEL * 11 // 4 + 127) // 128 * 128
CONV_W = 3
EPS = 1e-6
NEG = -1e30
BIG = 1e30

kernel_name = 'nsa_s5_hybrid_convffn_step'


def rmsnorm(x, g):
    xf = x.astype(jnp.float32)
    y = xf * lax.rsqrt(jnp.mean(xf * xf, axis=-1, keepdims=True) + EPS)
    return (y * g.astype(jnp.float32)).astype(x.dtype)


def pad_rows(a, n):
    return jnp.pad(a, ((0, 0), (0, n)) + ((0, 0),) * (a.ndim - 2))


def project_in(h, w_in):
    b, t = h.shape[:2]
    z = h @ w_in
    c0 = NSA_WIDTH
    cuts = [c0, c0 + KV_WIDTH, c0 + 2 * KV_WIDTH, c0 + 3 * KV_WIDTH, c0 + 3 * KV_WIDTH + N_GATES]
    q, kv_c, kv_s, kv_w, gl, u = jnp.split(z, cuts, axis=-1)
    kv_shape = (b, t, 2, N_KV_HEADS, HEAD_DIM)
    return (q.reshape(b, t, N_KV_HEADS, GQA, HEAD_DIM), kv_c.reshape(kv_shape), kv_s.reshape(kv_shape),
            kv_w.reshape(kv_shape), gl.reshape(b, t, N_KV_HEADS, GQA, 3), u)


def cmp_partials(kv_rows, w_cmp):
    b, l = kv_rows.shape[:2]
    sub = kv_rows.reshape(b, l // CMP_STRIDE, CMP_STRIDE, 2, N_KV_HEADS, HEAD_DIM)
    w = w_cmp.reshape(2, CMP_RATIO, CMP_STRIDE, HEAD_DIM, HEAD_DIM)
    return jnp.einsum('bnsxhd,xrsde->rbnxhe', sub, w)


def cmp_finish(parts, pe_cmp, w_cmp):
    n_cmp = parts.shape[2] - CMP_RATIO + 1
    kv_c = jnp.einsum('xld,xlde->xe', pe_cmp, w_cmp)[None, None, :, None, :]
    for r in range(CMP_RATIO):
        kv_c = kv_c + parts[r][:, r:r + n_cmp]
    cmp_end = jnp.arange(n_cmp) * CMP_STRIDE + CMP_LEN - 1
    return kv_c, cmp_end


def sel_map(n_cmp, n_sel):
    c0 = (jnp.arange(n_cmp) * CMP_STRIDE)[:, None]
    s0 = (jnp.arange(n_sel) * SEL_BLOCK)[None, :]
    return ((c0 < s0 + SEL_BLOCK) & (c0 + CMP_LEN > s0)).astype(jnp.float32)


def dense_block_gather(kv_blocks):
    def gather(idx):
        b = jnp.arange(idx.shape[0])[:, None, None, None]
        h = jnp.arange(N_KV_HEADS)[None, None, :, None]
        return kv_blocks[b, idx, :, :, h]
    return gather


def paged_block_gather(pool, layer, page_table, new_blocks):
    bpp = PAGE_SIZE // SEL_BLOCK
    n_past = page_table.shape[1] * bpp
    n_new = new_blocks.shape[1]
    def gather(idx):
        b = jnp.arange(idx.shape[0])[:, None, None, None]
        h = jnp.arange(N_KV_HEADS)[None, None, :, None]
        past_idx = jnp.minimum(idx, n_past - 1)
        page = page_table[b, past_idx // bpp]
        rows = ((past_idx % bpp) * SEL_BLOCK)[..., None] + jnp.arange(SEL_BLOCK)
        past = pool[layer, page[..., None], rows, :, h[..., None]]
        new = new_blocks[b, jnp.clip(idx - n_past, 0, n_new - 1), :, :, h]
        return jnp.where((idx < n_past)[..., None, None, None], past, new)
    return gather


def masked_softmax(s, m):
    return jax.nn.softmax(jnp.where(m, s, NEG), axis=-1)


def nsa_attend(q, gate_logits, q_pos, kv_c, cmp_end, n_sel, gather_sel, kv_w, w_pos):
    b, tq = q.shape[:2]
    scale = HEAD_DIM ** -0.5
    s_c = jnp.einsum('bqhgd,bnhd->bqhgn', q, kv_c[:, :, 0]).astype(jnp.float32) * scale
    m_c = (cmp_end[None, :] <= q_pos[:, None])[None, :, None, None, :]
    p_c = jnp.where(m_c, masked_softmax(s_c, m_c), 0.0)
    o_c = jnp.einsum('bqhgn,bnhd->bqhgd', p_c.astype(q.dtype), kv_c[:, :, 1])
    imp = jnp.einsum('bqhn,ns->bqhs', p_c.sum(axis=3), sel_map(kv_c.shape[1], n_sel))
    blk = jnp.arange(n_sel)[None, :]
    allowed = (blk * SEL_BLOCK <= q_pos[:, None])[None, :, None, :]
    forced = ((blk == 0) | (blk == (q_pos // SEL_BLOCK)[:, None]))[None, :, None, :]
    score = jnp.where(forced, BIG, jnp.where(allowed, imp, NEG))
    top_val, top_idx = lax.top_k(score, min(N_SELECT, n_sel))
    k_sel = top_idx.shape[-1]
    kv_s = gather_sel(top_idx)
    tok_pos = top_idx[..., None] * SEL_BLOCK + jnp.arange(SEL_BLOCK)
    m_s = ((top_val > 0.5 * NEG)[..., None] & (tok_pos <= q_pos[None, :, None, None, None]))
    m_s = m_s.reshape(b, tq, N_KV_HEADS, 1, k_sel * SEL_BLOCK)
    k_s = kv_s[..., 0, :].reshape(b, tq, N_KV_HEADS, k_sel * SEL_BLOCK, HEAD_DIM)
    v_s = kv_s[..., 1, :].reshape(b, tq, N_KV_HEADS, k_sel * SEL_BLOCK, HEAD_DIM)
    s_s = jnp.einsum('bqhgd,bqhmd->bqhgm', q, k_s).astype(jnp.float32) * scale
    o_s = jnp.einsum('bqhgm,bqhmd->bqhgd', masked_softmax(s_s, m_s).astype(q.dtype), v_s)
    s_w = jnp.einsum('bqhgd,bwhd->bqhgw', q, kv_w[:, :, 0]).astype(jnp.float32) * scale
    m_w = ((w_pos[None, :] <= q_pos[:, None]) & (w_pos[None, :] >= q_pos[:, None] - WINDOW)
           & (w_pos[None, :] >= 0))[None, :, None, None, :]
    o_w = jnp.einsum('bqhgw,bwhd->bqhgd', masked_softmax(s_w, m_w).astype(q.dtype), kv_w[:, :, 1])
    g = jax.nn.sigmoid(gate_logits.astype(jnp.float32)).astype(q.dtype)
    return g[..., 0:1] * o_c + g[..., 1:2] * o_s + g[..., 2:3] * o_w


def _ssm_combine(e1, e2):
    ar1, ai1, br1, bi1 = e1
    ar2, ai2, br2, bi2 = e2
    return (ar1 * ar2 - ai1 * ai2, ar1 * ai2 + ai1 * ar2,
            ar2 * br1 - ai2 * bi1 + br2, ar2 * bi1 + ai2 * br1 + bi2)


def ssm_scan(u, h0_re, h0_im, p):
    b, t = u.shape[:2]
    ug = u.astype(jnp.float32).reshape(b, t, SSM_GROUPS, SSM_GROUP)
    dt = jnp.exp(p['ssm_log_dt'].astype(jnp.float32))[:, None]
    are = p['ssm_a_re'].astype(jnp.float32)
    aim = p['ssm_a_im'].astype(jnp.float32)
    mag = jnp.exp(dt * are)
    ab_re, ab_im = mag * jnp.cos(dt * aim), mag * jnp.sin(dt * aim)
    den = are * are + aim * aim
    zr, zi = ab_re - 1.0, ab_im
    f_re = (zr * are + zi * aim) / den
    f_im = (zi * are - zr * aim) / den
    b_re = p['ssm_b_re'].astype(jnp.float32)
    b_im = p['ssm_b_im'].astype(jnp.float32)
    bb_re = f_re[..., None] * b_re - f_im[..., None] * b_im
    bb_im = f_re[..., None] * b_im + f_im[..., None] * b_re
    bu_re = jnp.einsum('btgc,gpc->btgp', ug, bb_re)
    bu_im = jnp.einsum('btgc,gpc->btgp', ug, bb_im)
    if h0_re is not None:
        h0r = h0_re.astype(jnp.float32)
        h0i = h0_im.astype(jnp.float32)
        bu_re = bu_re.at[:, 0].add(ab_re * h0r - ab_im * h0i)
        bu_im = bu_im.at[:, 0].add(ab_re * h0i + ab_im * h0r)
    a_r = jnp.broadcast_to(ab_re, bu_re.shape)
    a_i = jnp.broadcast_to(ab_im, bu_im.shape)
    _, _, h_re, h_im = lax.associative_scan(_ssm_combine, (a_r, a_i, bu_re, bu_im), axis=1)
    c_re = p['ssm_c_re'].astype(jnp.float32)
    c_im = p['ssm_c_im'].astype(jnp.float32)
    y = (jnp.einsum('btgp,gcp->btgc', h_re, c_re) - jnp.einsum('btgp,gcp->btgc', h_im, c_im)
         + p['ssm_d'].astype(jnp.float32).reshape(SSM_GROUPS, SSM_GROUP) * ug)
    return y.reshape(b, t, SSM_WIDTH).astype(u.dtype), h_re[:, -1], h_im[:, -1]


def ssm_glu(y, p):
    z = jax.nn.gelu(y) @ p['w_glu'] + p['b_glu']
    za, zb = jnp.split(z, 2, axis=-1)
    return za * jax.nn.sigmoid(zb)


def ffn_block(x, hist, p):
    hu = rmsnorm(x, p['norm_ffn']) @ p['w_up']
    hp = jnp.concatenate([hist.astype(hu.dtype), hu], axis=1)
    c = lax.conv_general_dilated(hp, p['conv_w'][:, None, :], (1,), 'VALID',
                                 dimension_numbers=('NWC', 'WIO', 'NWC'),
                                 feature_group_count=hu.shape[-1]) + p['conv_b']
    a, g = jnp.split(c, 2, axis=-1)
    return x + (jax.nn.silu(a) * g) @ p['w_down'], hp[:, -(CONV_W - 1):]


def prompt_layer(x, p):
    b, t, _ = x.shape
    h = rmsnorm(x, p['norm_mix'])
    q, kvc_rows, kvs_rows, kvw_rows, gl, u = project_in(h, p['w_in'])
    kv_c, cmp_end = cmp_finish(cmp_partials(kvc_rows, p['w_cmp']), p['pe_cmp'], p['w_cmp'])
    n_sel = t // SEL_BLOCK
    gather = dense_block_gather(kvs_rows.reshape(b, n_sel, SEL_BLOCK, 2, N_KV_HEADS, HEAD_DIM))
    kvw_pad = pad_rows(kvw_rows[:, ::-1], WINDOW)[:, ::-1]
    n_qb = t // Q_BLOCK
    qb = q.reshape(b, n_qb, Q_BLOCK, N_KV_HEADS, GQA, HEAD_DIM).swapaxes(0, 1)
    gb = gl.reshape(b, n_qb, Q_BLOCK, N_KV_HEADS, GQA, 3).swapaxes(0, 1)

    def query_block(args):
        q_i, g_i, i = args
        start = i * Q_BLOCK
        kvw_i = lax.dynamic_slice_in_dim(kvw_pad, start, WINDOW + Q_BLOCK, axis=1)
        w_pos = start - WINDOW + jnp.arange(WINDOW + Q_BLOCK)
        return nsa_attend(q_i, g_i, start + jnp.arange(Q_BLOCK), kv_c, cmp_end, n_sel, gather, kvw_i, w_pos)

    o = lax.map(query_block, (qb, gb, jnp.arange(n_qb)))
    o_nsa = o.swapaxes(0, 1).reshape(b, t, NSA_WIDTH)
    y_ssm, h_re, h_im = ssm_scan(u, None, None, p)
    x = x + jnp.concatenate([o_nsa, ssm_glu(y_ssm, p)], axis=-1) @ p['w_out']
    x, conv_state = ffn_block(x, jnp.zeros((b, CONV_W - 1, 2 * D_FF), x.dtype), p)
    win_keep = min(WINDOW, t)
    return x, (kvc_rows, kvs_rows, kvw_rows[:, t - win_keep:], h_re, h_im, conv_state)


def sample_layer(x, layer, cache_kv_cmp, cache_kv_sel, kv_win_buf, h0_re, h0_im, conv_hist, page_table, p):
    bd, s, _ = x.shape
    past_len = page_table.shape[1] * PAGE_SIZE
    h = rmsnorm(x, p['norm_mix'])
    q, kvc_new, kvs_new, kvw_new, gl, u = project_in(h, p['w_in'])
    kvc_past = cache_kv_cmp[layer, page_table].reshape(bd, past_len, 2, N_KV_HEADS, HEAD_DIM)
    s_pad = -(-s // CMP_STRIDE) * CMP_STRIDE - s
    parts = jnp.concatenate([cmp_partials(kvc_past, p['w_cmp']),
                             cmp_partials(pad_rows(kvc_new, s_pad), p['w_cmp'])], axis=2)
    kv_c, cmp_end = cmp_finish(parts, p['pe_cmp'], p['w_cmp'])
    n_past_sel = past_len // SEL_BLOCK
    n_new_sel = -(-s // SEL_BLOCK)
    new_blocks = pad_rows(kvs_new, n_new_sel * SEL_BLOCK - s).reshape(
        bd, n_new_sel, SEL_BLOCK, 2, N_KV_HEADS, HEAD_DIM)
    gather = paged_block_gather(cache_kv_sel, layer, page_table, new_blocks)
    n_buf = kv_win_buf.shape[1]
    kv_w = jnp.concatenate([kv_win_buf.astype(kvw_new.dtype), kvw_new], axis=1)
    w_pos = past_len - n_buf + jnp.arange(n_buf + s)
    q_pos = past_len + jnp.arange(s)
    o = nsa_attend(q, gl, q_pos, kv_c, cmp_end, n_past_sel + n_new_sel, gather, kv_w, w_pos)
    o_nsa = o.reshape(bd, s, NSA_WIDTH)
    y_ssm, h_re, h_im = ssm_scan(u, h0_re, h0_im, p)
    x = x + jnp.concatenate([o_nsa, ssm_glu(y_ssm, p)], axis=-1) @ p['w_out']
    x, conv_state = ffn_block(x, conv_hist, p)
    win_keep = min(WINDOW, n_buf + s)
    return x, (kvc_new, kvs_new, kv_w[:, n_buf + s - win_keep:], h_re, h_im, conv_state)


def setup_inputs(seed: int = 0) -> dict:
    key = jax.random.key(seed)
    ks = iter(jax.random.split(key, 40))

    def nrm(shape, scale):
        return scale * jax.random.normal(next(ks), shape, jnp.float32)

    n_pages = PAST_LEN // PAGE_SIZE
    n_used = DEC_BATCH * n_pages
    n_pool = n_used + -(-n_used // 4)
    win_len = min(WINDOW, PAST_LEN)
    page_table = jax.random.permutation(next(ks), n_pool)[:n_used].reshape(DEC_BATCH, n_pages).astype(jnp.int32)
    L = DEPTH
    a_im0 = math.pi * jnp.arange(SSM_STATE, dtype=jnp.float32)
    log_dt = jax.random.uniform(next(ks), (L, SSM_GROUPS), jnp.float32, math.log(1e-3), math.log(1e-1))
    return {
        'x_prompt': nrm((BATCH, SEQ, D_MODEL), 1.0),
        'x_sample': nrm((DEC_BATCH, DEC_SEQ, D_MODEL), 1.0),
        'cache_kv_cmp': nrm((L, n_pool, PAGE_SIZE, 2, N_KV_HEADS, HEAD_DIM), 1.0),
        'cache_kv_sel': nrm((L, n_pool, PAGE_SIZE, 2, N_KV_HEADS, HEAD_DIM), 1.0),
        'cache_kv_win': nrm((L, DEC_BATCH, win_len, 2, N_KV_HEADS, HEAD_DIM), 1.0),
        'state_ssm_re': nrm((L, DEC_BATCH, SSM_GROUPS, SSM_STATE), 0.1),
        'state_ssm_im': nrm((L, DEC_BATCH, SSM_GROUPS, SSM_STATE), 0.1),
        'state_ffn_conv': nrm((L, DEC_BATCH, CONV_W - 1, 2 * D_FF), 1.0),
        'page_table': page_table,
        'norm_mix': 1.0 + nrm((L, D_MODEL), 0.01),
        'w_in': nrm((L, D_MODEL, IN_WIDTH), D_MODEL ** -0.5),
        'pe_cmp': nrm((L, 2, CMP_LEN, HEAD_DIM), 0.02),
        'w_cmp': nrm((L, 2, CMP_LEN, HEAD_DIM, HEAD_DIM), (CMP_LEN * HEAD_DIM) ** -0.5),
        'ssm_a_re': -0.5 + nrm((L, SSM_GROUPS, SSM_STATE), 0.01),
        'ssm_a_im': a_im0 + nrm((L, SSM_GROUPS, SSM_STATE), 0.01),
        'ssm_log_dt': log_dt,
        'ssm_b_re': nrm((L, SSM_GROUPS, SSM_STATE, SSM_GROUP), (2 * SSM_GROUP) ** -0.5),
        'ssm_b_im': nrm((L, SSM_GROUPS, SSM_STATE, SSM_GROUP), (2 * SSM_GROUP) ** -0.5),
        'ssm_c_re': nrm((L, SSM_GROUPS, SSM_GROUP, SSM_STATE), (2 * SSM_STATE) ** -0.5),
        'ssm_c_im': nrm((L, SSM_GROUPS, SSM_GROUP, SSM_STATE), (2 * SSM_STATE) ** -0.5),
        'ssm_d': nrm((L, SSM_WIDTH), 1.0),
        'w_glu': nrm((L, SSM_WIDTH, 2 * SSM_WIDTH), SSM_WIDTH ** -0.5),
        'b_glu': nrm((L, 2 * SSM_WIDTH), 0.01),
        'w_out': nrm((L, MIX_WIDTH, D_MODEL), MIX_WIDTH ** -0.5),
        'norm_ffn': 1.0 + nrm((L, D_MODEL), 0.01),
        'w_up': nrm((L, D_MODEL, 2 * D_FF), D_MODEL ** -0.5),
        'conv_w': nrm((L, CONV_W, 2 * D_FF), CONV_W ** -0.5),
        'conv_b': nrm((L, 2 * D_FF), 0.01),
        'w_down': nrm((L, D_FF, D_MODEL), D_FF ** -0.5),
        'norm_final': 1.0 + nrm((D_MODEL,), 0.01),
    }


def reference(x_prompt, x_sample, cache_kv_cmp, cache_kv_sel, cache_kv_win, state_ssm_re, state_ssm_im,
              state_ffn_conv, page_table, norm_mix, w_in, pe_cmp, w_cmp, ssm_a_re, ssm_a_im, ssm_log_dt,
              ssm_b_re, ssm_b_im, ssm_c_re, ssm_c_im, ssm_d, w_glu, b_glu, w_out, norm_ffn, w_up, conv_w,
              conv_b, w_down, norm_final):
    xp, xs = x_prompt, x_sample
    sp, ss = [], []
    for l in range(DEPTH):
        p = {'norm_mix': norm_mix[l], 'w_in': w_in[l], 'pe_cmp': pe_cmp[l], 'w_cmp': w_cmp[l],
             'ssm_a_re': ssm_a_re[l], 'ssm_a_im': ssm_a_im[l], 'ssm_log_dt': ssm_log_dt[l],
             'ssm_b_re': ssm_b_re[l], 'ssm_b_im': ssm_b_im[l], 'ssm_c_re': ssm_c_re[l], 'ssm_c_im': ssm_c_im[l],
             'ssm_d': ssm_d[l], 'w_glu': w_glu[l], 'b_glu': b_glu[l], 'w_out': w_out[l],
             'norm_ffn': norm_ffn[l], 'w_up': w_up[l], 'conv_w': conv_w[l], 'conv_b': conv_b[l],
             'w_down': w_down[l]}
        xp, st_p = prompt_layer(xp, p)
        sp.append(st_p)
        xs, st_s = sample_layer(xs, l, cache_kv_cmp, cache_kv_sel, cache_kv_win[l], state_ssm_re[l],
                                state_ssm_im[l], state_ffn_conv[l], page_table, p)
        ss.append(st_s)
    y_prompt = rmsnorm(xp, norm_final)
    y_sample = rmsnorm(xs, norm_final)
    p_kv_cmp, p_kv_sel, p_kv_win, p_ssm_re, p_ssm_im, p_ffn_conv = [jnp.stack(a) for a in zip(*sp)]
    s_kv_cmp, s_kv_sel, s_kv_win, s_ssm_re, s_ssm_im, s_ffn_conv = [jnp.stack(a) for a in zip(*ss)]
    return (y_prompt, y_sample, p_kv_cmp, p_kv_sel, p_kv_win, p_ssm_re, p_ssm_im, p_ffn_conv,
            s_kv_cmp, s_kv_sel, s_kv_win, s_ssm_re, s_ssm_im, s_ffn_conv)
```

```python
import functools
import math

import jax
import jax.numpy as jnp
from jax import lax
from jax.experimental import pallas as pl
from jax.experimental.pallas import tpu as pltpu

D_MODEL = 1024
N_HEADS = 8
N_KV_HEADS = 2
GQA = N_HEADS // N_KV_HEADS
HEAD_DIM = 64
NSA_WIDTH = N_HEADS * HEAD_DIM
KV_WIDTH = 2 * N_KV_HEADS * HEAD_DIM
N_GATES = 3 * N_HEADS
CMP_LEN = 32
CMP_STRIDE = 16
SEL_BLOCK = 64
N_SELECT = 16
WINDOW = 512
Q_BLOCK = 128
PAGE_SIZE = 128
SSM_WIDTH = D_MODEL - NSA_WIDTH
SSM_GROUP = 16
SSM_GROUPS = SSM_WIDTH // SSM_GROUP
SSM_STATE = 64
SSM_N = SSM_GROUPS * SSM_STATE
D_FF = (D_MODEL * 11 // 4 + 127) // 128 * 128
CONV_W = 3
EPS = 1e-6
NEG = -1e30
BIG = 1e30
BELOW_NEG = -3e38

LANES = 128
GATE_PAD = LANES
IN_PAD = NSA_WIDTH + 3 * KV_WIDTH + SSM_WIDTH + GATE_PAD
SUB_W = CMP_STRIDE * KV_WIDTH
VMEM_LIMIT = 56 * 1024 * 1024

F32 = jnp.float32
BF16 = jnp.bfloat16


def _nt_dot(a, b):
    return lax.dot_general(a, b, (((1,), (1,)), ((), ())), preferred_element_type=F32)


def _dot(a, b):
    return jnp.dot(a, b, preferred_element_type=F32)


def _sigmoid(x):
    return 1.0 / (1.0 + jnp.exp(-x))


def _gelu_tanh(x):
    return 0.5 * x * (1.0 + jnp.tanh(math.sqrt(2.0 / math.pi) * (x + 0.044715 * (x * x * x))))


def _rms(x, g):
    return x * lax.rsqrt(jnp.mean(x * x, axis=-1, keepdims=True) + EPS) * g


def _split_dot(x, w_bf):
    hi = x.astype(BF16)
    lo = (x - hi.astype(F32)).astype(BF16)
    return _dot(hi, w_bf) + _dot(lo, w_bf)


def _cparams(*sem):
    return pltpu.CompilerParams(dimension_semantics=sem, vmem_limit_bytes=VMEM_LIMIT)


def _const_spec(shape):
    nd = len(shape)
    return pl.BlockSpec(shape, lambda *_: (0,) * nd, pipeline_mode=pl.Buffered(1))


def _inproj_body(x_ref, g_ref, w_ref, kvc_ref, kvs_ref, kvw_ref, gl_ref, u_ref, qb_ref, kvsb_ref, kvwb_ref):
    h = _rms(x_ref[...], g_ref[...])
    z = _dot(h.astype(BF16), w_ref[...])
    c = NSA_WIDTH
    qb_ref[...] = (z[:, :c] * (HEAD_DIM ** -0.5)).astype(BF16)
    kvc_ref[...] = z[:, c:c + KV_WIDTH]
    kvs = z[:, c + KV_WIDTH:c + 2 * KV_WIDTH]
    kvw = z[:, c + 2 * KV_WIDTH:c + 3 * KV_WIDTH]
    kvs_ref[...] = kvs
    kvw_ref[...] = kvw
    kvsb_ref[...] = kvs.astype(BF16)
    kvwb_ref[...] = kvw.astype(BF16)
    c += 3 * KV_WIDTH
    u_ref[...] = z[:, c:c + SSM_WIDTH]
    gl_ref[...] = z[:, c + SSM_WIDTH:]


def _inproj(x2d, g, w_pad, tm):
    n = x2d.shape[0]
    row = lambda w: pl.BlockSpec((tm, w), lambda i: (i, 0))
    outs = [(KV_WIDTH, F32)] * 3 + [(GATE_PAD, F32), (SSM_WIDTH, F32), (NSA_WIDTH, BF16), (KV_WIDTH, BF16),
                                    (KV_WIDTH, BF16)]
    return pl.pallas_call(
        _inproj_body,
        grid=(n // tm,),
        in_specs=[row(D_MODEL), _const_spec((1, D_MODEL)), _const_spec((D_MODEL, IN_PAD))],
        out_specs=[row(w) for w, _ in outs],
        out_shape=[jax.ShapeDtypeStruct((n, w), dt) for w, dt in outs],
        compiler_params=_cparams("parallel"),
        name="in_proj",
    )(x2d, g, w_pad)


def _cmp_prompt_body(sub_ref, pe_ref, w_ref, kc_ref):
    w = w_ref[...]
    parts = _dot(sub_ref[0].astype(BF16), w)
    pe = _dot(pe_ref[...].astype(BF16), w)
    bias = pe[0:1, :KV_WIDTH] + pe[1:2, KV_WIDTH:]
    n_sub = parts.shape[0]
    nxt = pltpu.roll(parts[:, KV_WIDTH:], n_sub - 1, 0)
    kc_ref[0] = (parts[:, :KV_WIDTH] + nxt + bias).astype(BF16)


def _cmp_prompt(sub, pe_sub, w_big):
    b, n_sub, _ = sub.shape
    return pl.pallas_call(
        _cmp_prompt_body,
        grid=(b,),
        in_specs=[pl.BlockSpec((1, n_sub, SUB_W), lambda i: (i, 0, 0)), _const_spec(pe_sub.shape),
                  _const_spec(w_big.shape)],
        out_specs=pl.BlockSpec((1, n_sub, KV_WIDTH), lambda i: (i, 0, 0)),
        out_shape=jax.ShapeDtypeStruct((b, n_sub, KV_WIDTH), BF16),
        compiler_params=_cparams("parallel"),
        name="cmp_prompt",
    )(sub, pe_sub, w_big)


def _stack_heads(q, h):
    return jnp.concatenate([q[:, (GQA * h + g) * HEAD_DIM:(GQA * h + g + 1) * HEAD_DIM] for g in range(GQA)], axis=0)


def _stack_heads_single(q, h):
    row = lax.broadcasted_iota(jnp.int32, (8, 1), 0)
    q8 = jnp.broadcast_to(q, (8, NSA_WIDTH))
    out = jnp.zeros((8, HEAD_DIM), F32)
    for g in range(GQA):
        lo = (GQA * h + g) * HEAD_DIM
        out = out + jnp.where(row == g, q8[:, lo:lo + HEAD_DIM], 0.0)
    return out.astype(BF16)


def _masked_softmax_rows(s, mask):
    sm = jnp.where(mask, s, NEG)
    m = jnp.max(sm, axis=-1, keepdims=True)
    p = jnp.where(mask, jnp.exp(sm - m), 0.0)
    l = jnp.sum(p, axis=-1, keepdims=True)
    return p * (1.0 / jnp.where(l > 0.0, l, 1.0))


def _block_scores(imp, blk, qpos):
    first = blk * SEL_BLOCK
    own_or_imp = jnp.where(first + SEL_BLOCK > qpos, BIG, imp)
    return jnp.where(blk == 0, BIG, jnp.where(first <= qpos, own_or_imp, NEG))


def _top_k_mask(score, blk, k):
    n = float(score.shape[-1])
    sel = jnp.zeros(score.shape, F32)
    for _ in range(k):
        mx = jnp.max(score, axis=-1, keepdims=True)
        idx = jnp.min(jnp.where(score == mx, blk, n), axis=-1, keepdims=True)
        hit = blk == idx
        sel = sel + jnp.where(hit, jnp.where(mx > 0.5 * NEG, 1.0, 0.0), 0.0)
        score = jnp.where(hit, BELOW_NEG, score)
    return sel


def _attn_prompt_body(q_ref, gl_ref, kc_ref, ks_ref, kw_ref, selmap_ref, exp_ref, o_ref, m_s, l_s, acc_s, *, kc_len):
    start = pl.program_id(1) * Q_BLOCK
    q = q_ref[0]
    gate = _sigmoid(gl_ref[0])
    rows = GQA * Q_BLOCK
    qpos = start + lax.broadcasted_iota(jnp.int32, (Q_BLOCK, 1), 0)
    qpos4 = jnp.concatenate([qpos] * GQA, axis=0)
    n_cmp = kc_ref.shape[1]
    n_sel = selmap_ref.shape[1]
    cmp_end = lax.broadcasted_iota(jnp.int32, (1, n_cmp), 1) * CMP_STRIDE + (CMP_LEN - 1)
    blk = lax.broadcasted_iota(jnp.int32, (1, n_sel), 1)
    head_out = []
    for h in range(N_KV_HEADS):
        k_lo, v_lo = h * HEAD_DIM, (N_KV_HEADS + h) * HEAD_DIM
        qs = _stack_heads(q, h)
        p_c = _masked_softmax_rows(_nt_dot(qs, kc_ref[0, :, k_lo:k_lo + HEAD_DIM]), cmp_end <= qpos4)
        o_c = _dot(p_c.astype(BF16), kc_ref[0, :, v_lo:v_lo + HEAD_DIM])
        p_sum = p_c[0:Q_BLOCK]
        for g in range(1, GQA):
            p_sum = p_sum + p_c[g * Q_BLOCK:(g + 1) * Q_BLOCK]
        imp = _split_dot(p_sum, selmap_ref[...])
        score = _block_scores(imp, blk, qpos)
        sel =_top_k_mask(score, blk.astype(F32), min(N_SELECT, n_sel)).astype(BF16)
        m_s[...] = jnp.full(m_s.shape, NEG, F32)
        l_s[...] = jnp.zeros(l_s.shape, F32)
        acc_s[...] = jnp.zeros(acc_s.shape, F32)

        def sel_chunk(c, carry):
            off = pl.multiple_of(c * kc_len, kc_len)
            k = ks_ref[0, pl.ds(off, kc_len), k_lo:k_lo + HEAD_DIM]
            v = ks_ref[0, pl.ds(off, kc_len), v_lo:v_lo + HEAD_DIM]
            s = _nt_dot(qs, k)
            chosen = _dot(sel, exp_ref[c])
            kpos = off + lax.broadcasted_iota(jnp.int32, (1, kc_len), 1)
            vis = jnp.where(kpos <= qpos, chosen, 0.0)
            mask = jnp.concatenate([vis] * GQA, axis=0) > 0.5
            sm = jnp.where(mask, s, NEG)
            m_prev = m_s[...]
            m_next = jnp.maximum(m_prev, jnp.max(sm, axis=-1, keepdims=True))
            p = jnp.where(mask, jnp.exp(sm - jnp.tile(m_next, (1, kc_len // LANES))), 0.0)
            alpha = jnp.exp(m_prev - m_next)
            l_s[...] = alpha * l_s[...] + jnp.sum(p, axis=-1, keepdims=True)
            acc_s[...] = acc_s[...] * alpha[:, :HEAD_DIM] + _dot(p.astype(BF16), v)
            m_s[...] = m_next
            return carry

        lax.fori_loop(0, (start + Q_BLOCK + kc_len - 1) // kc_len, sel_chunk, 0)
        o_s = acc_s[...] * (1.0 / l_s[...])[:, :HEAD_DIM]
        kw, vw = [], []
        for c in range((WINDOW + Q_BLOCK) // Q_BLOCK):
            src = pl.multiple_of(jnp.maximum(start - WINDOW + c * Q_BLOCK, 0), Q_BLOCK)
            kw.append(kw_ref[0, pl.ds(src, Q_BLOCK), k_lo:k_lo + HEAD_DIM])
            vw.append(kw_ref[0, pl.ds(src, Q_BLOCK), v_lo:v_lo + HEAD_DIM])
        s = _nt_dot(qs, jnp.concatenate(kw, axis=0))
        wpos = start - WINDOW + lax.broadcasted_iota(jnp.int32, (1, WINDOW + Q_BLOCK), 1)
        sm = jnp.where(wpos <= qpos4, jnp.where(wpos >= jnp.maximum(qpos4 - WINDOW, 0), s, NEG), NEG)
        p = jnp.exp(sm - jnp.max(sm, axis=-1, keepdims=True))
        p = p * (1.0 / jnp.sum(p, axis=-1, keepdims=True))
        o_w = _dot(p.astype(BF16), jnp.concatenate(vw, axis=0))
        for g in range(GQA):
            col = (GQA * h + g) * 3
            r = slice(g * Q_BLOCK, (g + 1) * Q_BLOCK)
            head_out.append(gate[:, col:col + 1] * o_c[r] + gate[:, col + 1:col + 2] * o_s[r]
                            + gate[:, col + 2:col + 3] * o_w[r])
    o_ref[0] = jnp.concatenate(head_out, axis=-1).astype(BF16)


def _attn_prompt(qb, gl, kc, ksb, kwb, selmap, kc_len=512):
    b, t, _ = qb.shape
    expand = _expand_map(selmap.shape[1], t, kc_len)
    rows = GQA * Q_BLOCK
    whole = lambda a: pl.BlockSpec((1,) + a.shape[1:], lambda i, j: (i, 0, 0))
    return pl.pallas_call(
        functools.partial(_attn_prompt_body, kc_len=kc_len),
        grid=(b, t // Q_BLOCK),
        in_specs=[pl.BlockSpec((1, Q_BLOCK, NSA_WIDTH), lambda i, j: (i, j, 0)),
                  pl.BlockSpec((1, Q_BLOCK, GATE_PAD), lambda i, j: (i, j, 0)),
                  whole(kc), whole(ksb), whole(kwb), _const_spec(selmap.shape), _const_spec(expand.shape)],
        out_specs=pl.BlockSpec((1, Q_BLOCK, NSA_WIDTH), lambda i, j: (i, j, 0)),
        out_shape=jax.ShapeDtypeStruct((b, t, NSA_WIDTH), BF16),
        scratch_shapes=[pltpu.VMEM((rows, LANES), F32), pltpu.VMEM((rows, LANES), F32),
                        pltpu.VMEM((rows, HEAD_DIM), F32)],
        compiler_params=_cparams("parallel", "arbitrary"),
        name="attn_prompt",
    )(qb, gl, kc, ksb, kwb, selmap, expand)


def _ssm_param_body(ldt_ref, are_ref, aim_ref, bre_ref, bim_ref, abr_ref, abi_ref, bbr_ref, bbi_ref):
    dt = jnp.exp(ldt_ref[...])
    are, aim = are_ref[...], aim_ref[...]
    mag = jnp.exp(dt * are)
    ab_re, ab_im = mag * jnp.cos(dt * aim), mag * jnp.sin(dt * aim)
    den = are * are + aim * aim
    zr, zi = ab_re - 1.0, ab_im
    f_re = (zr * are + zi * aim) / den
    f_im = (zi * are - zr * aim) / den
    abr_ref[...] = ab_re
    abi_ref[...] = ab_im
    bbr_ref[...] = f_re * bre_ref[...] - f_im * bim_ref[...]
    bbi_ref[...] = f_re * bim_ref[...] + f_im * bre_ref[...]


def _ssm_params(log_dt, a_re, a_im, b_re, b_im):
    col = lambda a: a.reshape(SSM_N, 1)
    ldt = col(jnp.broadcast_to(log_dt[:, None], (SSM_GROUPS, SSM_STATE)))
    col_t = jax.ShapeDtypeStruct((SSM_N, 1), F32)
    mat_t = jax.ShapeDtypeStruct((SSM_N, SSM_GROUP), F32)
    ab_re, ab_im, bb_re, bb_im = pl.pallas_call(
        _ssm_param_body, out_shape=[col_t, col_t, mat_t, mat_t], name="ssm_params",
    )(ldt, col(a_re), col(a_im), b_re.reshape(SSM_N, SSM_GROUP), b_im.reshape(SSM_N, SSM_GROUP))
    return ab_re.reshape(1, SSM_N), ab_im.reshape(1, SSM_N), bb_re, bb_im


def _block_diag_in(bb):
    m = bb.reshape(SSM_GROUPS, SSM_STATE, SSM_GROUP).transpose(0, 2, 1)
    eye = jnp.eye(SSM_GROUPS, dtype=bb.dtype)
    return (eye[:, None, :, None] * m[:, :, None, :]).reshape(SSM_WIDTH, SSM_N)


def _block_diag_out(c):
    m = c.transpose(0, 2, 1)
    eye = jnp.eye(SSM_GROUPS, dtype=c.dtype)
    return (eye[:, None, :, None] * m[:, :, None, :]).reshape(SSM_N, SSM_WIDTH)


def _ssm_prompt_body(u_ref, bb_ref, cc_ref, d_ref, abr_ref, abi_ref, y_ref, hr_ref, hi_ref, bu_s, hs_s, st_s):
    tc = u_ref.shape[1]

    @pl.when(pl.program_id(1) == 0)
    def _():
        st_s[...] = jnp.zeros(st_s.shape, F32)

    u = u_ref[0]
    bu_s[...] = _dot(u.astype(BF16), bb_ref[...])
    ar, ai = abr_ref[...], abi_ref[...]

    def step(t, carry):
        hr, hi = carry
        nr = ar * hr - ai * hi + bu_s[pl.ds(t, 1), :SSM_N]
        ni = ar * hi + ai * hr + bu_s[pl.ds(t, 1), SSM_N:]
        hs_s[pl.ds(t, 1), :SSM_N] = nr
        hs_s[pl.ds(t, 1), SSM_N:] = ni
        return nr, ni

    hr, hi = lax.fori_loop(0, tc, step, (st_s[0:1, :], st_s[1:2, :]), unroll=8)
    st_s[0:1, :] = hr
    st_s[1:2, :] = hi
    y_ref[0] = _dot(hs_s[...].astype(BF16), cc_ref[...]) + d_ref[...] * u
    hr_ref[0] = hr
    hi_ref[0] = hi


def _ssm_prompt(u, bb, cc, d, ab_re, ab_im, tc=256):
    b, t, _ = u.shape
    st = jax.ShapeDtypeStruct((b, 1, SSM_N), F32)
    st_spec = pl.BlockSpec((1, 1, SSM_N), lambda i, j: (i, 0, 0))
    return pl.pallas_call(
        _ssm_prompt_body,
        grid=(b, t // tc),
        in_specs=[pl.BlockSpec((1, tc, SSM_WIDTH), lambda i, j: (i, j, 0)), _const_spec(bb.shape),
                  _const_spec(cc.shape), _const_spec(d.shape), _const_spec(ab_re.shape), _const_spec(ab_im.shape)],
        out_specs=[pl.BlockSpec((1, tc, SSM_WIDTH), lambda i, j: (i, j, 0)), st_spec, st_spec],
        out_shape=[jax.ShapeDtypeStruct((b, t, SSM_WIDTH), F32), st, st],
        scratch_shapes=[pltpu.VMEM((tc, 2 * SSM_N), F32), pltpu.VMEM((tc, 2 * SSM_N), F32),
                        pltpu.VMEM((8, SSM_N), F32)],
        compiler_params=_cparams("parallel", "arbitrary"),
        name="ssm_prompt",
    )(u, bb, cc, d, ab_re, ab_im)


FF_CHUNK = 256


def _mix_out(x, o_nsa, y_ssm, wglu_ref, bglu_ref, wout_ref):
    z = _dot(_gelu_tanh(y_ssm).astype(BF16), wglu_ref[...]) + bglu_ref[...]
    glu = z[:, :SSM_WIDTH] * _sigmoid(z[:, SSM_WIDTH:])
    return x + _dot(o_nsa, wout_ref[:NSA_WIDTH, :]) + _dot(glu.astype(BF16), wout_ref[NSA_WIDTH:, :])


def _ffn_chunks(hn, wup_ref, cw_ref, cb_ref, wdown_ref, prev_rows):
    acc = jnp.zeros((hn.shape[0], D_MODEL), F32)
    for j in range(D_FF // FF_CHUNK):
        conv = []
        for base in (0, D_FF):
            lo = base + j * FF_CHUNK
            hi = lo + FF_CHUNK
            hu = _dot(hn, wup_ref[:, lo:hi])
            hu2, hu1 = prev_rows(lo, hi, hu)
            conv.append(cw_ref[0:1, lo:hi] * hu2 + cw_ref[1:2, lo:hi] * hu1 + cw_ref[2:3, lo:hi] * hu
                        + cb_ref[:, lo:hi])
        a, g = conv
        act = (a * _sigmoid(a) * g).astype(BF16)
        acc = acc + _dot(act, wdown_ref[j * FF_CHUNK:(j + 1) * FF_CHUNK, :])
    return acc


def _tail_prompt_body(x_ref, o_ref, y_ref, wglu_ref, bglu_ref, wout_ref, nf_ref, wup_ref, cw_ref, cb_ref, wdown_ref,
                      nfin_ref, out_ref, cs_ref, prev_s):
    tm = x_ref.shape[1]

    @pl.when(pl.program_id(1) == 0)
    def _():
        prev_s[...] = jnp.zeros(prev_s.shape, F32)

    x1 = _mix_out(x_ref[0], o_ref[0], y_ref[0], wglu_ref, bglu_ref, wout_ref)
    hn = _rms(x1, nf_ref[...]).astype(BF16)
    row = lax.broadcasted_iota(jnp.int32, (tm, 1), 0)

    def prev_rows(lo, hi, hu):
        p2, p1 = prev_s[6:7, lo:hi], prev_s[7:8, lo:hi]
        hu1 = jnp.where(row == 0, p1, pltpu.roll(hu, 1, 0))
        hu2 = jnp.where(row == 0, p2, jnp.where(row == 1, p1, pltpu.roll(hu, 2, 0)))
        prev_s[:, lo:hi] = hu[tm - 8:, :]
        cs_ref[0, :, lo:hi] = hu[tm - (CONV_W - 1):, :]
        return hu2, hu1

    x2 = x1 + _ffn_chunks(hn, wup_ref, cw_ref, cb_ref, wdown_ref, prev_rows)
    out_ref[0] = _rms(x2, nfin_ref[...])


def _tail_prompt(x, o_nsa, y_ssm, w, tm=512):
    b, t, _ = x.shape
    tile = lambda width: pl.BlockSpec((1, tm, width), lambda i, j: (i, j, 0))
    consts = [w["w_glu"], w["b_glu"], w["w_out"], w["norm_ffn"], w["w_up"], w["conv_w"], w["conv_b"], w["w_down"],
              w["norm_final"]]
    return pl.pallas_call(
        _tail_prompt_body,
        grid=(b, t // tm),
        in_specs=[tile(D_MODEL), tile(NSA_WIDTH), tile(SSM_WIDTH)] + [_const_spec(c.shape) for c in consts],
        out_specs=[tile(D_MODEL), pl.BlockSpec((1, CONV_W - 1, 2 * D_FF), lambda i, j: (i, 0, 0))],
        out_shape=[jax.ShapeDtypeStruct((b, t, D_MODEL), F32), jax.ShapeDtypeStruct((b, CONV_W - 1, 2 * D_FF), F32)],
        scratch_shapes=[pltpu.VMEM((8, 2 * D_FF), F32)],
        compiler_params=_cparams("parallel", "arbitrary"),
        name="tail_prompt",
    )(x, o_nsa, y_ssm, *consts)


def _cmp_sample_body(pt_ref, pool_ref, new_ref, q_ref, pe_ref, w_ref, selmap_ref, oc_ref, idx_ref, buf, sem, *,
                     n_pages, n_seq):
    b = pl.program_id(0)
    sub_per_page = PAGE_SIZE // CMP_STRIDE
    n_sub = n_pages * sub_per_page

    def page_copy(seq, slot, p):
        return pltpu.make_async_copy(pool_ref.at[pt_ref[seq * n_pages + p]],
                                     buf.at[slot, pl.ds(p * sub_per_page, sub_per_page)], sem.at[slot])

    def fetch(seq, slot):
        for p in range(n_pages):
            page_copy(seq, slot, p).start()

    @pl.when(b == 0)
    def _():
        fetch(0, 0)

    @pl.when(b + 1 < n_seq)
    def _():
        fetch(b + 1, (b + 1) % 2)

    slot = b % 2
    for p in range(n_pages):
        page_copy(b, slot, p).wait()

    w = w_ref[...]
    parts = _dot(buf[slot].astype(BF16), w)
    pe = _dot(pe_ref[...].astype(BF16), w)
    bias = pe[0:1, :KV_WIDTH] + pe[1:2, KV_WIDTH:]
    new_part = _dot(jnp.broadcast_to(new_ref[0], (8, KV_WIDTH)).astype(BF16), w_ref[:KV_WIDTH, :])[0:1, KV_WIDTH:]
    row = lax.broadcasted_iota(jnp.int32, (n_sub, 1), 0)
    nxt = jnp.where(row == n_sub - 1, new_part, pltpu.roll(parts[:, KV_WIDTH:], n_sub - 1, 0))
    kc = (parts[:, :KV_WIDTH] + nxt + bias).astype(BF16)

    q_pos = n_pages * PAGE_SIZE
    n_sel = selmap_ref.shape[1]
    n_real = q_pos // SEL_BLOCK + 1
    cmp_end = lax.broadcasted_iota(jnp.int32, (1, n_sub), 1) * CMP_STRIDE + (CMP_LEN - 1)
    blk = lax.broadcasted_iota(jnp.int32, (1, n_sel), 1)
    q = q_ref[0].astype(F32)
    o_heads, scores = [], []
    for h in range(N_KV_HEADS):
        k_lo, v_lo = h * HEAD_DIM, (N_KV_HEADS + h) * HEAD_DIM
        qs = _stack_heads_single(q, h)
        p_c = _masked_softmax_rows(_nt_dot(qs, kc[:, k_lo:k_lo + HEAD_DIM]), cmp_end <= q_pos)
        o_c = _dot(p_c.astype(BF16), kc[:, v_lo:v_lo + HEAD_DIM])
        o_heads += [o_c[g:g + 1] for g in range(GQA)]
        imp = _split_dot(jnp.broadcast_to(jnp.sum(p_c[:GQA], axis=0, keepdims=True), (8, n_sub)), selmap_ref[...])[0:1]
        score = _block_scores(imp, blk, q_pos)
        scores.append(jnp.where(blk < n_real, score, BELOW_NEG))
    oc_ref[0] = jnp.concatenate(o_heads, axis=-1)
    score = jnp.concatenate(scores + [jnp.full((8 - N_KV_HEADS, n_sel), BELOW_NEG, F32)], axis=0)
    lane = lax.broadcasted_iota(jnp.int32, (8, LANES), 1)
    blk_f = blk.astype(F32)
    picked = jnp.full((8, LANES), -1.0, F32)
    for it in range(min(N_SELECT, n_real)):
        mx = jnp.max(score, axis=-1, keepdims=True)
        idx = jnp.min(jnp.where(score == mx, blk_f, float(n_sel)), axis=-1, keepdims=True)
        picked = jnp.where(lane == it, jnp.where(mx > 0.5 * NEG, idx, -1.0), picked)
        score = jnp.where(blk_f == idx, BELOW_NEG, score)
    idx_ref[0] = picked.astype(jnp.int32)


def _cmp_sample(page_table, pool, kvc_new, qb, pe_sub, w_big, selmap):
    n_seq, n_pages = page_table.shape
    n_sub = n_pages * (PAGE_SIZE // CMP_STRIDE)
    grid_spec = pltpu.PrefetchScalarGridSpec(
        num_scalar_prefetch=1,
        grid=(n_seq,),
        in_specs=[pl.BlockSpec(memory_space=pl.ANY),
                  pl.BlockSpec((1, 1, KV_WIDTH), lambda i, pt: (i, 0, 0)),
                  pl.BlockSpec((1, 1, NSA_WIDTH), lambda i, pt: (i, 0, 0)),
                  _const_spec(pe_sub.shape), _const_spec(w_big.shape), _const_spec(selmap.shape)],
        out_specs=[pl.BlockSpec((1, 1, NSA_WIDTH), lambda i, pt: (i, 0, 0)),
                   pl.BlockSpec((1, 8, LANES), lambda i, pt: (i, 0, 0))],
        scratch_shapes=[pltpu.VMEM((2, n_sub, SUB_W), F32), pltpu.SemaphoreType.DMA((2,))],
    )
    return pl.pallas_call(
        functools.partial(_cmp_sample_body, n_pages=n_pages, n_seq=n_seq),
        grid_spec=grid_spec,
        out_shape=[jax.ShapeDtypeStruct((n_seq, 1, NSA_WIDTH), F32), jax.ShapeDtypeStruct((n_seq, 8, LANES), jnp.int32)],
        compiler_params=_cparams("arbitrary"),
        name="cmp_sample",
    )(page_table.reshape(-1), pool, kvc_new, qb, pe_sub, w_big, selmap)


def _attn_sample_body(idx_ref, pt_ref, pool_ref, q_ref, gl_ref, oc_ref, ksn_ref, kwn_ref, win_ref, o_ref, wout_ref,
                      buf, sem, *, n_pages, n_seq, k_sel):
    b = pl.program_id(0)
    blk_per_page = PAGE_SIZE // SEL_BLOCK
    n_past = n_pages * blk_per_page
    n_blk = N_KV_HEADS * k_sel

    def blk_copy(seq, slot, j):
        i = jnp.clip(idx_ref[seq * n_blk + j], 0, n_past - 1)
        page = pt_ref[seq * n_pages + i // blk_per_page]
        return pltpu.make_async_copy(pool_ref.at[page, i % blk_per_page], buf.at[slot, j], sem.at[slot])

    def fetch(seq, slot):
        for j in range(n_blk):
            blk_copy(seq, slot, j).start()

    @pl.when(b == 0)
    def _():
        fetch(0, 0)

    @pl.when(b + 1 < n_seq)
    def _():
        fetch(b + 1, (b + 1) % 2)

    slot = b % 2
    for j in range(n_blk):
        blk_copy(b, slot, j).wait()

    q = q_ref[0].astype(F32)
    gate = _sigmoid(gl_ref[0])
    o_c = oc_ref[0]
    ks_new = ksn_ref[0].astype(BF16).astype(F32)
    kw_new = kwn_ref[0].astype(BF16).astype(F32)
    win = win_ref[0]
    n_win = win.shape[0]
    win_bf = win.astype(BF16)
    tok_blk = lax.shift_right_logical(lax.broadcasted_iota(jnp.int32, (1, k_sel * SEL_BLOCK), 1),
                                      SEL_BLOCK.bit_length() - 1)
    head_out = []
    for h in range(N_KV_HEADS):
        k_lo, v_lo = h * HEAD_DIM, (N_KV_HEADS + h) * HEAD_DIM
        qs = _stack_heads_single(q, h)
        qf = qs.astype(F32)
        past = jnp.zeros((1, k_sel * SEL_BLOCK), F32)
        has_new = jnp.zeros((1, 1), F32)
        for j in range(k_sel):
            i = idx_ref[b * n_blk + h * k_sel + j]
            past = jnp.where(tok_blk == j, jnp.where((i >= 0) & (i < n_past), 1.0, 0.0), past)
            has_new = jnp.maximum(has_new, jnp.where(i >= n_past, 1.0, 0.0))
        rows = buf[slot, h * k_sel:(h + 1) * k_sel].reshape(k_sel * SEL_BLOCK, KV_WIDTH).astype(BF16)
        s = jnp.where(past > 0.5, _nt_dot(qs, rows[:, k_lo:k_lo + HEAD_DIM]), NEG)
        s_new = jnp.where(has_new > 0.5, jnp.sum(qf * ks_new[:, k_lo:k_lo + HEAD_DIM], axis=-1, keepdims=True), NEG)
        m = jnp.maximum(jnp.max(s, axis=-1, keepdims=True), s_new)
        p = jnp.where(past > 0.5, jnp.exp(s - m), 0.0)
        p_new = jnp.where(has_new > 0.5, jnp.exp(s_new - m), 0.0)
        inv = 1.0 / (jnp.sum(p, axis=-1, keepdims=True) + p_new)
        o_s = (_dot(p.astype(BF16), rows[:, v_lo:v_lo + HEAD_DIM])
               + p_new.astype(BF16).astype(F32) * ks_new[:, v_lo:v_lo + HEAD_DIM]) * inv
        s = _nt_dot(qs, win_bf[:, k_lo:k_lo + HEAD_DIM])
        s_new = jnp.sum(qf * kw_new[:, k_lo:k_lo + HEAD_DIM], axis=-1, keepdims=True)
        m = jnp.maximum(jnp.max(s, axis=-1, keepdims=True), s_new)
        p, p_new = jnp.exp(s - m), jnp.exp(s_new - m)
        inv = 1.0 / (jnp.sum(p, axis=-1, keepdims=True) + p_new)
        o_w = (_dot(p.astype(BF16), win_bf[:, v_lo:v_lo + HEAD_DIM])
               + p_new.astype(BF16).astype(F32) * kw_new[:, v_lo:v_lo + HEAD_DIM]) * inv
        for g in range(GQA):
            hd = GQA * h + g
            col = hd * 3
            head_out.append(gate[:, col:col + 1] * o_c[:, hd * HEAD_DIM:(hd + 1) * HEAD_DIM]
                            + gate[:, col + 1:col + 2] * o_s[g:g + 1] + gate[:, col + 2:col + 3] * o_w[g:g + 1])
    o_ref[0] = jnp.concatenate(head_out, axis=-1).astype(BF16)
    row = lax.broadcasted_iota(jnp.int32, (n_win, 1), 0)
    wout_ref[0] = jnp.where(row == n_win - 1, kwn_ref[0], pltpu.roll(win, n_win - 1, 0))


def _attn_sample(idx, page_table, pool, qb, gl, o_c, kvs_new, kvw_new, win, k_sel):
    n_seq, n_pages = page_table.shape
    n_win = win.shape[1]
    n_blk = N_KV_HEADS * k_sel
    one = lambda w: pl.BlockSpec((1, 1, w), lambda i, a, p: (i, 0, 0))
    win_spec = pl.BlockSpec((1, n_win, KV_WIDTH), lambda i, a, p: (i, 0, 0))
    grid_spec = pltpu.PrefetchScalarGridSpec(
        num_scalar_prefetch=2,
        grid=(n_seq,),
        in_specs=[pl.BlockSpec(memory_space=pl.ANY), one(NSA_WIDTH), one(GATE_PAD), one(NSA_WIDTH), one(KV_WIDTH),
                  one(KV_WIDTH), win_spec],
        out_specs=[one(NSA_WIDTH), win_spec],
        scratch_shapes=[pltpu.VMEM((2, n_blk, SEL_BLOCK, KV_WIDTH), F32), pltpu.SemaphoreType.DMA((2,))],
    )
    return pl.pallas_call(
        functools.partial(_attn_sample_body, n_pages=n_pages, n_seq=n_seq, k_sel=k_sel),
        grid_spec=grid_spec,
        out_shape=[jax.ShapeDtypeStruct((n_seq, 1, NSA_WIDTH), BF16),
                   jax.ShapeDtypeStruct((n_seq, n_win, KV_WIDTH), F32)],
        compiler_params=_cparams("arbitrary"),
        name="attn_sample",
    )(idx, page_table.reshape(-1), pool, qb, gl, o_c, kvs_new, kvw_new, win)


def _tail_sample_body(x_ref, o_ref, u_ref, h0r_ref, h0i_ref, hist2_ref, hist1_ref, bb_ref, cc_ref, d_ref, abr_ref,
                      abi_ref, wglu_ref, bglu_ref, wout_ref, nf_ref, wup_ref, cw_ref, cb_ref, wdown_ref, nfin_ref,
                      out_ref, hr_ref, hi_ref, cs_ref):
    u = u_ref[...]
    bu = _split_dot(u, bb_ref[...])
    ar, ai = abr_ref[...], abi_ref[...]
    h0r, h0i = h0r_ref[...], h0i_ref[...]
    hr = ar * h0r - ai * h0i + bu[:, :SSM_N]
    hi = ar * h0i + ai * h0r + bu[:, SSM_N:]
    hr_ref[...] = hr
    hi_ref[...] = hi
    y = _dot(hr.astype(BF16), cc_ref[:SSM_N, :]) + _dot(hi.astype(BF16), cc_ref[SSM_N:, :]) + d_ref[...] * u
    x1 = _mix_out(x_ref[...], o_ref[...], y, wglu_ref, bglu_ref, wout_ref)
    hn = _rms(x1, nf_ref[...]).astype(BF16)

    def prev_rows(lo, hi_col, hu):
        cs_ref[:, lo:hi_col] = hist1_ref[:, lo:hi_col]
        cs_ref[:, 2 * D_FF + lo:2 * D_FF + hi_col] = hu
        return hist2_ref[:, lo:hi_col], hist1_ref[:, lo:hi_col]

    x2 = x1 + _ffn_chunks(hn, wup_ref, cw_ref, cb_ref, wdown_ref, prev_rows)
    out_ref[...] = _rms(x2, nfin_ref[...])


def _tail_sample(x, o_nsa, u, h0r, h0i, hist2, hist1, bb, cc, d, ab_re, ab_im, w):
    n = x.shape[0]
    sds = lambda width: jax.ShapeDtypeStruct((n, width), F32)
    return pl.pallas_call(
        _tail_sample_body,
        out_shape=[sds(D_MODEL), sds(SSM_N), sds(SSM_N), sds((CONV_W - 1) * 2 * D_FF)],
        compiler_params=pltpu.CompilerParams(vmem_limit_bytes=VMEM_LIMIT),
        name="tail_sample",
    )(x, o_nsa, u, h0r, h0i, hist2, hist1, bb, cc, d, ab_re, ab_im, w["w_glu"], w["b_glu"], w["w_out"],
      w["norm_ffn"], w["w_up"], w["conv_w"], w["conv_b"], w["w_down"], w["norm_final"])


def _pad_w_in(w_in):
    c = NSA_WIDTH + 3 * KV_WIDTH
    return jnp.concatenate([w_in[:, :c], w_in[:, c + N_GATES:], w_in[:, c:c + N_GATES],
                            jnp.zeros((D_MODEL, GATE_PAD - N_GATES), w_in.dtype)], axis=1).astype(BF16)


def _cmp_weight(w_cmp):
    w = w_cmp.reshape(2, CMP_LEN // CMP_STRIDE, CMP_STRIDE, HEAD_DIM, HEAD_DIM)
    eye_x = jnp.eye(2, dtype=w.dtype)
    eye_h = jnp.eye(N_KV_HEADS, dtype=w.dtype)
    big = (w.transpose(2, 0, 3, 1, 4)[:, :, None, :, :, None, None, :]
           * eye_x[None, :, None, None, None, :, None, None] * eye_h[None, None, :, None, None, None, :, None])
    return big.reshape(SUB_W, 2 * KV_WIDTH).astype(BF16)


def _pe_sub(pe_cmp):
    rows = jnp.broadcast_to(pe_cmp.transpose(1, 0, 2)[:, :, None, :], (CMP_LEN, 2, N_KV_HEADS, HEAD_DIM))
    sub = rows.reshape(CMP_LEN // CMP_STRIDE, SUB_W)
    return jnp.concatenate([sub, jnp.zeros((8 - sub.shape[0], SUB_W), sub.dtype)], axis=0)


def _sel_map(n_cmp, n_sel, n_sel_pad):
    c0 = (jnp.arange(n_cmp) * CMP_STRIDE)[:, None]
    s0 = (jnp.arange(n_sel_pad) * SEL_BLOCK)[None, :]
    hit = (c0 < s0 + SEL_BLOCK) & (c0 + CMP_LEN > s0) & (jnp.arange(n_sel_pad)[None, :] < n_sel)
    return hit.astype(BF16)


def _expand_map(n_sel, t, kc_len):
    hit = jnp.arange(t)[None, :] // SEL_BLOCK == jnp.arange(n_sel)[:, None]
    return hit.astype(BF16).reshape(n_sel, t // kc_len, kc_len).transpose(1, 0, 2)


def kernel(x_prompt, x_sample, cache_kv_cmp, cache_kv_sel, cache_kv_win, state_ssm_re, state_ssm_im, state_ffn_conv, page_table, norm_mix, w_in, pe_cmp, w_cmp, ssm_a_re, ssm_a_im, ssm_log_dt, ssm_b_re, ssm_b_im, ssm_c_re, ssm_c_im, ssm_d, w_glu, b_glu, w_out, norm_ffn, w_up, conv_w, conv_b, w_down, norm_final):
    depth = w_in.shape[0]
    assert depth == 1, "single-layer trunk"
    b, t, _ = x_prompt.shape
    bd, s, _ = x_sample.shape
    assert s == 1, "one new position per sample sequence"
    n_pages = page_table.shape[1]
    n_pool = cache_kv_cmp.shape[1]
    l = 0
    w_pad = _pad_w_in(w_in[l])
    g_mix = norm_mix[l].reshape(1, D_MODEL)
    w_big = _cmp_weight(w_cmp[l])
    pe_sub = _pe_sub(pe_cmp[l])
    ab_re, ab_im, bb_re, bb_im = _ssm_params(ssm_log_dt[l], ssm_a_re[l], ssm_a_im[l], ssm_b_re[l], ssm_b_im[l])
    bb = jnp.concatenate([_block_diag_in(bb_re), _block_diag_in(bb_im)], axis=1).astype(BF16)
    cc = jnp.concatenate([_block_diag_out(ssm_c_re[l]), -_block_diag_out(ssm_c_im[l])], axis=0).astype(BF16)
    d_row = ssm_d[l].reshape(1, SSM_WIDTH)
    tail_w = {"w_glu": w_glu[l].astype(BF16), "b_glu": b_glu[l].reshape(1, -1), "w_out": w_out[l].astype(BF16),
              "norm_ffn": norm_ffn[l].reshape(1, -1), "w_up": w_up[l].astype(BF16), "conv_w": conv_w[l],
              "conv_b": conv_b[l].reshape(1, -1), "w_down": w_down[l].astype(BF16),
              "norm_final": norm_final.reshape(1, -1)}

    kvc, kvs, kvw, gl, u, qb, kvsb, kvwb = _inproj(x_prompt.reshape(b * t, D_MODEL), g_mix, w_pad, 512)
    n_sub = t // CMP_STRIDE
    kc = _cmp_prompt(kvc.reshape(b, n_sub, SUB_W), pe_sub, w_big)
    n_sel = t // SEL_BLOCK
    o_nsa = _attn_prompt(qb.reshape(b, t, -1), gl.reshape(b, t, -1), kc, kvsb.reshape(b, t, -1),
                         kvwb.reshape(b, t, -1), _sel_map(n_sub, n_sel, n_sel))
    y_ssm, p_hr, p_hi = _ssm_prompt(u.reshape(b, t, -1), bb, cc, d_row, ab_re, ab_im)
    y_prompt, p_conv = _tail_prompt(x_prompt, o_nsa, y_ssm, tail_w)
    kv_shape = (depth, b, t, 2, N_KV_HEADS, HEAD_DIM)
    win_keep = min(WINDOW, t)
    p_kv_win = kvw.reshape(b, t, KV_WIDTH)[:, t - win_keep:].reshape(depth, b, win_keep, 2, N_KV_HEADS, HEAD_DIM)
    st_shape = (depth, b, SSM_GROUPS, SSM_STATE)

    kvc_n, kvs_n, kvw_n, gl_n, u_n, qb_n, _, _ = _inproj(x_sample.reshape(bd, D_MODEL), g_mix, w_pad, bd)
    n_sub_s = n_pages * (PAGE_SIZE // CMP_STRIDE)
    n_sel_s = n_pages * (PAGE_SIZE // SEL_BLOCK) + 1
    n_sel_pad = -(-n_sel_s // LANES) * LANES
    k_sel = min(N_SELECT, n_sel_s)
    pool_c = cache_kv_cmp[l].reshape(n_pool, PAGE_SIZE // CMP_STRIDE, SUB_W)
    o_c, picked = _cmp_sample(page_table, pool_c, kvc_n.reshape(bd, 1, -1), qb_n.reshape(bd, 1, -1), pe_sub, w_big,
                              _sel_map(n_sub_s, n_sel_s, n_sel_pad))
    idx = picked[:, :N_KV_HEADS, :k_sel].reshape(-1)
    pool_s = cache_kv_sel[l].reshape(n_pool, PAGE_SIZE // SEL_BLOCK, SEL_BLOCK, KV_WIDTH)
    n_buf = cache_kv_win.shape[2]
    assert n_buf == WINDOW, "window buffer holds exactly WINDOW rows"
    o_nsa_s, s_win = _attn_sample(idx, page_table, pool_s, qb_n.reshape(bd, 1, -1), gl_n.reshape(bd, 1, -1), o_c,
                                  kvs_n.reshape(bd, 1, -1), kvw_n.reshape(bd, 1, -1),
                                  cache_kv_win[l].reshape(bd, n_buf, KV_WIDTH), k_sel)
    hist = state_ffn_conv[l]
    y_sample, s_hr, s_hi, s_conv = _tail_sample(
        x_sample.reshape(bd, D_MODEL), o_nsa_s.reshape(bd, -1), u_n, state_ssm_re[l].reshape(bd, SSM_N),
        state_ssm_im[l].reshape(bd, SSM_N), hist[:, 0], hist[:, 1], bb, cc, d_row, ab_re, ab_im, tail_w)
    kv_shape_s = (depth, bd, s, 2, N_KV_HEADS, HEAD_DIM)
    st_shape_s = (depth, bd, SSM_GROUPS, SSM_STATE)
    return (y_prompt, y_sample.reshape(bd, s, D_MODEL),
            kvc.reshape(kv_shape), kvs.reshape(kv_shape), p_kv_win,
            p_hr.reshape(st_shape), p_hi.reshape(st_shape), p_conv.reshape(depth, b, CONV_W - 1, 2 * D_FF),
            kvc_n.reshape(kv_shape_s), kvs_n.reshape(kv_shape_s),
            s_win.reshape(depth, bd, n_buf, 2, N_KV_HEADS, HEAD_DIM),
            s_hr.reshape(st_shape_s), s_hi.reshape(st_shape_s), s_conv.reshape(depth, bd, CONV_W - 1, 2 * D_FF))
```

```python
import functools
import math

import jax
import jax.numpy as jnp
from jax import lax
from jax.experimental import pallas as pl
from jax.experimental.pallas import tpu as pltpu

D_MODEL = 1024
N_HEADS = 8
N_KV_HEADS = 2
GQA = N_HEADS // N_KV_HEADS
HEAD_DIM = 64
NSA_WIDTH = N_HEADS * HEAD_DIM
KV_WIDTH = 2 * N_KV_HEADS * HEAD_DIM
N_GATES = 3 * N_HEADS
CMP_LEN = 32
CMP_STRIDE = 16
SEL_BLOCK = 64
N_SELECT = 16
WINDOW = 512
Q_BLOCK = 128
PAGE_SIZE = 128
SSM_WIDTH = D_MODEL - NSA_WIDTH
SSM_GROUP = 16
SSM_GROUPS = SSM_WIDTH // SSM_GROUP
SSM_STATE = 64
SSM_N = SSM_GROUPS * SSM_STATE
D_FF = (D_MODEL * 11 // 4 + 127) // 128 * 128
CONV_W = 3
EPS = 1e-6
NEG = -1e30
BIG = 1e30
BELOW_NEG = -3e38

LANES = 128
GATE_PAD = LANES
IN_PAD = NSA_WIDTH + 3 * KV_WIDTH + SSM_WIDTH + GATE_PAD
SUB_W = CMP_STRIDE * KV_WIDTH
VMEM_LIMIT = 56 * 1024 * 1024

F32 = jnp.float32
BF16 = jnp.bfloat16


def _nt_dot(a, b):
    return lax.dot_general(a, b, (((1,), (1,)), ((), ())), preferred_element_type=F32)


def _dot(a, b):
    return jnp.dot(a, b, preferred_element_type=F32)


def _sigmoid(x):
    return 1.0 / (1.0 + jnp.exp(-x))


def _gelu_tanh(x):
    return 0.5 * x * (1.0 + jnp.tanh(math.sqrt(2.0 / math.pi) * (x + 0.044715 * (x * x * x))))


def _rms(x, g):
    return x * lax.rsqrt(jnp.mean(x * x, axis=-1, keepdims=True) + EPS) * g


def _split_dot(x, w_bf):
    hi = x.astype(BF16)
    lo = (x - hi.astype(F32)).astype(BF16)
    return _dot(hi, w_bf) + _dot(lo, w_bf)


def _cparams(*sem):
    return pltpu.CompilerParams(dimension_semantics=sem, vmem_limit_bytes=VMEM_LIMIT)


def _const_spec(shape):
    nd = len(shape)
    return pl.BlockSpec(shape, lambda *_: (0,) * nd, pipeline_mode=pl.Buffered(1))


def _inproj_body(x_ref, g_ref, w_ref, kvc_ref, kvs_ref, kvw_ref, kvct_ref, kvst_ref, kvwt_ref, kstb_ref, kwtb_ref,
                 gl_ref, u_ref, qb_ref):
    h = _rms(x_ref[0], g_ref[...])
    z = _dot(h.astype(BF16), w_ref[...])
    tm = z.shape[0]
    c = NSA_WIDTH
    qb_ref[0] = (z[:, :c] * (HEAD_DIM ** -0.5)).astype(BF16)
    for rm_ref, t_ref, tb_ref in ((kvc_ref, kvct_ref, None), (kvs_ref, kvst_ref, kstb_ref),
                                  (kvw_ref, kvwt_ref, kwtb_ref)):
        rows = z[:, c:c + KV_WIDTH]
        c += KV_WIDTH
        rm_ref[0] = rows
        cols = rows.T
        t_ref[0] = cols
        if tb_ref is not None:
            cols_bf = cols.astype(BF16)
            for k in range(tm // LANES):
                tb_ref[0, k] = cols_bf[:, k * LANES:(k + 1) * LANES]
    u_ref[0] = z[:, c:c + SSM_WIDTH]
    gl_ref[0] = z[:, c + SSM_WIDTH:]


def _inproj(x, g, w_pad, tm):
    b, t, _ = x.shape
    row = lambda w: pl.BlockSpec((1, tm, w), lambda i, j: (i, j, 0))
    col = pl.BlockSpec((1, KV_WIDTH, tm), lambda i, j: (i, 0, j))
    chunk = pl.BlockSpec((1, tm // LANES, KV_WIDTH, LANES), lambda i, j: (i, j, 0, 0))
    rm_t = jax.ShapeDtypeStruct((b, t, KV_WIDTH), F32)
    col_t = jax.ShapeDtypeStruct((b, KV_WIDTH, t), F32)
    chunk_t = jax.ShapeDtypeStruct((b, t // LANES, KV_WIDTH, LANES), BF16)
    return pl.pallas_call(
        _inproj_body,
        grid=(b, t // tm),
        in_specs=[row(D_MODEL), _const_spec((1, D_MODEL)), _const_spec((D_MODEL, IN_PAD))],
        out_specs=[row(KV_WIDTH)] * 3 + [col] * 3 + [chunk] * 2 + [row(GATE_PAD), row(SSM_WIDTH), row(NSA_WIDTH)],
        out_shape=[rm_t] * 3 + [col_t] * 3 + [chunk_t] * 2
        + [jax.ShapeDtypeStruct((b, t, GATE_PAD), F32), jax.ShapeDtypeStruct((b, t, SSM_WIDTH), F32),
           jax.ShapeDtypeStruct((b, t, NSA_WIDTH), BF16)],
        compiler_params=_cparams("parallel", "parallel"),
        name="in_proj",
    )(x, g, w_pad)


def _cmp_prompt_body(sub_ref, pe_ref, w_ref, kc_ref, bias_ref):
    w = w_ref[...]
    parts = _dot(sub_ref[0].astype(BF16), w)
    pe = _dot(pe_ref[...].astype(BF16), w)
    bias = pe[0:1, :KV_WIDTH] + pe[1:2, KV_WIDTH:]
    n_sub = parts.shape[0]
    nxt = pltpu.roll(parts[:, KV_WIDTH:], n_sub - 1, 0)
    kc_ref[0] = (parts[:, :KV_WIDTH] + nxt + bias).astype(BF16)
    bias_ref[...] = jnp.broadcast_to(bias, bias_ref.shape)


def _cmp_prompt(sub, pe_sub, w_big):
    b, n_sub, _ = sub.shape
    return pl.pallas_call(
        _cmp_prompt_body,
        grid=(b,),
        in_specs=[pl.BlockSpec((1, n_sub, SUB_W), lambda i: (i, 0, 0)), _const_spec(pe_sub.shape),
                  _const_spec(w_big.shape)],
        out_specs=[pl.BlockSpec((1, n_sub, KV_WIDTH), lambda i: (i, 0, 0)),
                   pl.BlockSpec((8, KV_WIDTH), lambda i: (0, 0))],
        out_shape=[jax.ShapeDtypeStruct((b, n_sub, KV_WIDTH), BF16), jax.ShapeDtypeStruct((8, KV_WIDTH), F32)],
        compiler_params=_cparams("arbitrary"),
        name="cmp_prompt",
    )(sub, pe_sub, w_big)


def _stack_heads(q, h):
    return jnp.concatenate([q[:, (GQA * h + g) * HEAD_DIM:(GQA * h + g + 1) * HEAD_DIM] for g in range(GQA)], axis=0)


def _stack_heads_single(q, h):
    row = lax.broadcasted_iota(jnp.int32, (8, 1), 0)
    q8 = jnp.broadcast_to(q, (8, NSA_WIDTH))
    out = jnp.zeros((8, HEAD_DIM), F32)
    for g in range(GQA):
        lo = (GQA * h + g) * HEAD_DIM
        out = out + jnp.where(row == g, q8[:, lo:lo + HEAD_DIM], 0.0)
    return out.astype(BF16)


def _masked_softmax_rows(s, mask):
    sm = jnp.where(mask, s, NEG)
    m = jnp.max(sm, axis=-1, keepdims=True)
    p = jnp.where(mask, jnp.exp(sm - m), 0.0)
    l = jnp.sum(p, axis=-1, keepdims=True)
    return p * (1.0 / jnp.where(l > 0.0, l, 1.0))


def _block_scores(imp, blk, qpos):
    first = blk * SEL_BLOCK
    own_or_imp = jnp.where(first + SEL_BLOCK > qpos, BIG, imp)
    return jnp.where(blk == 0, BIG, jnp.where(first <= qpos, own_or_imp, NEG))


def _masked_softmax_cols(s, mask):
    sm = jnp.where(mask, s, NEG)
    m = jnp.max(sm, axis=0, keepdims=True)
    p = jnp.where(mask, jnp.exp(sm - m), 0.0)
    l = jnp.sum(p, axis=0, keepdims=True)
    return p * (1.0 / jnp.where(l > 0.0, l, 1.0))


def _top_k_mask_cols(score, blk, k):
    n = float(score.shape[0])
    sel = jnp.zeros(score.shape, F32)
    for _ in range(k):
        mx = jnp.max(score, axis=0, keepdims=True)
        idx = jnp.min(jnp.where(score == mx, blk, n), axis=0, keepdims=True)
        hit = blk == idx
        sel = sel + jnp.where(hit, jnp.where(mx > 0.5 * NEG, 1.0, 0.0), 0.0)
        score = jnp.where(hit, BELOW_NEG, score)
    return sel


KEY_BLOCKS = 4


def _attn_prompt_body(q_ref, gl_ref, kc_ref, kst_ref, kwt_ref, selmap_ref, exp_ref, o_ref, m_s, l_s, acc_s):
    qb_idx = pl.program_id(1)
    start = qb_idx * Q_BLOCK
    kc_len = KEY_BLOCKS * LANES
    q = q_ref[0]
    gate = _sigmoid(gl_ref[0])
    qpos = start + lax.broadcasted_iota(jnp.int32, (Q_BLOCK, 1), 0)
    qpos4 = jnp.concatenate([qpos] * GQA, axis=0)
    qpos_l = start + lax.broadcasted_iota(jnp.int32, (1, Q_BLOCK), 1)
    qpos4_l = jnp.concatenate([qpos_l] * GQA, axis=1)
    n_cmp = kc_ref.shape[1]
    n_sel = selmap_ref.shape[0]
    cmp_end = lax.broadcasted_iota(jnp.int32, (n_cmp, 1), 0) * CMP_STRIDE + (CMP_LEN - 1)
    blk = lax.broadcasted_iota(jnp.int32, (n_sel, 1), 0)
    qs, o_c, scores = [], [], []
    for h in range(N_KV_HEADS):
        k_lo, v_lo = h * HEAD_DIM, (N_KV_HEADS + h) * HEAD_DIM
        qs.append(_stack_heads(q, h))
        p_t = _masked_softmax_cols(_nt_dot(kc_ref[0, :, k_lo:k_lo + HEAD_DIM], qs[h]), cmp_end <= qpos4_l)
        o_c.append(_dot(p_t.T.astype(BF16), kc_ref[0, :, v_lo:v_lo + HEAD_DIM]))
        p_sum = p_t[:, 0:Q_BLOCK]
        for g in range(1, GQA):
            p_sum = p_sum + p_t[:, g * Q_BLOCK:(g + 1) * Q_BLOCK]
        hi = p_sum.astype(BF16)
        lo = (p_sum - hi.astype(F32)).astype(BF16)
        imp = _dot(selmap_ref[...], hi) + _dot(selmap_ref[...], lo)
        scores.append(_block_scores(imp, blk, qpos_l))
    sel_t = _top_k_mask_cols(jnp.concatenate(scores, axis=1), blk.astype(F32), min(N_SELECT, n_sel))
    head_out = []
    for h in range(N_KV_HEADS):
        k_lo, v_lo = h * HEAD_DIM, (N_KV_HEADS + h) * HEAD_DIM
        sel = sel_t[:, h * Q_BLOCK:(h + 1) * Q_BLOCK].T.astype(BF16)
        m_s[...] = jnp.full(m_s.shape, NEG, F32)
        l_s[...] = jnp.zeros(l_s.shape, F32)
        acc_s[...] = jnp.zeros(acc_s.shape, F32)

        def sel_chunk(c, carry):
            k_t = jnp.concatenate([kst_ref[0, c * KEY_BLOCKS + j, k_lo:k_lo + HEAD_DIM, :]
                                   for j in range(KEY_BLOCKS)], axis=1)
            v_t = jnp.concatenate([kst_ref[0, c * KEY_BLOCKS + j, v_lo:v_lo + HEAD_DIM, :]
                                   for j in range(KEY_BLOCKS)], axis=1)
            s = _dot(qs[h], k_t)
            chosen = _dot(sel, exp_ref[c])
            kpos = c * kc_len + lax.broadcasted_iota(jnp.int32, (1, kc_len), 1)
            vis = jnp.where(kpos <= qpos, chosen, 0.0)
            mask = jnp.concatenate([vis] * GQA, axis=0) > 0.5
            sm = jnp.where(mask, s, NEG)
            m_prev = m_s[...]
            m_next = jnp.maximum(m_prev, jnp.max(sm, axis=-1, keepdims=True))
            p = jnp.where(mask, jnp.exp(sm - jnp.tile(m_next, (1, KEY_BLOCKS))), 0.0)
            alpha = jnp.exp(m_prev - m_next)
            l_s[...] = alpha * l_s[...] + jnp.sum(p, axis=-1, keepdims=True)
            acc_s[...] = acc_s[...] * alpha[:, :HEAD_DIM] + _nt_dot(p.astype(BF16), v_t)
            m_s[...] = m_next
            return carry

        lax.fori_loop(0, (start + Q_BLOCK + kc_len - 1) // kc_len, sel_chunk, 0)
        o_s = acc_s[...] * (1.0 / l_s[...])[:, :HEAD_DIM]
        kw, vw = [], []
        for c in range((WINDOW + Q_BLOCK) // LANES):
            src = jnp.maximum(qb_idx - WINDOW // LANES + c, 0)
            kw.append(kwt_ref[0, src, k_lo:k_lo + HEAD_DIM, :])
            vw.append(kwt_ref[0, src, v_lo:v_lo + HEAD_DIM, :])
        s = _dot(qs[h], jnp.concatenate(kw, axis=1))
        wpos = start - WINDOW + lax.broadcasted_iota(jnp.int32, (1, WINDOW + Q_BLOCK), 1)
        sm = jnp.where(wpos <= qpos4, jnp.where(wpos >= jnp.maximum(qpos4 - WINDOW, 0), s, NEG), NEG)
        p = jnp.exp(sm - jnp.max(sm, axis=-1, keepdims=True))
        p = p * (1.0 / jnp.sum(p, axis=-1, keepdims=True))
        o_w = _nt_dot(p.astype(BF16), jnp.concatenate(vw, axis=1))
        for g in range(GQA):
            col = (GQA * h + g) * 3
            r = slice(g * Q_BLOCK, (g + 1) * Q_BLOCK)
            head_out.append(gate[:, col:col + 1] * o_c[h][r] + gate[:, col + 1:col + 2] * o_s[r]
                            + gate[:, col + 2:col + 3] * o_w[r])
    o_ref[0] = jnp.concatenate(head_out, axis=-1).astype(BF16)


def _attn_prompt(qb, gl, kc, kst, kwt, selmap_t):
    b, t, _ = qb.shape
    expand = _expand_map(selmap_t.shape[0], t, KEY_BLOCKS * LANES)
    rows = GQA * Q_BLOCK
    whole = lambda a: pl.BlockSpec((1,) + a.shape[1:], lambda i, j: (i,) + (0,) * (a.ndim - 1))
    return pl.pallas_call(
        _attn_prompt_body,
        grid=(b, t // Q_BLOCK),
        in_specs=[pl.BlockSpec((1, Q_BLOCK, NSA_WIDTH), lambda i, j: (i, j, 0)),
                  pl.BlockSpec((1, Q_BLOCK, GATE_PAD), lambda i, j: (i, j, 0)),
                  whole(kc), whole(kst), whole(kwt), _const_spec(selmap_t.shape), _const_spec(expand.shape)],
        out_specs=pl.BlockSpec((1, Q_BLOCK, NSA_WIDTH), lambda i, j: (i, j, 0)),
        out_shape=jax.ShapeDtypeStruct((b, t, NSA_WIDTH), BF16),
        scratch_shapes=[pltpu.VMEM((rows, LANES), F32), pltpu.VMEM((rows, LANES), F32),
                        pltpu.VMEM((rows, HEAD_DIM), F32)],
        compiler_params=_cparams("parallel", "arbitrary"),
        name="attn_prompt",
    )(qb, gl, kc, kst, kwt, selmap_t, expand)


def _ssm_param_body(ldt_ref, are_ref, aim_ref, bre_ref, bim_ref, abr_ref, abi_ref, bbr_ref, bbi_ref):
    dt = jnp.exp(ldt_ref[...])
    are, aim = are_ref[...], aim_ref[...]
    mag = jnp.exp(dt * are)
    ab_re, ab_im = mag * jnp.cos(dt * aim), mag * jnp.sin(dt * aim)
    den = are * are + aim * aim
    zr, zi = ab_re - 1.0, ab_im
    f_re = (zr * are + zi * aim) / den
    f_im = (zi * are - zr * aim) / den
    abr_ref[...] = ab_re
    abi_ref[...] = ab_im
    bbr_ref[...] = f_re * bre_ref[...] - f_im * bim_ref[...]
    bbi_ref[...] = f_re * bim_ref[...] + f_im * bre_ref[...]


def _ssm_params(log_dt, a_re, a_im, b_re, b_im):
    col = lambda a: a.reshape(SSM_N, 1)
    ldt = col(jnp.broadcast_to(log_dt[:, None], (SSM_GROUPS, SSM_STATE)))
    col_t = jax.ShapeDtypeStruct((SSM_N, 1), F32)
    mat_t = jax.ShapeDtypeStruct((SSM_N, SSM_GROUP), F32)
    ab_re, ab_im, bb_re, bb_im = pl.pallas_call(
        _ssm_param_body, out_shape=[col_t, col_t, mat_t, mat_t], name="ssm_params",
    )(ldt, col(a_re), col(a_im), b_re.reshape(SSM_N, SSM_GROUP), b_im.reshape(SSM_N, SSM_GROUP))
    return ab_re.reshape(1, SSM_N), ab_im.reshape(1, SSM_N), bb_re, bb_im


def _block_diag_in(bb):
    m = bb.reshape(SSM_GROUPS, SSM_STATE, SSM_GROUP).transpose(0, 2, 1)
    eye = jnp.eye(SSM_GROUPS, dtype=bb.dtype)
    return (eye[:, None, :, None] * m[:, :, None, :]).reshape(SSM_WIDTH, SSM_N)


def _block_diag_out(c):
    m = c.transpose(0, 2, 1)
    eye = jnp.eye(SSM_GROUPS, dtype=c.dtype)
    return (eye[:, None, :, None] * m[:, :, None, :]).reshape(SSM_N, SSM_WIDTH)


def _ssm_prompt_body(u_ref, bb_ref, cc_ref, d_ref, abr_ref, abi_ref, y_ref, hr_ref, hi_ref, bu_s, hs_s, st_s):
    tc = u_ref.shape[1]

    @pl.when(pl.program_id(1) == 0)
    def _():
        st_s[...] = jnp.zeros(st_s.shape, F32)

    u = u_ref[0]
    bu_s[...] = _dot(u.astype(BF16), bb_ref[...])
    ar, ai = abr_ref[...], abi_ref[...]

    def step(t, carry):
        hr, hi = carry
        nr = ar * hr - ai * hi + bu_s[pl.ds(t, 1), :SSM_N]
        ni = ar * hi + ai * hr + bu_s[pl.ds(t, 1), SSM_N:]
        hs_s[pl.ds(t, 1), :SSM_N] = nr
        hs_s[pl.ds(t, 1), SSM_N:] = ni
        return nr, ni

    hr, hi = lax.fori_loop(0, tc, step, (st_s[0:1, :], st_s[1:2, :]), unroll=8)
    st_s[0:1, :] = hr
    st_s[1:2, :] = hi
    y_ref[0] = _dot(hs_s[...].astype(BF16), cc_ref[...]) + d_ref[...] * u
    hr_ref[0] = hr
    hi_ref[0] = hi


def _ssm_prompt(u, bb, cc, d, ab_re, ab_im, tc=256):
    b, t, _ = u.shape
    st = jax.ShapeDtypeStruct((b, 1, SSM_N), F32)
    st_spec = pl.BlockSpec((1, 1, SSM_N), lambda i, j: (i, 0, 0))
    return pl.pallas_call(
        _ssm_prompt_body,
        grid=(b, t // tc),
        in_specs=[pl.BlockSpec((1, tc, SSM_WIDTH), lambda i, j: (i, j, 0)), _const_spec(bb.shape),
                  _const_spec(cc.shape), _const_spec(d.shape), _const_spec(ab_re.shape), _const_spec(ab_im.shape)],
        out_specs=[pl.BlockSpec((1, tc, SSM_WIDTH), lambda i, j: (i, j, 0)), st_spec, st_spec],
        out_shape=[jax.ShapeDtypeStruct((b, t, SSM_WIDTH), F32), st, st],
        scratch_shapes=[pltpu.VMEM((tc, 2 * SSM_N), F32), pltpu.VMEM((tc, 2 * SSM_N), F32),
                        pltpu.VMEM((8, SSM_N), F32)],
        compiler_params=_cparams("parallel", "arbitrary"),
        name="ssm_prompt",
    )(u, bb, cc, d, ab_re, ab_im)


FF_CHUNK = 256


def _mix_out(x, o_nsa, y_ssm, wglu_ref, bglu_ref, wout_ref):
    z = _dot(_gelu_tanh(y_ssm).astype(BF16), wglu_ref[...]) + bglu_ref[...]
    glu = z[:, :SSM_WIDTH] * _sigmoid(z[:, SSM_WIDTH:])
    return x + _dot(o_nsa, wout_ref[:NSA_WIDTH, :]) + _dot(glu.astype(BF16), wout_ref[NSA_WIDTH:, :])


def _ffn_chunks(hn, wup_ref, cw_ref, cb_ref, wdown_ref, prev_rows):
    acc = jnp.zeros((hn.shape[0], D_MODEL), F32)
    for j in range(D_FF // FF_CHUNK):
        conv = []
        for base in (0, D_FF):
            lo = base + j * FF_CHUNK
            hi = lo + FF_CHUNK
            hu = _dot(hn, wup_ref[:, lo:hi])
            hu2, hu1 = prev_rows(lo, hi, hu)
            conv.append(cw_ref[0:1, lo:hi] * hu2 + cw_ref[1:2, lo:hi] * hu1 + cw_ref[2:3, lo:hi] * hu
                        + cb_ref[:, lo:hi])
        a, g = conv
        act = (a * _sigmoid(a) * g).astype(BF16)
        acc = acc + _dot(act, wdown_ref[j * FF_CHUNK:(j + 1) * FF_CHUNK, :])
    return acc


def _tail_prompt_body(x_ref, o_ref, y_ref, wglu_ref, bglu_ref, wout_ref, nf_ref, wup_ref, cw_ref, cb_ref, wdown_ref,
                      nfin_ref, out_ref, cs_ref, prev_s):
    tm = x_ref.shape[1]

    @pl.when(pl.program_id(1) == 0)
    def _():
        prev_s[...] = jnp.zeros(prev_s.shape, F32)

    x1 = _mix_out(x_ref[0], o_ref[0], y_ref[0], wglu_ref, bglu_ref, wout_ref)
    hn = _rms(x1, nf_ref[...]).astype(BF16)
    row = lax.broadcasted_iota(jnp.int32, (tm, 1), 0)

    def prev_rows(lo, hi, hu):
        p2, p1 = prev_s[6:7, lo:hi], prev_s[7:8, lo:hi]
        hu1 = jnp.where(row == 0, p1, pltpu.roll(hu, 1, 0))
        hu2 = jnp.where(row == 0, p2, jnp.where(row == 1, p1, pltpu.roll(hu, 2, 0)))
        prev_s[:, lo:hi] = hu[tm - 8:, :]
        cs_ref[0, :, lo:hi] = hu[tm - (CONV_W - 1):, :]
        return hu2, hu1

    x2 = x1 + _ffn_chunks(hn, wup_ref, cw_ref, cb_ref, wdown_ref, prev_rows)
    out_ref[0] = _rms(x2, nfin_ref[...])


def _tail_prompt(x, o_nsa, y_ssm, w, tm=512):
    b, t, _ = x.shape
    tile = lambda width: pl.BlockSpec((1, tm, width), lambda i, j: (i, j, 0))
    consts = [w["w_glu"], w["b_glu"], w["w_out"], w["norm_ffn"], w["w_up"], w["conv_w"], w["conv_b"], w["w_down"],
              w["norm_final"]]
    return pl.pallas_call(
        _tail_prompt_body,
        grid=(b, t // tm),
        in_specs=[tile(D_MODEL), tile(NSA_WIDTH), tile(SSM_WIDTH)] + [_const_spec(c.shape) for c in consts],
        out_specs=[tile(D_MODEL), pl.BlockSpec((1, CONV_W - 1, 2 * D_FF), lambda i, j: (i, 0, 0))],
        out_shape=[jax.ShapeDtypeStruct((b, t, D_MODEL), F32), jax.ShapeDtypeStruct((b, CONV_W - 1, 2 * D_FF), F32)],
        scratch_shapes=[pltpu.VMEM((8, 2 * D_FF), F32)],
        compiler_params=_cparams("parallel", "arbitrary"),
        name="tail_prompt",
    )(x, o_nsa, y_ssm, *consts)


def _pair_rows(q8, lane):
    row = lax.broadcasted_iota(jnp.int32, (8, 1), 0)
    out = jnp.zeros((8, LANES), F32)
    for r in range(N_HEADS):
        pair = q8[:, (r // 2) * LANES:(r // 2 + 1) * LANES]
        want_hi = r // GQA
        if r % 2 != want_hi:
            pair = pltpu.roll(pair, HEAD_DIM, 1)
        keep = (lane >= HEAD_DIM) if want_hi else (lane < HEAD_DIM)
        out = out + jnp.where(row == r, jnp.where(keep, pair, 0.0), 0.0)
    return out


def _unpair_rows(o, lane):
    pieces = []
    for j in range(N_HEADS // 2):
        lo, hi = o[2 * j:2 * j + 1], o[2 * j + 1:2 * j + 2]
        if (2 * j) // GQA == 0:
            hi = pltpu.roll(hi, HEAD_DIM, 1)
        else:
            lo = pltpu.roll(lo, HEAD_DIM, 1)
        pieces.append(jnp.where(lane < HEAD_DIM, lo, hi))
    return jnp.concatenate(pieces, axis=1)


def _cmp_sample_body(pt_ref, pool_ref, new_ref, q_ref, bias_ref, w_ref, selmap_ref, oc_ref, idx_ref, buf, xs, sem, *,
                     n_pages, n_seq):
    b = pl.program_id(0)
    n_pos = n_pages * PAGE_SIZE
    n_sub = n_pos // CMP_STRIDE
    half = N_KV_HEADS * HEAD_DIM

    def page_copy(seq, slot, p):
        return pltpu.make_async_copy(pool_ref.at[pt_ref[seq * n_pages + p]], buf.at[slot, p], sem.at[slot])

    def fetch(seq, slot):
        for p in range(n_pages):
            page_copy(seq, slot, p).start()

    @pl.when(b == 0)
    def _():
        fetch(0, 0)

    @pl.when(b + 1 < n_seq)
    def _():
        fetch(b + 1, (b + 1) % 2)

    slot = b % 2
    for p in range(n_pages):
        page_copy(b, slot, p).wait()

    row = lax.broadcasted_iota(jnp.int32, (n_sub, 1), 0)
    kv_c = []
    for x in range(2):
        def to_rows(p, carry):
            xs[pl.ds(pl.multiple_of(p * PAGE_SIZE, PAGE_SIZE), PAGE_SIZE), :] = buf[slot, p, x].reshape(half, PAGE_SIZE).T
            return carry

        lax.fori_loop(0, n_pages, to_rows, 0)
        parts = jnp.zeros((n_sub, 2 * half), F32)
        for s in range(CMP_STRIDE):
            parts = parts + _dot(xs[pl.ds(s, n_sub, stride=CMP_STRIDE), :].astype(BF16), w_ref[x, s])
        new = jnp.broadcast_to(new_ref[0][:, x * half:(x + 1) * half], (8, half)).astype(BF16)
        new_part = _dot(new, w_ref[x, 0])[0:1, half:]
        nxt = jnp.where(row == n_sub - 1, new_part, pltpu.roll(parts[:, half:], n_sub - 1, 0))
        kv_c.append((parts[:, :half] + nxt + bias_ref[0:1, x * half:(x + 1) * half]).astype(BF16))

    q_pos = n_pos
    n_sel = selmap_ref.shape[1]
    n_real = q_pos // SEL_BLOCK + 1
    cmp_end = lax.broadcasted_iota(jnp.int32, (1, n_sub), 1) * CMP_STRIDE + (CMP_LEN - 1)
    blk = lax.broadcasted_iota(jnp.int32, (1, n_sel), 1)
    lane = lax.broadcasted_iota(jnp.int32, (1, LANES), 1)
    row8 = lax.broadcasted_iota(jnp.int32, (8, 1), 0)
    q2 = _pair_rows(jnp.broadcast_to(q_ref[0].astype(F32), (8, NSA_WIDTH)), lane).astype(BF16)
    p_c = _masked_softmax_rows(_nt_dot(q2, kv_c[0]), cmp_end <= q_pos)
    oc_ref[0] = _unpair_rows(_dot(p_c.astype(BF16), kv_c[1]), lane)
    p_sum = jnp.zeros((8, n_sub), F32)
    for h in range(N_KV_HEADS):
        in_h = (row8 >= h * GQA) & (row8 < (h + 1) * GQA)
        p_sum = p_sum + jnp.where(row8 == h, jnp.sum(jnp.where(in_h, p_c, 0.0), axis=0, keepdims=True), 0.0)
    score = _block_scores(_split_dot(p_sum, selmap_ref[...]), blk, q_pos)
    score = jnp.where((blk < n_real) & (row8 < N_KV_HEADS), score, BELOW_NEG)
    blk_f = blk.astype(F32)
    picked = jnp.full((8, LANES), -1.0, F32)
    for it in range(min(N_SELECT, n_real)):
        mx = jnp.max(score, axis=-1, keepdims=True)
        idx = jnp.min(jnp.where(score == mx, blk_f, float(n_sel)), axis=-1, keepdims=True)
        picked = jnp.where(lane == it, jnp.where(mx > 0.5 * NEG, idx, -1.0), picked)
        score = jnp.where(blk_f == idx, BELOW_NEG, score)
    idx_ref[0] = picked.astype(jnp.int32)


def _cmp_sample(page_table, pool_t, kvc_new, qb, bias, w_pos, selmap):
    n_seq, n_pages = page_table.shape
    n_pos = n_pages * PAGE_SIZE
    grid_spec = pltpu.PrefetchScalarGridSpec(
        num_scalar_prefetch=1,
        grid=(n_seq,),
        in_specs=[pl.BlockSpec(memory_space=pl.ANY),
                  pl.BlockSpec((1, 1, KV_WIDTH), lambda i, pt: (i, 0, 0)),
                  pl.BlockSpec((1, 1, NSA_WIDTH), lambda i, pt: (i, 0, 0)),
                  _const_spec(bias.shape), _const_spec(w_pos.shape), _const_spec(selmap.shape)],
        out_specs=[pl.BlockSpec((1, 1, NSA_WIDTH), lambda i, pt: (i, 0, 0)),
                   pl.BlockSpec((1, 8, LANES), lambda i, pt: (i, 0, 0))],
        scratch_shapes=[pltpu.VMEM((2, n_pages, 2, N_KV_HEADS, HEAD_DIM, PAGE_SIZE), F32),
                        pltpu.VMEM((n_pos, N_KV_HEADS * HEAD_DIM), F32), pltpu.SemaphoreType.DMA((2,))],
    )
    return pl.pallas_call(
        functools.partial(_cmp_sample_body, n_pages=n_pages, n_seq=n_seq),
        grid_spec=grid_spec,
        out_shape=[jax.ShapeDtypeStruct((n_seq, 1, NSA_WIDTH), F32), jax.ShapeDtypeStruct((n_seq, 8, LANES), jnp.int32)],
        compiler_params=_cparams("arbitrary"),
        name="cmp_sample",
    )(page_table.reshape(-1), pool_t, kvc_new, qb, bias, w_pos, selmap)


def _attn_sample_body(idx_ref, pt_ref, pool_ref, q_ref, gl_ref, oc_ref, ksn_ref, kwn_ref, win_ref, o_ref, wout_ref,
                      buf, sem, *, n_pages, n_seq, k_sel):
    b = pl.program_id(0)
    blk_per_page = PAGE_SIZE // SEL_BLOCK
    n_past = n_pages * blk_per_page
    n_blk = N_KV_HEADS * k_sel

    def blk_copy(seq, slot, j):
        i = jnp.clip(idx_ref[seq * n_blk + j], 0, n_past - 1)
        page = pt_ref[seq * n_pages + i // blk_per_page]
        return pltpu.make_async_copy(pool_ref.at[page, :, j // k_sel], buf.at[slot, j], sem.at[slot])

    def fetch(seq, slot):
        for j in range(n_blk):
            blk_copy(seq, slot, j).start()

    @pl.when(b == 0)
    def _():
        fetch(0, 0)

    @pl.when(b + 1 < n_seq)
    def _():
        fetch(b + 1, (b + 1) % 2)

    slot = b % 2
    for j in range(n_blk):
        blk_copy(b, slot, j).wait()

    q = q_ref[0].astype(F32)
    gate = _sigmoid(gl_ref[0])
    o_c = oc_ref[0]
    ks_new = ksn_ref[0].astype(BF16).astype(F32)
    kw_new = kwn_ref[0].astype(BF16).astype(F32)
    n_win = win_ref.shape[-1]
    lane = lax.broadcasted_iota(jnp.int32, (1, PAGE_SIZE), 1)
    lane_blk = lax.shift_right_logical(lane, SEL_BLOCK.bit_length() - 1)
    head_out = []
    for h in range(N_KV_HEADS):
        k_lo, v_lo = h * HEAD_DIM, (N_KV_HEADS + h) * HEAD_DIM
        qs = _stack_heads_single(q, h)
        qf = qs.astype(F32)
        s_blocks, m_blocks = [], []
        has_new = jnp.zeros((1, 1), F32)
        for j in range(k_sel):
            i = idx_ref[b * n_blk + h * k_sel + j]
            ok = jnp.where((i >= 0) & (i < n_past), 1.0, 0.0)
            m_blocks.append(jnp.where(lane_blk == i % blk_per_page, ok, 0.0))
            has_new = jnp.maximum(has_new, jnp.where(i >= n_past, 1.0, 0.0))
            s_blocks.append(_dot(qs, buf[slot, h * k_sel + j, 0].astype(BF16)))
        past = jnp.concatenate(m_blocks, axis=1) > 0.5
        s = jnp.where(past, jnp.concatenate(s_blocks, axis=1), NEG)
        s_new = jnp.where(has_new > 0.5, jnp.sum(qf * ks_new[:, k_lo:k_lo + HEAD_DIM], axis=-1, keepdims=True), NEG)
        m = jnp.maximum(jnp.max(s, axis=-1, keepdims=True), s_new)
        p = jnp.where(past, jnp.exp(s - m), 0.0)
        p_new = jnp.where(has_new > 0.5, jnp.exp(s_new - m), 0.0)
        p_bf = p.astype(BF16)
        o_s = p_new.astype(BF16).astype(F32) * ks_new[:, v_lo:v_lo + HEAD_DIM]
        for j in range(k_sel):
            o_s = o_s + _nt_dot(p_bf[:, j * PAGE_SIZE:(j + 1) * PAGE_SIZE], buf[slot, h * k_sel + j, 1].astype(BF16))
        o_s = o_s * (1.0 / (jnp.sum(p, axis=-1, keepdims=True) + p_new))
        s = _dot(qs, win_ref[0, 0, h].astype(BF16))
        s_new = jnp.sum(qf * kw_new[:, k_lo:k_lo + HEAD_DIM], axis=-1, keepdims=True)
        m = jnp.maximum(jnp.max(s, axis=-1, keepdims=True), s_new)
        p, p_new = jnp.exp(s - m), jnp.exp(s_new - m)
        inv = 1.0 / (jnp.sum(p, axis=-1, keepdims=True) + p_new)
        o_w = (_nt_dot(p.astype(BF16), win_ref[0, 1, h].astype(BF16))
               + p_new.astype(BF16).astype(F32) * kw_new[:, v_lo:v_lo + HEAD_DIM]) * inv
        for g in range(GQA):
            hd = GQA * h + g
            col = hd * 3
            head_out.append(gate[:, col:col + 1] * o_c[:, hd * HEAD_DIM:(hd + 1) * HEAD_DIM]
                            + gate[:, col + 1:col + 2] * o_s[g:g + 1] + gate[:, col + 2:col + 3] * o_w[g:g + 1])
    o_ref[0] = jnp.concatenate(head_out, axis=-1).astype(BF16)
    d_row = lax.broadcasted_iota(jnp.int32, (HEAD_DIM, HEAD_DIM), 0)
    d_col = lax.broadcasted_iota(jnp.int32, (HEAD_DIM, HEAD_DIM), 1)
    pos = lax.broadcasted_iota(jnp.int32, (1, n_win), 1)
    new_rows = jnp.broadcast_to(kwn_ref[0], (HEAD_DIM, KV_WIDTH))
    for x in range(2):
        for h in range(N_KV_HEADS):
            lo = (x * N_KV_HEADS + h) * HEAD_DIM
            new_col = jnp.sum(jnp.where(d_row == d_col, new_rows[:, lo:lo + HEAD_DIM], 0.0), axis=-1, keepdims=True)
            wout_ref[0, x, h] = jnp.where(pos == n_win - 1, new_col, pltpu.roll(win_ref[0, x, h], n_win - 1, 1))


def _attn_sample(idx, page_table, pool_t, qb, gl, o_c, kvs_new, kvw_new, win_t, k_sel):
    n_seq, n_pages = page_table.shape
    n_win = win_t.shape[-1]
    n_blk = N_KV_HEADS * k_sel
    one = lambda w: pl.BlockSpec((1, 1, w), lambda i, a, p: (i, 0, 0))
    win_spec = pl.BlockSpec((1, 2, N_KV_HEADS, HEAD_DIM, n_win), lambda i, a, p: (i, 0, 0, 0, 0))
    grid_spec = pltpu.PrefetchScalarGridSpec(
        num_scalar_prefetch=2,
        grid=(n_seq,),
        in_specs=[pl.BlockSpec(memory_space=pl.ANY), one(NSA_WIDTH), one(GATE_PAD), one(NSA_WIDTH), one(KV_WIDTH),
                  one(KV_WIDTH), win_spec],
        out_specs=[one(NSA_WIDTH), win_spec],
        scratch_shapes=[pltpu.VMEM((2, n_blk, 2, HEAD_DIM, PAGE_SIZE), F32), pltpu.SemaphoreType.DMA((2,))],
    )
    return pl.pallas_call(
        functools.partial(_attn_sample_body, n_pages=n_pages, n_seq=n_seq, k_sel=k_sel),
        grid_spec=grid_spec,
        out_shape=[jax.ShapeDtypeStruct((n_seq, 1, NSA_WIDTH), BF16), jax.ShapeDtypeStruct(win_t.shape, F32)],
        compiler_params=_cparams("arbitrary"),
        name="attn_sample",
    )(idx, page_table.reshape(-1), pool_t, qb, gl, o_c, kvs_new, kvw_new, win_t)


def _tail_sample_body(x_ref, o_ref, u_ref, h0r_ref, h0i_ref, hist2_ref, hist1_ref, bb_ref, cc_ref, d_ref, abr_ref,
                      abi_ref, wglu_ref, bglu_ref, wout_ref, nf_ref, wup_ref, cw_ref, cb_ref, wdown_ref, nfin_ref,
                      out_ref, hr_ref, hi_ref, cs_ref):
    u = u_ref[...]
    bu = _split_dot(u, bb_ref[...])
    ar, ai = abr_ref[...], abi_ref[...]
    h0r, h0i = h0r_ref[...], h0i_ref[...]
    hr = ar * h0r - ai * h0i + bu[:, :SSM_N]
    hi = ar * h0i + ai * h0r + bu[:, SSM_N:]
    hr_ref[...] = hr
    hi_ref[...] = hi
    y = _dot(hr.astype(BF16), cc_ref[:SSM_N, :]) + _dot(hi.astype(BF16), cc_ref[SSM_N:, :]) + d_ref[...] * u
    x1 = _mix_out(x_ref[...], o_ref[...], y, wglu_ref, bglu_ref, wout_ref)
    hn = _rms(x1, nf_ref[...]).astype(BF16)

    def prev_rows(lo, hi_col, hu):
        cs_ref[:, lo:hi_col] = hist1_ref[:, lo:hi_col]
        cs_ref[:, 2 * D_FF + lo:2 * D_FF + hi_col] = hu
        return hist2_ref[:, lo:hi_col], hist1_ref[:, lo:hi_col]

    x2 = x1 + _ffn_chunks(hn, wup_ref, cw_ref, cb_ref, wdown_ref, prev_rows)
    out_ref[...] = _rms(x2, nfin_ref[...])


def _tail_sample(x, o_nsa, u, h0r, h0i, hist2, hist1, bb, cc, d, ab_re, ab_im, w):
    n = x.shape[0]
    sds = lambda width: jax.ShapeDtypeStruct((n, width), F32)
    return pl.pallas_call(
        _tail_sample_body,
        out_shape=[sds(D_MODEL), sds(SSM_N), sds(SSM_N), sds((CONV_W - 1) * 2 * D_FF)],
        compiler_params=pltpu.CompilerParams(vmem_limit_bytes=VMEM_LIMIT),
        name="tail_sample",
    )(x, o_nsa, u, h0r, h0i, hist2, hist1, bb, cc, d, ab_re, ab_im, w["w_glu"], w["b_glu"], w["w_out"],
      w["norm_ffn"], w["w_up"], w["conv_w"], w["conv_b"], w["w_down"], w["norm_final"])


def _pad_w_in(w_in):
    c = NSA_WIDTH + 3 * KV_WIDTH
    return jnp.concatenate([w_in[:, :c], w_in[:, c + N_GATES:], w_in[:, c:c + N_GATES],
                            jnp.zeros((D_MODEL, GATE_PAD - N_GATES), w_in.dtype)], axis=1).astype(BF16)


def _cmp_weight(w_cmp):
    w = w_cmp.reshape(2, CMP_LEN // CMP_STRIDE, CMP_STRIDE, HEAD_DIM, HEAD_DIM)
    eye_x = jnp.eye(2, dtype=w.dtype)
    eye_h = jnp.eye(N_KV_HEADS, dtype=w.dtype)
    big = (w.transpose(2, 0, 3, 1, 4)[:, :, None, :, :, None, None, :]
           * eye_x[None, :, None, None, None, :, None, None] * eye_h[None, None, :, None, None, None, :, None])
    return big.reshape(SUB_W, 2 * KV_WIDTH).astype(BF16)


def _cmp_weight_pos(w_cmp):
    w = w_cmp.reshape(2, CMP_LEN // CMP_STRIDE, CMP_STRIDE, HEAD_DIM, HEAD_DIM)
    eye_h = jnp.eye(N_KV_HEADS, dtype=w.dtype)
    big = (w.transpose(0, 2, 3, 1, 4)[:, :, None, :, :, None, :]
           * eye_h[None, None, :, None, None, :, None])
    half = N_KV_HEADS * HEAD_DIM
    return big.reshape(2, CMP_STRIDE, half, 2 * half).astype(BF16)


def _pe_sub(pe_cmp):
    rows = jnp.broadcast_to(pe_cmp.transpose(1, 0, 2)[:, :, None, :], (CMP_LEN, 2, N_KV_HEADS, HEAD_DIM))
    sub = rows.reshape(CMP_LEN // CMP_STRIDE, SUB_W)
    return jnp.concatenate([sub, jnp.zeros((8 - sub.shape[0], SUB_W), sub.dtype)], axis=0)


def _sel_map(n_cmp, n_sel, n_sel_pad):
    c0 = (jnp.arange(n_cmp) * CMP_STRIDE)[:, None]
    s0 = (jnp.arange(n_sel_pad) * SEL_BLOCK)[None, :]
    hit = (c0 < s0 + SEL_BLOCK) & (c0 + CMP_LEN > s0) & (jnp.arange(n_sel_pad)[None, :] < n_sel)
    return hit.astype(BF16)


def _expand_map(n_sel, t, kc_len):
    hit = jnp.arange(t)[None, :] // SEL_BLOCK == jnp.arange(n_sel)[:, None]
    return hit.astype(BF16).reshape(n_sel, t // kc_len, kc_len).transpose(1, 0, 2)


def kernel(x_prompt, x_sample, cache_kv_cmp, cache_kv_sel, cache_kv_win, state_ssm_re, state_ssm_im, state_ffn_conv, page_table, norm_mix, w_in, pe_cmp, w_cmp, ssm_a_re, ssm_a_im, ssm_log_dt, ssm_b_re, ssm_b_im, ssm_c_re, ssm_c_im, ssm_d, w_glu, b_glu, w_out, norm_ffn, w_up, conv_w, conv_b, w_down, norm_final):
    depth = w_in.shape[0]
    assert depth == 1, "single-layer trunk"
    b, t, _ = x_prompt.shape
    bd, s, _ = x_sample.shape
    assert s == 1, "one new position per sample sequence"
    n_pages = page_table.shape[1]
    l = 0
    w_pad = _pad_w_in(w_in[l])
    g_mix = norm_mix[l].reshape(1, D_MODEL)
    w_big = _cmp_weight(w_cmp[l])
    pe_sub = _pe_sub(pe_cmp[l])
    ab_re, ab_im, bb_re, bb_im = _ssm_params(ssm_log_dt[l], ssm_a_re[l], ssm_a_im[l], ssm_b_re[l], ssm_b_im[l])
    bb = jnp.concatenate([_block_diag_in(bb_re), _block_diag_in(bb_im)], axis=1).astype(BF16)
    cc = jnp.concatenate([_block_diag_out(ssm_c_re[l]), -_block_diag_out(ssm_c_im[l])], axis=0).astype(BF16)
    d_row = ssm_d[l].reshape(1, SSM_WIDTH)
    tail_w = {"w_glu": w_glu[l].astype(BF16), "b_glu": b_glu[l].reshape(1, -1), "w_out": w_out[l].astype(BF16),
              "norm_ffn": norm_ffn[l].reshape(1, -1), "w_up": w_up[l].astype(BF16), "conv_w": conv_w[l],
              "conv_b": conv_b[l].reshape(1, -1), "w_down": w_down[l].astype(BF16),
              "norm_final": norm_final.reshape(1, -1)}

    kvc, _, _, kvct, kvst, kvwt, kstb, kwtb, gl, u, qb = _inproj(x_prompt, g_mix, w_pad, 512)
    n_sub = t // CMP_STRIDE
    kc, cmp_bias = _cmp_prompt(kvc.reshape(b, n_sub, SUB_W), pe_sub, w_big)
    n_sel = t // SEL_BLOCK
    o_nsa = _attn_prompt(qb, gl, kc, kstb, kwtb, _sel_map(n_sub, n_sel, n_sel).T)
    y_ssm, p_hr, p_hi = _ssm_prompt(u, bb, cc, d_row, ab_re, ab_im)
    y_prompt, p_conv = _tail_prompt(x_prompt, o_nsa, y_ssm, tail_w)
    win_keep = min(WINDOW, t)
    kv_rows = lambda a: a.reshape(a.shape[0], 2, N_KV_HEADS, HEAD_DIM, a.shape[2]).transpose(0, 4, 1, 2, 3)[None]
    st_shape = (depth, b, SSM_GROUPS, SSM_STATE)

    kvc_n, kvs_n, kvw_n, kvct_n, kvst_n, _, _, _, gl_n, u_n, qb_n = _inproj(x_sample.reshape(1, bd, D_MODEL), g_mix, w_pad, bd)
    per_seq = lambda a: a.reshape(bd, 1, a.shape[-1])
    n_sub_s = n_pages * (PAGE_SIZE // CMP_STRIDE)
    n_sel_s = n_pages * (PAGE_SIZE // SEL_BLOCK) + 1
    n_sel_pad = -(-n_sel_s // LANES) * LANES
    k_sel = min(N_SELECT, n_sel_s)
    pos_minor = lambda a: a.transpose(0, 2, 3, 4, 1)
    o_c, picked = _cmp_sample(page_table, pos_minor(cache_kv_cmp[l]), per_seq(kvc_n), per_seq(qb_n), cmp_bias,
                              _cmp_weight_pos(w_cmp[l]), _sel_map(n_sub_s, n_sel_s, n_sel_pad))
    idx = picked[:, :N_KV_HEADS, :k_sel].reshape(-1)
    n_buf = cache_kv_win.shape[2]
    assert n_buf == WINDOW, "window buffer holds exactly WINDOW rows"
    o_nsa_s, s_win = _attn_sample(idx, page_table, pos_minor(cache_kv_sel[l]), per_seq(qb_n), per_seq(gl_n), o_c,
                                  per_seq(kvs_n), per_seq(kvw_n), pos_minor(cache_kv_win[l]), k_sel)
    hist = state_ffn_conv[l]
    y_sample, s_hr, s_hi, s_conv = _tail_sample(
        x_sample.reshape(bd, D_MODEL), o_nsa_s.reshape(bd, -1), u_n.reshape(bd, -1),
        state_ssm_re[l].reshape(bd, SSM_N), state_ssm_im[l].reshape(bd, SSM_N), hist[:, 0], hist[:, 1], bb, cc, d_row,
        ab_re, ab_im, tail_w)
    kv_rows_s = lambda a: kv_rows(a).reshape(depth, bd, s, 2, N_KV_HEADS, HEAD_DIM)
    st_shape_s = (depth, bd, SSM_GROUPS, SSM_STATE)
    return (y_prompt, y_sample.reshape(bd, s, D_MODEL),
            kv_rows(kvct), kv_rows(kvst), kv_rows(kvwt[:, :, t - win_keep:]),
            p_hr.reshape(st_shape), p_hi.reshape(st_shape), p_conv.reshape(depth, b, CONV_W - 1, 2 * D_FF),
            kv_rows_s(kvct_n), kv_rows_s(kvst_n),
            s_win.transpose(0, 4, 1, 2, 3)[None],
            s_hr.reshape(st_shape_s), s_hi.reshape(st_shape_s), s_conv.reshape(depth, bd, CONV_W - 1, 2 * D_FF))
```

```python
import functools
import math

import jax
import jax.numpy as jnp
from jax import lax
from jax.experimental import pallas as pl
from jax.experimental.pallas import tpu as pltpu

D_MODEL = 1024
N_HEADS = 8
N_KV_HEADS = 2
GQA = N_HEADS // N_KV_HEADS
HEAD_DIM = 64
NSA_WIDTH = N_HEADS * HEAD_DIM
KV_WIDTH = 2 * N_KV_HEADS * HEAD_DIM
N_GATES = 3 * N_HEADS
CMP_LEN = 32
CMP_STRIDE = 16
SEL_BLOCK = 64
N_SELECT = 16
WINDOW = 512
Q_BLOCK = 128
PAGE_SIZE = 128
SSM_WIDTH = D_MODEL - NSA_WIDTH
SSM_GROUP = 16
SSM_GROUPS = SSM_WIDTH // SSM_GROUP
SSM_STATE = 64
SSM_N = SSM_GROUPS * SSM_STATE
D_FF = (D_MODEL * 11 // 4 + 127) // 128 * 128
CONV_W = 3
EPS = 1e-6
NEG = -1e30
BIG = 1e30
BELOW_NEG = -3e38

LANES = 128
GATE_PAD = LANES
IN_PAD = NSA_WIDTH + 3 * KV_WIDTH + SSM_WIDTH + GATE_PAD
SUB_W = CMP_STRIDE * KV_WIDTH
VMEM_LIMIT = 56 * 1024 * 1024

F32 = jnp.float32
BF16 = jnp.bfloat16


def _nt_dot(a, b):
    return lax.dot_general(a, b, (((1,), (1,)), ((), ())), preferred_element_type=F32)


def _dot(a, b):
    return jnp.dot(a, b, preferred_element_type=F32)


def _sigmoid(x):
    return 1.0 / (1.0 + jnp.exp(-x))


def _gelu_tanh(x):
    return 0.5 * x * (1.0 + jnp.tanh(math.sqrt(2.0 / math.pi) * (x + 0.044715 * (x * x * x))))


def _rms(x, g):
    return x * lax.rsqrt(jnp.mean(x * x, axis=-1, keepdims=True) + EPS) * g


def _split_dot(x, w_bf):
    hi = x.astype(BF16)
    lo = (x - hi.astype(F32)).astype(BF16)
    return _dot(hi, w_bf) + _dot(lo, w_bf)


def _cparams(*sem):
    return pltpu.CompilerParams(dimension_semantics=sem, vmem_limit_bytes=VMEM_LIMIT)


def _const_spec(shape):
    nd = len(shape)
    return pl.BlockSpec(shape, lambda *_: (0,) * nd, pipeline_mode=pl.Buffered(1))


def _inproj_body(x_ref, g_ref, w_ref, kvc_ref, kvs_ref, kvw_ref, kvct_ref, kvst_ref, kvwt_ref, kstb_ref, kwtb_ref,
                 gl_ref, u_ref, qb_ref):
    h = _rms(x_ref[0], g_ref[...])
    z = _dot(h.astype(BF16), w_ref[...])
    tm = z.shape[0]
    c = NSA_WIDTH
    qb_ref[0] = (z[:, :c] * (HEAD_DIM ** -0.5)).astype(BF16)
    for rm_ref, t_ref, tb_ref in ((kvc_ref, kvct_ref, None), (kvs_ref, kvst_ref, kstb_ref),
                                  (kvw_ref, kvwt_ref, kwtb_ref)):
        rows = z[:, c:c + KV_WIDTH]
        c += KV_WIDTH
        rm_ref[0] = rows
        cols = rows.T
        t_ref[0] = cols
        if tb_ref is not None:
            cols_bf = cols.astype(BF16)
            for k in range(tm // LANES):
                tb_ref[0, k] = cols_bf[:, k * LANES:(k + 1) * LANES]
    u_ref[0] = z[:, c:c + SSM_WIDTH]
    gl_ref[0] = z[:, c + SSM_WIDTH:]


def _inproj(x, g, w_pad, tm):
    b, t, _ = x.shape
    row = lambda w: pl.BlockSpec((1, tm, w), lambda i, j: (i, j, 0))
    col = pl.BlockSpec((1, KV_WIDTH, tm), lambda i, j: (i, 0, j))
    chunk = pl.BlockSpec((1, tm // LANES, KV_WIDTH, LANES), lambda i, j: (i, j, 0, 0))
    rm_t = jax.ShapeDtypeStruct((b, t, KV_WIDTH), F32)
    col_t = jax.ShapeDtypeStruct((b, KV_WIDTH, t), F32)
    chunk_t = jax.ShapeDtypeStruct((b, t // LANES, KV_WIDTH, LANES), BF16)
    return pl.pallas_call(
        _inproj_body,
        grid=(b, t // tm),
        in_specs=[row(D_MODEL), _const_spec((1, D_MODEL)), _const_spec((D_MODEL, IN_PAD))],
        out_specs=[row(KV_WIDTH)] * 3 + [col] * 3 + [chunk] * 2 + [row(GATE_PAD), row(SSM_WIDTH), row(NSA_WIDTH)],
        out_shape=[rm_t] * 3 + [col_t] * 3 + [chunk_t] * 2
        + [jax.ShapeDtypeStruct((b, t, GATE_PAD), F32), jax.ShapeDtypeStruct((b, t, SSM_WIDTH), F32),
           jax.ShapeDtypeStruct((b, t, NSA_WIDTH), BF16)],
        compiler_params=_cparams("parallel", "parallel"),
        name="in_proj",
    )(x, g, w_pad)


def _cmp_prompt_body(sub_ref, pe_ref, w_ref, kc_ref, bias_ref):
    w = w_ref[...]
    parts = _dot(sub_ref[0].astype(BF16), w)
    pe = _dot(pe_ref[...].astype(BF16), w)
    bias = pe[0:1, :KV_WIDTH] + pe[1:2, KV_WIDTH:]
    n_sub = parts.shape[0]
    nxt = pltpu.roll(parts[:, KV_WIDTH:], n_sub - 1, 0)
    kc_ref[0] = (parts[:, :KV_WIDTH] + nxt + bias).astype(BF16)
    bias_ref[...] = jnp.broadcast_to(bias, bias_ref.shape)


def _cmp_prompt(sub, pe_sub, w_big):
    b, n_sub, _ = sub.shape
    return pl.pallas_call(
        _cmp_prompt_body,
        grid=(b,),
        in_specs=[pl.BlockSpec((1, n_sub, SUB_W), lambda i: (i, 0, 0)), _const_spec(pe_sub.shape),
                  _const_spec(w_big.shape)],
        out_specs=[pl.BlockSpec((1, n_sub, KV_WIDTH), lambda i: (i, 0, 0)),
                   pl.BlockSpec((8, KV_WIDTH), lambda i: (0, 0))],
        out_shape=[jax.ShapeDtypeStruct((b, n_sub, KV_WIDTH), BF16), jax.ShapeDtypeStruct((8, KV_WIDTH), F32)],
        compiler_params=_cparams("arbitrary"),
        name="cmp_prompt",
    )(sub, pe_sub, w_big)


def _stack_heads(q, h):
    return jnp.concatenate([q[:, (GQA * h + g) * HEAD_DIM:(GQA * h + g + 1) * HEAD_DIM] for g in range(GQA)], axis=0)


def _stack_heads_single(q, h):
    row = lax.broadcasted_iota(jnp.int32, (8, 1), 0)
    q8 = jnp.broadcast_to(q, (8, NSA_WIDTH))
    out = jnp.zeros((8, HEAD_DIM), F32)
    for g in range(GQA):
        lo = (GQA * h + g) * HEAD_DIM
        out = out + jnp.where(row == g, q8[:, lo:lo + HEAD_DIM], 0.0)
    return out.astype(BF16)


def _masked_softmax_rows(s, mask):
    sm = jnp.where(mask, s, NEG)
    m = jnp.max(sm, axis=-1, keepdims=True)
    p = jnp.where(mask, jnp.exp(sm - m), 0.0)
    l = jnp.sum(p, axis=-1, keepdims=True)
    return p * (1.0 / jnp.where(l > 0.0, l, 1.0))


def _block_scores(imp, blk, qpos):
    first = blk * SEL_BLOCK
    own_or_imp = jnp.where(first + SEL_BLOCK > qpos, BIG, imp)
    return jnp.where(blk == 0, BIG, jnp.where(first <= qpos, own_or_imp, NEG))


def _masked_softmax_cols(s, mask):
    sm = jnp.where(mask, s, NEG)
    m = jnp.max(sm, axis=0, keepdims=True)
    p = jnp.where(mask, jnp.exp(sm - m), 0.0)
    l = jnp.sum(p, axis=0, keepdims=True)
    return p * (1.0 / jnp.where(l > 0.0, l, 1.0))


def _top_k_mask_cols(score, blk, k):
    n = float(score.shape[0])
    sel = jnp.zeros(score.shape, F32)
    for _ in range(k):
        mx = jnp.max(score, axis=0, keepdims=True)
        idx = jnp.min(jnp.where(score == mx, blk, n), axis=0, keepdims=True)
        hit = blk == idx
        sel = sel + jnp.where(hit, jnp.where(mx > 0.5 * NEG, 1.0, 0.0), 0.0)
        score = jnp.where(hit, BELOW_NEG, score)
    return sel


KEY_BLOCKS = 4
ROW_BLOCK = 32


def _attn_prompt_body(q_ref, gl_ref, kc_ref, kst_ref, kwt_ref, selmap_ref, exp_ref, o_ref, m_s, l_s, a_s, acc_s, s_s,
                      b_s, p_s):
    qb_idx = pl.program_id(1)
    start = qb_idx * Q_BLOCK
    kc_len = KEY_BLOCKS * LANES
    q = q_ref[0]
    gate = _sigmoid(gl_ref[0])
    qpos = start + lax.broadcasted_iota(jnp.int32, (Q_BLOCK, 1), 0)
    qpos4 = jnp.concatenate([qpos] * GQA, axis=0)
    qpos_l = start + lax.broadcasted_iota(jnp.int32, (1, Q_BLOCK), 1)
    qpos4_l = jnp.concatenate([qpos_l] * GQA, axis=1)
    n_cmp = kc_ref.shape[1]
    n_sel = selmap_ref.shape[0]
    cmp_end = lax.broadcasted_iota(jnp.int32, (n_cmp, 1), 0) * CMP_STRIDE + (CMP_LEN - 1)
    blk = lax.broadcasted_iota(jnp.int32, (n_sel, 1), 0)
    qs, o_c, scores = [], [], []
    for h in range(N_KV_HEADS):
        k_lo, v_lo = h * HEAD_DIM, (N_KV_HEADS + h) * HEAD_DIM
        qs.append(_stack_heads(q, h))
        p_t = _masked_softmax_cols(_nt_dot(kc_ref[0, :, k_lo:k_lo + HEAD_DIM], qs[h]), cmp_end <= qpos4_l)
        o_c.append(_dot(p_t.T.astype(BF16), kc_ref[0, :, v_lo:v_lo + HEAD_DIM]))
        p_sum = p_t[:, 0:Q_BLOCK]
        for g in range(1, GQA):
            p_sum = p_sum + p_t[:, g * Q_BLOCK:(g + 1) * Q_BLOCK]
        hi = p_sum.astype(BF16)
        lo = (p_sum - hi.astype(F32)).astype(BF16)
        imp = _dot(selmap_ref[...], hi) + _dot(selmap_ref[...], lo)
        scores.append(_block_scores(imp, blk, qpos_l))
    sel_t = _top_k_mask_cols(jnp.concatenate(scores, axis=1), blk.astype(F32), min(N_SELECT, n_sel))
    sel = [sel_t[:, h * Q_BLOCK:(h + 1) * Q_BLOCK].T.astype(BF16) for h in range(N_KV_HEADS)]
    m_s[...] = jnp.full(m_s.shape, NEG, F32)
    l_s[...] = jnp.zeros(l_s.shape, F32)
    acc_s[...] = jnp.zeros(acc_s.shape, F32)
    n_chunks = (start + Q_BLOCK + kc_len - 1) // kc_len

    def sel_chunk(it, carry):
        c = n_chunks - 1 - it
        kpos = c * kc_len + lax.broadcasted_iota(jnp.int32, (1, kc_len), 1)
        for h in range(N_KV_HEADS):
            k_lo, v_lo = h * HEAD_DIM, (N_KV_HEADS + h) * HEAD_DIM
            k_t = jnp.concatenate([kst_ref[0, c * KEY_BLOCKS + j, k_lo:k_lo + HEAD_DIM, :]
                                   for j in range(KEY_BLOCKS)], axis=1)
            v_t = jnp.concatenate([kst_ref[0, c * KEY_BLOCKS + j, v_lo:v_lo + HEAD_DIM, :]
                                   for j in range(KEY_BLOCKS)], axis=1)
            s_s[h] = _dot(qs[h], k_t)
            chosen = _dot(sel[h], exp_ref[c])
            b_s[h] = jnp.where(kpos <= qpos, jnp.where(chosen > 0.5, 0.0, NEG), NEG)
            for r0 in range(0, GQA * Q_BLOCK, ROW_BLOCK):
                rows = slice(r0, r0 + ROW_BLOCK)
                q0 = r0 % Q_BLOCK
                sm = s_s[h, rows, :] + b_s[h, q0:q0 + ROW_BLOCK, :]
                s_s[h, rows, :] = sm
                m_prev = m_s[h, rows, :]
                m_next = jnp.maximum(m_prev, jnp.max(sm, axis=-1, keepdims=True))
                a_s[h, rows, :] = jnp.exp(m_prev - m_next)
                m_s[h, rows, :] = m_next
            for r0 in range(0, GQA * Q_BLOCK, ROW_BLOCK):
                rows = slice(r0, r0 + ROW_BLOCK)
                p = jnp.exp(s_s[h, rows, :] - jnp.tile(m_s[h, rows, :], (1, KEY_BLOCKS)))
                l_s[h, rows, :] = a_s[h, rows, :] * l_s[h, rows, :] + jnp.sum(p, axis=-1, keepdims=True)
                p_s[h, rows, :] = p.astype(BF16)
            acc_s[h] = acc_s[h] * a_s[h, :, :HEAD_DIM] + _nt_dot(p_s[h], v_t)
        return carry

    lax.fori_loop(0, n_chunks, sel_chunk, 0)
    head_out = []
    for h in range(N_KV_HEADS):
        k_lo, v_lo = h * HEAD_DIM, (N_KV_HEADS + h) * HEAD_DIM
        o_s = acc_s[h] * (1.0 / l_s[h])[:, :HEAD_DIM]
        kw, vw = [], []
        for c in range((WINDOW + Q_BLOCK) // LANES):
            src = jnp.maximum(qb_idx - WINDOW // LANES + c, 0)
            kw.append(kwt_ref[0, src, k_lo:k_lo + HEAD_DIM, :])
            vw.append(kwt_ref[0, src, v_lo:v_lo + HEAD_DIM, :])
        s = _dot(qs[h], jnp.concatenate(kw, axis=1))
        wpos = start - WINDOW + lax.broadcasted_iota(jnp.int32, (1, WINDOW + Q_BLOCK), 1)
        sm = jnp.where(wpos <= qpos4, jnp.where(wpos >= jnp.maximum(qpos4 - WINDOW, 0), s, NEG), NEG)
        p = jnp.exp(sm - jnp.max(sm, axis=-1, keepdims=True))
        o_w = _nt_dot(p.astype(BF16), jnp.concatenate(vw, axis=1)) * (1.0 / jnp.sum(p, axis=-1, keepdims=True))
        for g in range(GQA):
            col = (GQA * h + g) * 3
            r = slice(g * Q_BLOCK, (g + 1) * Q_BLOCK)
            head_out.append(gate[:, col:col + 1] * o_c[h][r] + gate[:, col + 1:col + 2] * o_s[r]
                            + gate[:, col + 2:col + 3] * o_w[r])
    o_ref[0] = jnp.concatenate(head_out, axis=-1).astype(BF16)


def _attn_prompt(qb, gl, kc, kst, kwt, selmap_t):
    b, t, _ = qb.shape
    kc_len = KEY_BLOCKS * LANES
    expand = _expand_map(selmap_t.shape[0], t, kc_len)
    rows = GQA * Q_BLOCK
    whole = lambda a: pl.BlockSpec((1,) + a.shape[1:], lambda i, j: (i,) + (0,) * (a.ndim - 1))
    return pl.pallas_call(
        _attn_prompt_body,
        grid=(b, t // Q_BLOCK),
        in_specs=[pl.BlockSpec((1, Q_BLOCK, NSA_WIDTH), lambda i, j: (i, j, 0)),
                  pl.BlockSpec((1, Q_BLOCK, GATE_PAD), lambda i, j: (i, j, 0)),
                  whole(kc), whole(kst), whole(kwt), _const_spec(selmap_t.shape), _const_spec(expand.shape)],
        out_specs=pl.BlockSpec((1, Q_BLOCK, NSA_WIDTH), lambda i, j: (i, j, 0)),
        out_shape=jax.ShapeDtypeStruct((b, t, NSA_WIDTH), BF16),
        scratch_shapes=[pltpu.VMEM((N_KV_HEADS, rows, LANES), F32), pltpu.VMEM((N_KV_HEADS, rows, LANES), F32),
                        pltpu.VMEM((N_KV_HEADS, rows, LANES), F32), pltpu.VMEM((N_KV_HEADS, rows, HEAD_DIM), F32),
                        pltpu.VMEM((N_KV_HEADS, rows, kc_len), F32), pltpu.VMEM((N_KV_HEADS, Q_BLOCK, kc_len), F32),
                        pltpu.VMEM((N_KV_HEADS, rows, kc_len), BF16)],
        compiler_params=_cparams("parallel", "arbitrary"),
        name="attn_prompt",
    )(qb, gl, kc, kst, kwt, selmap_t, expand)


def _ssm_param_body(ldt_ref, are_ref, aim_ref, bre_ref, bim_ref, abr_ref, abi_ref, bbr_ref, bbi_ref):
    dt = jnp.exp(ldt_ref[...])
    are, aim = are_ref[...], aim_ref[...]
    mag = jnp.exp(dt * are)
    ab_re, ab_im = mag * jnp.cos(dt * aim), mag * jnp.sin(dt * aim)
    den = are * are + aim * aim
    zr, zi = ab_re - 1.0, ab_im
    f_re = (zr * are + zi * aim) / den
    f_im = (zi * are - zr * aim) / den
    abr_ref[...] = ab_re
    abi_ref[...] = ab_im
    bbr_ref[...] = f_re * bre_ref[...] - f_im * bim_ref[...]
    bbi_ref[...] = f_re * bim_ref[...] + f_im * bre_ref[...]


def _ssm_params(log_dt, a_re, a_im, b_re, b_im):
    col = lambda a: a.reshape(SSM_N, 1)
    ldt = col(jnp.broadcast_to(log_dt[:, None], (SSM_GROUPS, SSM_STATE)))
    col_t = jax.ShapeDtypeStruct((SSM_N, 1), F32)
    mat_t = jax.ShapeDtypeStruct((SSM_N, SSM_GROUP), F32)
    ab_re, ab_im, bb_re, bb_im = pl.pallas_call(
        _ssm_param_body, out_shape=[col_t, col_t, mat_t, mat_t], name="ssm_params",
    )(ldt, col(a_re), col(a_im), b_re.reshape(SSM_N, SSM_GROUP), b_im.reshape(SSM_N, SSM_GROUP))
    return ab_re.reshape(1, SSM_N), ab_im.reshape(1, SSM_N), bb_re, bb_im


def _block_diag_in(bb):
    m = bb.reshape(SSM_GROUPS, SSM_STATE, SSM_GROUP).transpose(0, 2, 1)
    eye = jnp.eye(SSM_GROUPS, dtype=bb.dtype)
    return (eye[:, None, :, None] * m[:, :, None, :]).reshape(SSM_WIDTH, SSM_N)


def _block_diag_out(c):
    m = c.transpose(0, 2, 1)
    eye = jnp.eye(SSM_GROUPS, dtype=c.dtype)
    return (eye[:, None, :, None] * m[:, :, None, :]).reshape(SSM_N, SSM_WIDTH)


def _ssm_prompt_body(u_ref, bb_ref, cc_ref, d_ref, abr_ref, abi_ref, y_ref, hr_ref, hi_ref, bu_s, hs_s, st_s):
    tc = u_ref.shape[1]

    @pl.when(pl.program_id(1) == 0)
    def _():
        st_s[...] = jnp.zeros(st_s.shape, F32)

    u = u_ref[0]
    bu_s[...] = _dot(u.astype(BF16), bb_ref[...])
    ar, ai = abr_ref[...], abi_ref[...]

    def step(t, carry):
        hr, hi = carry
        nr = ar * hr - ai * hi + bu_s[pl.ds(t, 1), :SSM_N]
        ni = ar * hi + ai * hr + bu_s[pl.ds(t, 1), SSM_N:]
        hs_s[pl.ds(t, 1), :SSM_N] = nr
        hs_s[pl.ds(t, 1), SSM_N:] = ni
        return nr, ni

    hr, hi = lax.fori_loop(0, tc, step, (st_s[0:1, :], st_s[1:2, :]), unroll=8)
    st_s[0:1, :] = hr
    st_s[1:2, :] = hi
    y_ref[0] = _dot(hs_s[...].astype(BF16), cc_ref[...]) + d_ref[...] * u
    hr_ref[0] = hr
    hi_ref[0] = hi


def _ssm_prompt(u, bb, cc, d, ab_re, ab_im, tc=256):
    b, t, _ = u.shape
    st = jax.ShapeDtypeStruct((b, 1, SSM_N), F32)
    st_spec = pl.BlockSpec((1, 1, SSM_N), lambda i, j: (i, 0, 0))
    return pl.pallas_call(
        _ssm_prompt_body,
        grid=(b, t // tc),
        in_specs=[pl.BlockSpec((1, tc, SSM_WIDTH), lambda i, j: (i, j, 0)), _const_spec(bb.shape),
                  _const_spec(cc.shape), _const_spec(d.shape), _const_spec(ab_re.shape), _const_spec(ab_im.shape)],
        out_specs=[pl.BlockSpec((1, tc, SSM_WIDTH), lambda i, j: (i, j, 0)), st_spec, st_spec],
        out_shape=[jax.ShapeDtypeStruct((b, t, SSM_WIDTH), F32), st, st],
        scratch_shapes=[pltpu.VMEM((tc, 2 * SSM_N), F32), pltpu.VMEM((tc, 2 * SSM_N), F32),
                        pltpu.VMEM((8, SSM_N), F32)],
        compiler_params=_cparams("parallel", "arbitrary"),
        name="ssm_prompt",
    )(u, bb, cc, d, ab_re, ab_im)


FF_CHUNK = 256


def _mix_out(x, o_nsa, y_ssm, wglu_ref, bglu_ref, wout_ref):
    z = _dot(_gelu_tanh(y_ssm).astype(BF16), wglu_ref[...]) + bglu_ref[...]
    glu = z[:, :SSM_WIDTH] * _sigmoid(z[:, SSM_WIDTH:])
    return x + _dot(o_nsa, wout_ref[:NSA_WIDTH, :]) + _dot(glu.astype(BF16), wout_ref[NSA_WIDTH:, :])


def _ffn_chunks(hn, wup_ref, cw_ref, cb_ref, wdown_ref, prev_rows):
    acc = jnp.zeros((hn.shape[0], D_MODEL), F32)
    for j in range(D_FF // FF_CHUNK):
        conv = []
        for base in (0, D_FF):
            lo = base + j * FF_CHUNK
            hi = lo + FF_CHUNK
            hu = _dot(hn, wup_ref[:, lo:hi])
            hu2, hu1 = prev_rows(lo, hi, hu)
            conv.append(cw_ref[0:1, lo:hi] * hu2 + cw_ref[1:2, lo:hi] * hu1 + cw_ref[2:3, lo:hi] * hu
                        + cb_ref[:, lo:hi])
        a, g = conv
        act = (a * _sigmoid(a) * g).astype(BF16)
        acc = acc + _dot(act, wdown_ref[j * FF_CHUNK:(j + 1) * FF_CHUNK, :])
    return acc


def _tail_prompt_body(x_ref, o_ref, y_ref, wglu_ref, bglu_ref, wout_ref, nf_ref, wup_ref, cw_ref, cb_ref, wdown_ref,
                      nfin_ref, out_ref, cs_ref, prev_s):
    tm = x_ref.shape[1]

    @pl.when(pl.program_id(1) == 0)
    def _():
        prev_s[...] = jnp.zeros(prev_s.shape, F32)

    x1 = _mix_out(x_ref[0], o_ref[0], y_ref[0], wglu_ref, bglu_ref, wout_ref)
    hn = _rms(x1, nf_ref[...]).astype(BF16)
    row = lax.broadcasted_iota(jnp.int32, (tm, 1), 0)

    def prev_rows(lo, hi, hu):
        p2, p1 = prev_s[6:7, lo:hi], prev_s[7:8, lo:hi]
        hu1 = jnp.where(row == 0, p1, pltpu.roll(hu, 1, 0))
        hu2 = jnp.where(row == 0, p2, jnp.where(row == 1, p1, pltpu.roll(hu, 2, 0)))
        prev_s[:, lo:hi] = hu[tm - 8:, :]
        cs_ref[0, :, lo:hi] = hu[tm - (CONV_W - 1):, :]
        return hu2, hu1

    x2 = x1 + _ffn_chunks(hn, wup_ref, cw_ref, cb_ref, wdown_ref, prev_rows)
    out_ref[0] = _rms(x2, nfin_ref[...])


def _tail_prompt(x, o_nsa, y_ssm, w, tm=512):
    b, t, _ = x.shape
    tile = lambda width: pl.BlockSpec((1, tm, width), lambda i, j: (i, j, 0))
    consts = [w["w_glu"], w["b_glu"], w["w_out"], w["norm_ffn"], w["w_up"], w["conv_w"], w["conv_b"], w["w_down"],
              w["norm_final"]]
    return pl.pallas_call(
        _tail_prompt_body,
        grid=(b, t // tm),
        in_specs=[tile(D_MODEL), tile(NSA_WIDTH), tile(SSM_WIDTH)] + [_const_spec(c.shape) for c in consts],
        out_specs=[tile(D_MODEL), pl.BlockSpec((1, CONV_W - 1, 2 * D_FF), lambda i, j: (i, 0, 0))],
        out_shape=[jax.ShapeDtypeStruct((b, t, D_MODEL), F32), jax.ShapeDtypeStruct((b, CONV_W - 1, 2 * D_FF), F32)],
        scratch_shapes=[pltpu.VMEM((8, 2 * D_FF), F32)],
        compiler_params=_cparams("parallel", "arbitrary"),
        name="tail_prompt",
    )(x, o_nsa, y_ssm, *consts)


def _pair_rows(q8, lane):
    row = lax.broadcasted_iota(jnp.int32, (8, 1), 0)
    out = jnp.zeros((8, LANES), F32)
    for r in range(N_HEADS):
        pair = q8[:, (r // 2) * LANES:(r // 2 + 1) * LANES]
        want_hi = r // GQA
        if r % 2 != want_hi:
            pair = pltpu.roll(pair, HEAD_DIM, 1)
        keep = (lane >= HEAD_DIM) if want_hi else (lane < HEAD_DIM)
        out = out + jnp.where(row == r, jnp.where(keep, pair, 0.0), 0.0)
    return out


def _unpair_rows(o, lane):
    pieces = []
    for j in range(N_HEADS // 2):
        lo, hi = o[2 * j:2 * j + 1], o[2 * j + 1:2 * j + 2]
        if (2 * j) // GQA == 0:
            hi = pltpu.roll(hi, HEAD_DIM, 1)
        else:
            lo = pltpu.roll(lo, HEAD_DIM, 1)
        pieces.append(jnp.where(lane < HEAD_DIM, lo, hi))
    return jnp.concatenate(pieces, axis=1)


PAIR_GROUP = 4


def _cmp_sample_body(pt_ref, pool_ref, new_ref, q_ref, bias_ref, w_ref, perm_ref, selmap_ref, oc_ref, idx_ref, buf, xs,
                     sem, *,
                     n_pages, n_seq):
    b = pl.program_id(0)
    n_pos = n_pages * PAGE_SIZE
    n_sub = n_pos // CMP_STRIDE
    half = N_KV_HEADS * HEAD_DIM
    sub_per_page = PAGE_SIZE // CMP_STRIDE

    def page_copy(seq, slot, p):
        dst = buf.at[slot, p // 2, :, :, :, pl.ds((p % 2) * PAGE_SIZE, PAGE_SIZE)]
        return pltpu.make_async_copy(pool_ref.at[pt_ref[seq * n_pages + p]], dst, sem.at[slot])

    def fetch(seq, slot):
        for p in range(n_pages):
            page_copy(seq, slot, p).start()

    @pl.when(b == 0)
    def _():
        fetch(0, 0)

    @pl.when(b + 1 < n_seq)
    def _():
        fetch(b + 1, (b + 1) % 2)

    slot = b % 2
    for p in range(n_pages):
        page_copy(b, slot, p).wait()

    row = lax.broadcasted_iota(jnp.int32, (n_sub, 1), 0)
    kv_c = []
    for x in range(2):
        for g0 in range(0, n_pages // 2, PAIR_GROUP):
            pairs = buf[slot, g0:g0 + PAIR_GROUP, x].reshape(PAIR_GROUP * half, 2 * PAGE_SIZE).astype(BF16)
            regrouped = _dot(pairs, perm_ref[...])
            for j in range(PAIR_GROUP):
                for k in range(2):
                    xs[x, 2 * (g0 + j) + k] = regrouped[j * half:(j + 1) * half, k * PAGE_SIZE:(k + 1) * PAGE_SIZE].T
        sub = jnp.concatenate([xs[x, :, s * sub_per_page:(s + 1) * sub_per_page, :].reshape(n_sub, half)
                               for s in range(CMP_STRIDE)], axis=1)
        parts = _dot(sub.astype(BF16), w_ref[x])
        new = jnp.broadcast_to(new_ref[0][:, x * half:(x + 1) * half], (8, half)).astype(BF16)
        new_part = _dot(new, w_ref[x, :half, :])[0:1, half:]
        nxt = jnp.where(row == n_sub - 1, new_part, pltpu.roll(parts[:, half:], n_sub - 1, 0))
        kv_c.append((parts[:, :half] + nxt + bias_ref[0:1, x * half:(x + 1) * half]).astype(BF16))

    q_pos = n_pos
    n_sel = selmap_ref.shape[1]
    n_real = q_pos // SEL_BLOCK + 1
    cmp_end = lax.broadcasted_iota(jnp.int32, (1, n_sub), 1) * CMP_STRIDE + (CMP_LEN - 1)
    blk = lax.broadcasted_iota(jnp.int32, (1, n_sel), 1)
    lane = lax.broadcasted_iota(jnp.int32, (1, LANES), 1)
    row8 = lax.broadcasted_iota(jnp.int32, (8, 1), 0)
    q2 = _pair_rows(jnp.broadcast_to(q_ref[0].astype(F32), (8, NSA_WIDTH)), lane).astype(BF16)
    p_c = _masked_softmax_rows(_nt_dot(q2, kv_c[0]), cmp_end <= q_pos)
    oc_ref[0] = _unpair_rows(_dot(p_c.astype(BF16), kv_c[1]), lane)
    p_sum = jnp.zeros((8, n_sub), F32)
    for h in range(N_KV_HEADS):
        in_h = (row8 >= h * GQA) & (row8 < (h + 1) * GQA)
        p_sum = p_sum + jnp.where(row8 == h, jnp.sum(jnp.where(in_h, p_c, 0.0), axis=0, keepdims=True), 0.0)
    score = _block_scores(_split_dot(p_sum, selmap_ref[...]), blk, q_pos)
    score = jnp.where((blk < n_real) & (row8 < N_KV_HEADS), score, BELOW_NEG)
    blk_f = blk.astype(F32)
    picked = jnp.full((8, LANES), -1.0, F32)
    for it in range(min(N_SELECT, n_real)):
        mx = jnp.max(score, axis=-1, keepdims=True)
        idx = jnp.min(jnp.where(score == mx, blk_f, float(n_sel)), axis=-1, keepdims=True)
        picked = jnp.where(lane == it, jnp.where(mx > 0.5 * NEG, idx, -1.0), picked)
        score = jnp.where(blk_f == idx, BELOW_NEG, score)
    idx_ref[0] = picked.astype(jnp.int32)


def _cmp_sample(page_table, pool_t, kvc_new, qb, bias, w_pos, selmap):
    n_seq, n_pages = page_table.shape
    half = N_KV_HEADS * HEAD_DIM
    pos = jnp.arange(PAGE_SIZE)
    dest = (pos % CMP_STRIDE) * (PAGE_SIZE // CMP_STRIDE) + pos // CMP_STRIDE
    perm = (dest[:, None] == jnp.arange(PAGE_SIZE)[None, :]).astype(BF16)
    zero = jnp.zeros_like(perm)
    perm = jnp.concatenate([jnp.concatenate([perm, zero], axis=1), jnp.concatenate([zero, perm], axis=1)], axis=0)
    grid_spec = pltpu.PrefetchScalarGridSpec(
        num_scalar_prefetch=1,
        grid=(n_seq,),
        in_specs=[pl.BlockSpec(memory_space=pl.ANY),
                  pl.BlockSpec((1, 1, KV_WIDTH), lambda i, pt: (i, 0, 0)),
                  pl.BlockSpec((1, 1, NSA_WIDTH), lambda i, pt: (i, 0, 0)),
                  _const_spec(bias.shape), _const_spec(w_pos.shape), _const_spec(perm.shape), _const_spec(selmap.shape)],
        out_specs=[pl.BlockSpec((1, 1, NSA_WIDTH), lambda i, pt: (i, 0, 0)),
                   pl.BlockSpec((1, 8, LANES), lambda i, pt: (i, 0, 0))],
        scratch_shapes=[pltpu.VMEM((2, n_pages // 2, 2, N_KV_HEADS, HEAD_DIM, 2 * PAGE_SIZE), F32),
                        pltpu.VMEM((2, n_pages, PAGE_SIZE, half), F32), pltpu.SemaphoreType.DMA((2,))],
    )
    return pl.pallas_call(
        functools.partial(_cmp_sample_body, n_pages=n_pages, n_seq=n_seq),
        grid_spec=grid_spec,
        out_shape=[jax.ShapeDtypeStruct((n_seq, 1, NSA_WIDTH), F32), jax.ShapeDtypeStruct((n_seq, 8, LANES), jnp.int32)],
        compiler_params=_cparams("arbitrary"),
        name="cmp_sample",
    )(page_table.reshape(-1), pool_t, kvc_new, qb, bias, w_pos, perm, selmap)


def _attn_sample_body(idx_ref, pt_ref, pool_ref, q_ref, gl_ref, oc_ref, ksn_ref, kwn_ref, win_ref, o_ref, wout_ref,
                      buf, sem, *, n_pages, n_seq, k_sel):
    b = pl.program_id(0)
    blk_per_page = PAGE_SIZE // SEL_BLOCK
    n_past = n_pages * blk_per_page
    n_blk = N_KV_HEADS * k_sel

    def blk_copy(seq, slot, j):
        i = jnp.clip(idx_ref[seq * n_blk + j], 0, n_past - 1)
        page = pt_ref[seq * n_pages + i // blk_per_page]
        return pltpu.make_async_copy(pool_ref.at[page, :, j // k_sel], buf.at[slot, j], sem.at[slot])

    def fetch(seq, slot):
        for j in range(n_blk):
            blk_copy(seq, slot, j).start()

    @pl.when(b == 0)
    def _():
        fetch(0, 0)

    @pl.when(b + 1 < n_seq)
    def _():
        fetch(b + 1, (b + 1) % 2)

    slot = b % 2
    for j in range(n_blk):
        blk_copy(b, slot, j).wait()

    q = q_ref[0].astype(F32)
    gate = _sigmoid(gl_ref[0])
    o_c = oc_ref[0]
    ks_new = ksn_ref[0].astype(BF16).astype(F32)
    kw_new = kwn_ref[0].astype(BF16).astype(F32)
    n_win = win_ref.shape[-1]
    lane = lax.broadcasted_iota(jnp.int32, (1, PAGE_SIZE), 1)
    lane_blk = lax.shift_right_logical(lane, SEL_BLOCK.bit_length() - 1)
    head_out = []
    for h in range(N_KV_HEADS):
        k_lo, v_lo = h * HEAD_DIM, (N_KV_HEADS + h) * HEAD_DIM
        qs = _stack_heads_single(q, h)
        qf = qs.astype(F32)
        s_blocks, m_blocks = [], []
        has_new = jnp.zeros((1, 1), F32)
        for j in range(k_sel):
            i = idx_ref[b * n_blk + h * k_sel + j]
            ok = jnp.where((i >= 0) & (i < n_past), 1.0, 0.0)
            m_blocks.append(jnp.where(lane_blk == i % blk_per_page, ok, 0.0))
            has_new = jnp.maximum(has_new, jnp.where(i >= n_past, 1.0, 0.0))
            s_blocks.append(_dot(qs, buf[slot, h * k_sel + j, 0].astype(BF16)))
        past = jnp.concatenate(m_blocks, axis=1) > 0.5
        s = jnp.where(past, jnp.concatenate(s_blocks, axis=1), NEG)
        s_new = jnp.where(has_new > 0.5, jnp.sum(qf * ks_new[:, k_lo:k_lo + HEAD_DIM], axis=-1, keepdims=True), NEG)
        m = jnp.maximum(jnp.max(s, axis=-1, keepdims=True), s_new)
        p = jnp.where(past, jnp.exp(s - m), 0.0)
        p_new = jnp.where(has_new > 0.5, jnp.exp(s_new - m), 0.0)
        p_bf = p.astype(BF16)
        o_s = p_new.astype(BF16).astype(F32) * ks_new[:, v_lo:v_lo + HEAD_DIM]
        for j in range(k_sel):
            o_s = o_s + _nt_dot(p_bf[:, j * PAGE_SIZE:(j + 1) * PAGE_SIZE], buf[slot, h * k_sel + j, 1].astype(BF16))
        o_s = o_s * (1.0 / (jnp.sum(p, axis=-1, keepdims=True) + p_new))
        s = _dot(qs, win_ref[0, 0, h].astype(BF16))
        s_new = jnp.sum(qf * kw_new[:, k_lo:k_lo + HEAD_DIM], axis=-1, keepdims=True)
        m = jnp.maximum(jnp.max(s, axis=-1, keepdims=True), s_new)
        p, p_new = jnp.exp(s - m), jnp.exp(s_new - m)
        inv = 1.0 / (jnp.sum(p, axis=-1, keepdims=True) + p_new)
        o_w = (_nt_dot(p.astype(BF16), win_ref[0, 1, h].astype(BF16))
               + p_new.astype(BF16).astype(F32) * kw_new[:, v_lo:v_lo + HEAD_DIM]) * inv
        for g in range(GQA):
            hd = GQA * h + g
            col = hd * 3
            head_out.append(gate[:, col:col + 1] * o_c[:, hd * HEAD_DIM:(hd + 1) * HEAD_DIM]
                            + gate[:, col + 1:col + 2] * o_s[g:g + 1] + gate[:, col + 2:col + 3] * o_w[g:g + 1])
    o_ref[0] = jnp.concatenate(head_out, axis=-1).astype(BF16)
    d_row = lax.broadcasted_iota(jnp.int32, (HEAD_DIM, HEAD_DIM), 0)
    d_col = lax.broadcasted_iota(jnp.int32, (HEAD_DIM, HEAD_DIM), 1)
    pos = lax.broadcasted_iota(jnp.int32, (1, n_win), 1)
    new_rows = jnp.broadcast_to(kwn_ref[0], (HEAD_DIM, KV_WIDTH))
    for x in range(2):
        for h in range(N_KV_HEADS):
            lo = (x * N_KV_HEADS + h) * HEAD_DIM
            new_col = jnp.sum(jnp.where(d_row == d_col, new_rows[:, lo:lo + HEAD_DIM], 0.0), axis=-1, keepdims=True)
            wout_ref[0, x, h] = jnp.where(pos == n_win - 1, new_col, pltpu.roll(win_ref[0, x, h], n_win - 1, 1))


def _attn_sample(idx, page_table, pool_t, qb, gl, o_c, kvs_new, kvw_new, win_t, k_sel):
    n_seq, n_pages = page_table.shape
    n_win = win_t.shape[-1]
    n_blk = N_KV_HEADS * k_sel
    one = lambda w: pl.BlockSpec((1, 1, w), lambda i, a, p: (i, 0, 0))
    win_spec = pl.BlockSpec((1, 2, N_KV_HEADS, HEAD_DIM, n_win), lambda i, a, p: (i, 0, 0, 0, 0))
    grid_spec = pltpu.PrefetchScalarGridSpec(
        num_scalar_prefetch=2,
        grid=(n_seq,),
        in_specs=[pl.BlockSpec(memory_space=pl.ANY), one(NSA_WIDTH), one(GATE_PAD), one(NSA_WIDTH), one(KV_WIDTH),
                  one(KV_WIDTH), win_spec],
        out_specs=[one(NSA_WIDTH), win_spec],
        scratch_shapes=[pltpu.VMEM((2, n_blk, 2, HEAD_DIM, PAGE_SIZE), F32), pltpu.SemaphoreType.DMA((2,))],
    )
    return pl.pallas_call(
        functools.partial(_attn_sample_body, n_pages=n_pages, n_seq=n_seq, k_sel=k_sel),
        grid_spec=grid_spec,
        out_shape=[jax.ShapeDtypeStruct((n_seq, 1, NSA_WIDTH), BF16), jax.ShapeDtypeStruct(win_t.shape, F32)],
        compiler_params=_cparams("arbitrary"),
        name="attn_sample",
    )(idx, page_table.reshape(-1), pool_t, qb, gl, o_c, kvs_new, kvw_new, win_t)


def _tail_sample_body(x_ref, o_ref, u_ref, h0r_ref, h0i_ref, hist2_ref, hist1_ref, bb_ref, cc_ref, d_ref, abr_ref,
                      abi_ref, wglu_ref, bglu_ref, wout_ref, nf_ref, wup_ref, cw_ref, cb_ref, wdown_ref, nfin_ref,
                      out_ref, hr_ref, hi_ref, cs_ref):
    u = u_ref[...]
    bu = _split_dot(u, bb_ref[...])
    ar, ai = abr_ref[...], abi_ref[...]
    h0r, h0i = h0r_ref[...], h0i_ref[...]
    hr = ar * h0r - ai * h0i + bu[:, :SSM_N]
    hi = ar * h0i + ai * h0r + bu[:, SSM_N:]
    hr_ref[...] = hr
    hi_ref[...] = hi
    y = _dot(hr.astype(BF16), cc_ref[:SSM_N, :]) + _dot(hi.astype(BF16), cc_ref[SSM_N:, :]) + d_ref[...] * u
    x1 = _mix_out(x_ref[...], o_ref[...], y, wglu_ref, bglu_ref, wout_ref)
    hn = _rms(x1, nf_ref[...]).astype(BF16)

    def prev_rows(lo, hi_col, hu):
        cs_ref[:, lo:hi_col] = hist1_ref[:, lo:hi_col]
        cs_ref[:, 2 * D_FF + lo:2 * D_FF + hi_col] = hu
        return hist2_ref[:, lo:hi_col], hist1_ref[:, lo:hi_col]

    x2 = x1 + _ffn_chunks(hn, wup_ref, cw_ref, cb_ref, wdown_ref, prev_rows)
    out_ref[...] = _rms(x2, nfin_ref[...])


def _tail_sample(x, o_nsa, u, h0r, h0i, hist2, hist1, bb, cc, d, ab_re, ab_im, w):
    n = x.shape[0]
    sds = lambda width: jax.ShapeDtypeStruct((n, width), F32)
    return pl.pallas_call(
        _tail_sample_body,
        out_shape=[sds(D_MODEL), sds(SSM_N), sds(SSM_N), sds((CONV_W - 1) * 2 * D_FF)],
        compiler_params=pltpu.CompilerParams(vmem_limit_bytes=VMEM_LIMIT),
        name="tail_sample",
    )(x, o_nsa, u, h0r, h0i, hist2, hist1, bb, cc, d, ab_re, ab_im, w["w_glu"], w["b_glu"], w["w_out"],
      w["norm_ffn"], w["w_up"], w["conv_w"], w["conv_b"], w["w_down"], w["norm_final"])


def _pad_w_in(w_in):
    c = NSA_WIDTH + 3 * KV_WIDTH
    return jnp.concatenate([w_in[:, :c], w_in[:, c + N_GATES:], w_in[:, c:c + N_GATES],
                            jnp.zeros((D_MODEL, GATE_PAD - N_GATES), w_in.dtype)], axis=1).astype(BF16)


def _cmp_weight(w_cmp):
    w = w_cmp.reshape(2, CMP_LEN // CMP_STRIDE, CMP_STRIDE, HEAD_DIM, HEAD_DIM)
    eye_x = jnp.eye(2, dtype=w.dtype)
    eye_h = jnp.eye(N_KV_HEADS, dtype=w.dtype)
    big = (w.transpose(2, 0, 3, 1, 4)[:, :, None, :, :, None, None, :]
           * eye_x[None, :, None, None, None, :, None, None] * eye_h[None, None, :, None, None, None, :, None])
    return big.reshape(SUB_W, 2 * KV_WIDTH).astype(BF16)


def _cmp_weight_pos(w_cmp):
    w = w_cmp.reshape(2, CMP_LEN // CMP_STRIDE, CMP_STRIDE, HEAD_DIM, HEAD_DIM)
    eye_h = jnp.eye(N_KV_HEADS, dtype=w.dtype)
    big = (w.transpose(0, 2, 3, 1, 4)[:, :, None, :, :, None, :]
           * eye_h[None, None, :, None, None, :, None])
    half = N_KV_HEADS * HEAD_DIM
    return big.reshape(2, CMP_STRIDE * half, 2 * half).astype(BF16)


def _pe_sub(pe_cmp):
    rows = jnp.broadcast_to(pe_cmp.transpose(1, 0, 2)[:, :, None, :], (CMP_LEN, 2, N_KV_HEADS, HEAD_DIM))
    sub = rows.reshape(CMP_LEN // CMP_STRIDE, SUB_W)
    return jnp.concatenate([sub, jnp.zeros((8 - sub.shape[0], SUB_W), sub.dtype)], axis=0)


def _sel_map(n_cmp, n_sel, n_sel_pad):
    c0 = (jnp.arange(n_cmp) * CMP_STRIDE)[:, None]
    s0 = (jnp.arange(n_sel_pad) * SEL_BLOCK)[None, :]
    hit = (c0 < s0 + SEL_BLOCK) & (c0 + CMP_LEN > s0) & (jnp.arange(n_sel_pad)[None, :] < n_sel)
    return hit.astype(BF16)


def _expand_map(n_sel, t, kc_len):
    hit = jnp.arange(t)[None, :] // SEL_BLOCK == jnp.arange(n_sel)[:, None]
    return hit.astype(BF16).reshape(n_sel, t // kc_len, kc_len).transpose(1, 0, 2)


def kernel(x_prompt, x_sample, cache_kv_cmp, cache_kv_sel, cache_kv_win, state_ssm_re, state_ssm_im, state_ffn_conv, page_table, norm_mix, w_in, pe_cmp, w_cmp, ssm_a_re, ssm_a_im, ssm_log_dt, ssm_b_re, ssm_b_im, ssm_c_re, ssm_c_im, ssm_d, w_glu, b_glu, w_out, norm_ffn, w_up, conv_w, conv_b, w_down, norm_final):
    depth = w_in.shape[0]
    assert depth == 1, "single-layer trunk"
    b, t, _ = x_prompt.shape
    bd, s, _ = x_sample.shape
    assert s == 1, "one new position per sample sequence"
    n_pages = page_table.shape[1]
    l = 0
    w_pad = _pad_w_in(w_in[l])
    g_mix = norm_mix[l].reshape(1, D_MODEL)
    w_big = _cmp_weight(w_cmp[l])
    pe_sub = _pe_sub(pe_cmp[l])
    ab_re, ab_im, bb_re, bb_im = _ssm_params(ssm_log_dt[l], ssm_a_re[l], ssm_a_im[l], ssm_b_re[l], ssm_b_im[l])
    bb = jnp.concatenate([_block_diag_in(bb_re), _block_diag_in(bb_im)], axis=1).astype(BF16)
    cc = jnp.concatenate([_block_diag_out(ssm_c_re[l]), -_block_diag_out(ssm_c_im[l])], axis=0).astype(BF16)
    d_row = ssm_d[l].reshape(1, SSM_WIDTH)
    tail_w = {"w_glu": w_glu[l].astype(BF16), "b_glu": b_glu[l].reshape(1, -1), "w_out": w_out[l].astype(BF16),
              "norm_ffn": norm_ffn[l].reshape(1, -1), "w_up": w_up[l].astype(BF16), "conv_w": conv_w[l],
              "conv_b": conv_b[l].reshape(1, -1), "w_down": w_down[l].astype(BF16),
              "norm_final": norm_final.reshape(1, -1)}

    kvc, _, _, kvct, kvst, kvwt, kstb, kwtb, gl, u, qb = _inproj(x_prompt, g_mix, w_pad, 512)
    n_sub = t // CMP_STRIDE
    kc, cmp_bias = _cmp_prompt(kvc.reshape(b, n_sub, SUB_W), pe_sub, w_big)
    n_sel = t // SEL_BLOCK
    o_nsa = _attn_prompt(qb, gl, kc, kstb, kwtb, _sel_map(n_sub, n_sel, n_sel).T)
    y_ssm, p_hr, p_hi = _ssm_prompt(u, bb, cc, d_row, ab_re, ab_im)
    y_prompt, p_conv = _tail_prompt(x_prompt, o_nsa, y_ssm, tail_w)
    win_keep = min(WINDOW, t)
    kv_rows = lambda a: a.reshape(a.shape[0], 2, N_KV_HEADS, HEAD_DIM, a.shape[2]).transpose(0, 4, 1, 2, 3)[None]
    st_shape = (depth, b, SSM_GROUPS, SSM_STATE)

    kvc_n, kvs_n, kvw_n, kvct_n, kvst_n, _, _, _, gl_n, u_n, qb_n = _inproj(x_sample.reshape(1, bd, D_MODEL), g_mix, w_pad, bd)
    per_seq = lambda a: a.reshape(bd, 1, a.shape[-1])
    n_sub_s = n_pages * (PAGE_SIZE // CMP_STRIDE)
    n_sel_s = n_pages * (PAGE_SIZE // SEL_BLOCK) + 1
    n_sel_pad = -(-n_sel_s // LANES) * LANES
    k_sel = min(N_SELECT, n_sel_s)
    pos_minor = lambda a: a.transpose(0, 2, 3, 4, 1)
    o_c, picked = _cmp_sample(page_table, pos_minor(cache_kv_cmp[l]), per_seq(kvc_n), per_seq(qb_n), cmp_bias,
                              _cmp_weight_pos(w_cmp[l]), _sel_map(n_sub_s, n_sel_s, n_sel_pad))
    idx = picked[:, :N_KV_HEADS, :k_sel].reshape(-1)
    n_buf = cache_kv_win.shape[2]
    assert n_buf == WINDOW, "window buffer holds exactly WINDOW rows"
    o_nsa_s, s_win = _attn_sample(idx, page_table, pos_minor(cache_kv_sel[l]), per_seq(qb_n), per_seq(gl_n), o_c,
                                  per_seq(kvs_n), per_seq(kvw_n), pos_minor(cache_kv_win[l]), k_sel)
    hist = state_ffn_conv[l]
    y_sample, s_hr, s_hi, s_conv = _tail_sample(
        x_sample.reshape(bd, D_MODEL), o_nsa_s.reshape(bd, -1), u_n.reshape(bd, -1),
        state_ssm_re[l].reshape(bd, SSM_N), state_ssm_im[l].reshape(bd, SSM_N), hist[:, 0], hist[:, 1], bb, cc, d_row,
        ab_re, ab_im, tail_w)
    kv_rows_s = lambda a: kv_rows(a).reshape(depth, bd, s, 2, N_KV_HEADS, HEAD_DIM)
    st_shape_s = (depth, bd, SSM_GROUPS, SSM_STATE)
    return (y_prompt, y_sample.reshape(bd, s, D_MODEL),
            kv_rows(kvct), kv_rows(kvst), kv_rows(kvwt[:, :, t - win_keep:]),
            p_hr.reshape(st_shape), p_hi.reshape(st_shape), p_conv.reshape(depth, b, CONV_W - 1, 2 * D_FF),
            kv_rows_s(kvct_n), kv_rows_s(kvst_n),
            s_win.transpose(0, 4, 1, 2, 3)[None],
            s_hr.reshape(st_shape_s), s_hi.reshape(st_shape_s), s_conv.reshape(depth, bd, CONV_W - 1, 2 * D_FF))
```

```python
import functools
import math

import jax
import jax.numpy as jnp
from jax import lax
from jax.experimental import pallas as pl
from jax.experimental.pallas import tpu as pltpu

D_MODEL = 1024
N_HEADS = 8
N_KV_HEADS = 2
GQA = N_HEADS // N_KV_HEADS
HEAD_DIM = 64
NSA_WIDTH = N_HEADS * HEAD_DIM
KV_WIDTH = 2 * N_KV_HEADS * HEAD_DIM
N_GATES = 3 * N_HEADS
CMP_LEN = 32
CMP_STRIDE = 16
SEL_BLOCK = 64
N_SELECT = 16
WINDOW = 512
Q_BLOCK = 128
PAGE_SIZE = 128
SSM_WIDTH = D_MODEL - NSA_WIDTH
SSM_GROUP = 16
SSM_GROUPS = SSM_WIDTH // SSM_GROUP
SSM_STATE = 64
SSM_N = SSM_GROUPS * SSM_STATE
D_FF = (D_MODEL * 11 // 4 + 127) // 128 * 128
CONV_W = 3
EPS = 1e-6
NEG = -1e30
BIG = 1e30
BELOW_NEG = -3e38

LANES = 128
GATE_PAD = LANES
IN_PAD = NSA_WIDTH + 3 * KV_WIDTH + SSM_WIDTH + GATE_PAD
SUB_W = CMP_STRIDE * KV_WIDTH
VMEM_LIMIT = 56 * 1024 * 1024

F32 = jnp.float32
BF16 = jnp.bfloat16


def _nt_dot(a, b):
    return lax.dot_general(a, b, (((1,), (1,)), ((), ())), preferred_element_type=F32)


def _dot(a, b):
    return jnp.dot(a, b, preferred_element_type=F32)


def _sigmoid(x):
    return 1.0 / (1.0 + jnp.exp(-x))


def _gelu_tanh(x):
    return 0.5 * x * (1.0 + jnp.tanh(math.sqrt(2.0 / math.pi) * (x + 0.044715 * (x * x * x))))


def _rms(x, g):
    return x * lax.rsqrt(jnp.mean(x * x, axis=-1, keepdims=True) + EPS) * g


def _split_dot(x, w_bf):
    hi = x.astype(BF16)
    lo = (x - hi.astype(F32)).astype(BF16)
    return _dot(hi, w_bf) + _dot(lo, w_bf)


def _cparams(*sem):
    return pltpu.CompilerParams(dimension_semantics=sem, vmem_limit_bytes=VMEM_LIMIT)


def _const_spec(shape):
    nd = len(shape)
    return pl.BlockSpec(shape, lambda *_: (0,) * nd, pipeline_mode=pl.Buffered(1))


def _inproj_body(x_ref, g_ref, w_ref, kvc_ref, kvs_ref, kvw_ref, kvct_ref, kvst_ref, kvwt_ref, kstb_ref, kwtb_ref,
                 gl_ref, u_ref, qb_ref):
    h = _rms(x_ref[0], g_ref[...])
    z = _dot(h.astype(BF16), w_ref[...])
    tm = z.shape[0]
    c = NSA_WIDTH
    qb_ref[0] = (z[:, :c] * (HEAD_DIM ** -0.5)).astype(BF16)
    for rm_ref, t_ref, tb_ref in ((kvc_ref, kvct_ref, None), (kvs_ref, kvst_ref, kstb_ref),
                                  (kvw_ref, kvwt_ref, kwtb_ref)):
        rows = z[:, c:c + KV_WIDTH]
        c += KV_WIDTH
        rm_ref[0] = rows
        cols = rows.T
        t_ref[0] = cols
        if tb_ref is not None:
            cols_bf = cols.astype(BF16)
            for k in range(tm // LANES):
                tb_ref[0, k] = cols_bf[:, k * LANES:(k + 1) * LANES]
    u_ref[0] = z[:, c:c + SSM_WIDTH]
    gl_ref[0] = z[:, c + SSM_WIDTH:]


def _inproj(x, g, w_pad, tm):
    b, t, _ = x.shape
    row = lambda w: pl.BlockSpec((1, tm, w), lambda i, j: (i, j, 0))
    col = pl.BlockSpec((1, KV_WIDTH, tm), lambda i, j: (i, 0, j))
    chunk = pl.BlockSpec((1, tm // LANES, KV_WIDTH, LANES), lambda i, j: (i, j, 0, 0))
    rm_t = jax.ShapeDtypeStruct((b, t, KV_WIDTH), F32)
    col_t = jax.ShapeDtypeStruct((b, KV_WIDTH, t), F32)
    chunk_t = jax.ShapeDtypeStruct((b, t // LANES, KV_WIDTH, LANES), BF16)
    return pl.pallas_call(
        _inproj_body,
        grid=(b, t // tm),
        in_specs=[row(D_MODEL), _const_spec((1, D_MODEL)), _const_spec((D_MODEL, IN_PAD))],
        out_specs=[row(KV_WIDTH)] * 3 + [col] * 3 + [chunk] * 2 + [row(GATE_PAD), row(SSM_WIDTH), row(NSA_WIDTH)],
        out_shape=[rm_t] * 3 + [col_t] * 3 + [chunk_t] * 2
        + [jax.ShapeDtypeStruct((b, t, GATE_PAD), F32), jax.ShapeDtypeStruct((b, t, SSM_WIDTH), F32),
           jax.ShapeDtypeStruct((b, t, NSA_WIDTH), BF16)],
        compiler_params=_cparams("parallel", "parallel"),
        name="in_proj",
    )(x, g, w_pad)


def _cmp_prompt_body(sub_ref, pe_ref, w_ref, kc_ref, bias_ref):
    w = w_ref[...]
    parts = _dot(sub_ref[0].astype(BF16), w)
    pe = _dot(pe_ref[...].astype(BF16), w)
    bias = pe[0:1, :KV_WIDTH] + pe[1:2, KV_WIDTH:]
    n_sub = parts.shape[0]
    nxt = pltpu.roll(parts[:, KV_WIDTH:], n_sub - 1, 0)
    kc_ref[0] = (parts[:, :KV_WIDTH] + nxt + bias).astype(BF16)
    bias_ref[...] = jnp.broadcast_to(bias, bias_ref.shape)


def _cmp_prompt(sub, pe_sub, w_big):
    b, n_sub, _ = sub.shape
    return pl.pallas_call(
        _cmp_prompt_body,
        grid=(b,),
        in_specs=[pl.BlockSpec((1, n_sub, SUB_W), lambda i: (i, 0, 0)), _const_spec(pe_sub.shape),
                  _const_spec(w_big.shape)],
        out_specs=[pl.BlockSpec((1, n_sub, KV_WIDTH), lambda i: (i, 0, 0)),
                   pl.BlockSpec((8, KV_WIDTH), lambda i: (0, 0))],
        out_shape=[jax.ShapeDtypeStruct((b, n_sub, KV_WIDTH), BF16), jax.ShapeDtypeStruct((8, KV_WIDTH), F32)],
        compiler_params=_cparams("arbitrary"),
        name="cmp_prompt",
    )(sub, pe_sub, w_big)


def _stack_heads(q, h):
    return jnp.concatenate([q[:, (GQA * h + g) * HEAD_DIM:(GQA * h + g + 1) * HEAD_DIM] for g in range(GQA)], axis=0)


def _stack_heads_single(q, h):
    row = lax.broadcasted_iota(jnp.int32, (8, 1), 0)
    q8 = jnp.broadcast_to(q, (8, NSA_WIDTH))
    out = jnp.zeros((8, HEAD_DIM), F32)
    for g in range(GQA):
        lo = (GQA * h + g) * HEAD_DIM
        out = out + jnp.where(row == g, q8[:, lo:lo + HEAD_DIM], 0.0)
    return out.astype(BF16)


def _masked_softmax_rows(s, mask):
    sm = jnp.where(mask, s, NEG)
    m = jnp.max(sm, axis=-1, keepdims=True)
    p = jnp.where(mask, jnp.exp(sm - m), 0.0)
    l = jnp.sum(p, axis=-1, keepdims=True)
    return p * (1.0 / jnp.where(l > 0.0, l, 1.0))


def _block_scores(imp, blk, qpos):
    first = blk * SEL_BLOCK
    own_or_imp = jnp.where(first + SEL_BLOCK > qpos, BIG, imp)
    return jnp.where(blk == 0, BIG, jnp.where(first <= qpos, own_or_imp, NEG))


def _masked_softmax_cols(s, mask):
    sm = jnp.where(mask, s, NEG)
    m = jnp.max(sm, axis=0, keepdims=True)
    p = jnp.where(mask, jnp.exp(sm - m), 0.0)
    l = jnp.sum(p, axis=0, keepdims=True)
    return p * (1.0 / jnp.where(l > 0.0, l, 1.0))


def _top_k_mask_cols(score, blk, k):
    n = float(score.shape[0])
    sel = jnp.zeros(score.shape, F32)
    for _ in range(k):
        mx = jnp.max(score, axis=0, keepdims=True)
        idx = jnp.min(jnp.where(score == mx, blk, n), axis=0, keepdims=True)
        hit = blk == idx
        sel = sel + jnp.where(hit, jnp.where(mx > 0.5 * NEG, 1.0, 0.0), 0.0)
        score = jnp.where(hit, BELOW_NEG, score)
    return sel


KEY_BLOCKS = 4
ROW_BLOCK = 32


def _attn_prompt_body(q_ref, gl_ref, kc_ref, kst_ref, kwt_ref, selmap_ref, exp_ref, o_ref, m_s, l_s, a_s, acc_s, s_s,
                      b_s, p_s):
    qb_idx = pl.program_id(1)
    start = qb_idx * Q_BLOCK
    kc_len = KEY_BLOCKS * LANES
    q = q_ref[0]
    gate = _sigmoid(gl_ref[0])
    qpos = start + lax.broadcasted_iota(jnp.int32, (Q_BLOCK, 1), 0)
    qpos_l = start + lax.broadcasted_iota(jnp.int32, (1, Q_BLOCK), 1)
    qpos4_l = jnp.concatenate([qpos_l] * GQA, axis=1)
    n_cmp = kc_ref.shape[1]
    n_sel = selmap_ref.shape[0]
    cmp_end = lax.broadcasted_iota(jnp.int32, (n_cmp, 1), 0) * CMP_STRIDE + (CMP_LEN - 1)
    blk = lax.broadcasted_iota(jnp.int32, (n_sel, 1), 0)
    qs, o_c, scores = [], [], []
    for h in range(N_KV_HEADS):
        k_lo, v_lo = h * HEAD_DIM, (N_KV_HEADS + h) * HEAD_DIM
        qs.append(_stack_heads(q, h))
        p_t = _masked_softmax_cols(_nt_dot(kc_ref[0, :, k_lo:k_lo + HEAD_DIM], qs[h]), cmp_end <= qpos4_l)
        o_c.append(_dot(p_t.T.astype(BF16), kc_ref[0, :, v_lo:v_lo + HEAD_DIM]))
        p_sum = p_t[:, 0:Q_BLOCK]
        for g in range(1, GQA):
            p_sum = p_sum + p_t[:, g * Q_BLOCK:(g + 1) * Q_BLOCK]
        hi = p_sum.astype(BF16)
        lo = (p_sum - hi.astype(F32)).astype(BF16)
        imp = _dot(selmap_ref[...], hi) + _dot(selmap_ref[...], lo)
        scores.append(_block_scores(imp, blk, qpos_l))
    sel_t = _top_k_mask_cols(jnp.concatenate(scores, axis=1), blk.astype(F32), min(N_SELECT, n_sel))
    sel = [sel_t[:, h * Q_BLOCK:(h + 1) * Q_BLOCK].T.astype(BF16) for h in range(N_KV_HEADS)]
    m_s[...] = jnp.full(m_s.shape, NEG, F32)
    l_s[...] = jnp.zeros(l_s.shape, F32)
    acc_s[...] = jnp.zeros(acc_s.shape, F32)
    n_chunks = (start + Q_BLOCK + kc_len - 1) // kc_len

    def sel_chunk(it, carry):
        c = n_chunks - 1 - it
        kpos = c * kc_len + lax.broadcasted_iota(jnp.int32, (1, kc_len), 1)
        for h in range(N_KV_HEADS):
            k_lo, v_lo = h * HEAD_DIM, (N_KV_HEADS + h) * HEAD_DIM
            k_t = jnp.concatenate([kst_ref[0, c * KEY_BLOCKS + j, k_lo:k_lo + HEAD_DIM, :]
                                   for j in range(KEY_BLOCKS)], axis=1)
            v_t = jnp.concatenate([kst_ref[0, c * KEY_BLOCKS + j, v_lo:v_lo + HEAD_DIM, :]
                                   for j in range(KEY_BLOCKS)], axis=1)
            s_s[h] = _dot(qs[h], k_t)
            chosen = _dot(sel[h], exp_ref[c])
            b_s[h] = jnp.where(kpos <= qpos, jnp.where(chosen > 0.5, 0.0, NEG), NEG)
            for r0 in range(0, GQA * Q_BLOCK, ROW_BLOCK):
                rows = slice(r0, r0 + ROW_BLOCK)
                q0 = r0 % Q_BLOCK
                sm = s_s[h, rows, :] + b_s[h, q0:q0 + ROW_BLOCK, :]
                s_s[h, rows, :] = sm
                m_prev = m_s[h, rows, :]
                m_next = jnp.maximum(m_prev, jnp.max(sm, axis=-1, keepdims=True))
                a_s[h, rows, :] = jnp.exp(m_prev - m_next)
                m_s[h, rows, :] = m_next
            for r0 in range(0, GQA * Q_BLOCK, ROW_BLOCK):
                rows = slice(r0, r0 + ROW_BLOCK)
                p = jnp.exp(s_s[h, rows, :] - jnp.tile(m_s[h, rows, :], (1, KEY_BLOCKS)))
                l_s[h, rows, :] = a_s[h, rows, :] * l_s[h, rows, :] + jnp.sum(p, axis=-1, keepdims=True)
                p_s[h, rows, :] = p.astype(BF16)
            acc_s[h] = acc_s[h] * a_s[h, :, :HEAD_DIM] + _nt_dot(p_s[h], v_t)
        return carry

    lax.fori_loop(0, n_chunks, sel_chunk, 0)
    wpos = start - WINDOW + lax.broadcasted_iota(jnp.int32, (1, WINDOW + Q_BLOCK), 1)
    visible = jnp.where(wpos <= qpos, jnp.where(wpos >= jnp.maximum(qpos - WINDOW, 0), 0.0, NEG), NEG)
    win_bias = jnp.concatenate([visible] * GQA, axis=0)
    head_out = []
    for h in range(N_KV_HEADS):
        k_lo, v_lo = h * HEAD_DIM, (N_KV_HEADS + h) * HEAD_DIM
        o_s = acc_s[h] * (1.0 / l_s[h])[:, :HEAD_DIM]
        kw, vw = [], []
        for c in range((WINDOW + Q_BLOCK) // LANES):
            src = jnp.maximum(qb_idx - WINDOW // LANES + c, 0)
            kw.append(kwt_ref[0, src, k_lo:k_lo + HEAD_DIM, :])
            vw.append(kwt_ref[0, src, v_lo:v_lo + HEAD_DIM, :])
        s = _dot(qs[h], jnp.concatenate(kw, axis=1))
        sm = s + win_bias
        p = jnp.exp(sm - jnp.max(sm, axis=-1, keepdims=True))
        o_w = _nt_dot(p.astype(BF16), jnp.concatenate(vw, axis=1)) * (1.0 / jnp.sum(p, axis=-1, keepdims=True))
        for g in range(GQA):
            col = (GQA * h + g) * 3
            r = slice(g * Q_BLOCK, (g + 1) * Q_BLOCK)
            head_out.append(gate[:, col:col + 1] * o_c[h][r] + gate[:, col + 1:col + 2] * o_s[r]
                            + gate[:, col + 2:col + 3] * o_w[r])
    o_ref[0] = jnp.concatenate(head_out, axis=-1).astype(BF16)


def _attn_prompt(qb, gl, kc, kst, kwt, selmap_t):
    b, t, _ = qb.shape
    kc_len = KEY_BLOCKS * LANES
    expand = _expand_map(selmap_t.shape[0], t, kc_len)
    rows = GQA * Q_BLOCK
    whole = lambda a: pl.BlockSpec((1,) + a.shape[1:], lambda i, j: (i,) + (0,) * (a.ndim - 1))
    return pl.pallas_call(
        _attn_prompt_body,
        grid=(b, t // Q_BLOCK),
        in_specs=[pl.BlockSpec((1, Q_BLOCK, NSA_WIDTH), lambda i, j: (i, j, 0)),
                  pl.BlockSpec((1, Q_BLOCK, GATE_PAD), lambda i, j: (i, j, 0)),
                  whole(kc), whole(kst), whole(kwt), _const_spec(selmap_t.shape), _const_spec(expand.shape)],
        out_specs=pl.BlockSpec((1, Q_BLOCK, NSA_WIDTH), lambda i, j: (i, j, 0)),
        out_shape=jax.ShapeDtypeStruct((b, t, NSA_WIDTH), BF16),
        scratch_shapes=[pltpu.VMEM((N_KV_HEADS, rows, LANES), F32), pltpu.VMEM((N_KV_HEADS, rows, LANES), F32),
                        pltpu.VMEM((N_KV_HEADS, rows, LANES), F32), pltpu.VMEM((N_KV_HEADS, rows, HEAD_DIM), F32),
                        pltpu.VMEM((N_KV_HEADS, rows, kc_len), F32), pltpu.VMEM((N_KV_HEADS, Q_BLOCK, kc_len), F32),
                        pltpu.VMEM((N_KV_HEADS, rows, kc_len), BF16)],
        compiler_params=_cparams("parallel", "arbitrary"),
        name="attn_prompt",
    )(qb, gl, kc, kst, kwt, selmap_t, expand)


def _ssm_param_body(ldt_ref, are_ref, aim_ref, bre_ref, bim_ref, abr_ref, abi_ref, bbr_ref, bbi_ref):
    dt = jnp.exp(ldt_ref[...])
    are, aim = are_ref[...], aim_ref[...]
    mag = jnp.exp(dt * are)
    ab_re, ab_im = mag * jnp.cos(dt * aim), mag * jnp.sin(dt * aim)
    den = are * are + aim * aim
    zr, zi = ab_re - 1.0, ab_im
    f_re = (zr * are + zi * aim) / den
    f_im = (zi * are - zr * aim) / den
    abr_ref[...] = ab_re
    abi_ref[...] = ab_im
    bbr_ref[...] = f_re * bre_ref[...] - f_im * bim_ref[...]
    bbi_ref[...] = f_re * bim_ref[...] + f_im * bre_ref[...]


def _ssm_params(log_dt, a_re, a_im, b_re, b_im):
    col = lambda a: a.reshape(SSM_N, 1)
    ldt = col(jnp.broadcast_to(log_dt[:, None], (SSM_GROUPS, SSM_STATE)))
    col_t = jax.ShapeDtypeStruct((SSM_N, 1), F32)
    mat_t = jax.ShapeDtypeStruct((SSM_N, SSM_GROUP), F32)
    ab_re, ab_im, bb_re, bb_im = pl.pallas_call(
        _ssm_param_body, out_shape=[col_t, col_t, mat_t, mat_t], name="ssm_params",
    )(ldt, col(a_re), col(a_im), b_re.reshape(SSM_N, SSM_GROUP), b_im.reshape(SSM_N, SSM_GROUP))
    return ab_re.reshape(1, SSM_N), ab_im.reshape(1, SSM_N), bb_re, bb_im


def _block_diag_in(bb):
    m = bb.reshape(SSM_GROUPS, SSM_STATE, SSM_GROUP).transpose(0, 2, 1)
    eye = jnp.eye(SSM_GROUPS, dtype=bb.dtype)
    return (eye[:, None, :, None] * m[:, :, None, :]).reshape(SSM_WIDTH, SSM_N)


def _block_diag_out(c):
    m = c.transpose(0, 2, 1)
    eye = jnp.eye(SSM_GROUPS, dtype=c.dtype)
    return (eye[:, None, :, None] * m[:, :, None, :]).reshape(SSM_N, SSM_WIDTH)


def _ssm_prompt_body(u_ref, bb_ref, cc_ref, d_ref, abr_ref, abi_ref, y_ref, hr_ref, hi_ref, bu_s, hs_s, st_s):
    tc = u_ref.shape[1]

    @pl.when(pl.program_id(1) == 0)
    def _():
        st_s[...] = jnp.zeros(st_s.shape, F32)

    u = u_ref[0]
    u_bf = u.astype(BF16)
    n_blocks = bb_ref.shape[0]
    ch, st = SSM_WIDTH // n_blocks, SSM_N // n_blocks
    for j in range(n_blocks):
        z = _dot(u_bf[:, j * ch:(j + 1) * ch], bb_ref[j])
        bu_s[:, j * st:(j + 1) * st] = z[:, :st]
        bu_s[:, SSM_N + j * st:SSM_N + (j + 1) * st] = z[:, st:]
    ar, ai = abr_ref[...], abi_ref[...]

    def step(t, carry):
        hr, hi = carry
        nr = ar * hr - ai * hi + bu_s[pl.ds(t, 1), :SSM_N]
        ni = ar * hi + ai * hr + bu_s[pl.ds(t, 1), SSM_N:]
        hs_s[pl.ds(t, 1), :SSM_N] = nr
        hs_s[pl.ds(t, 1), SSM_N:] = ni
        return nr, ni

    hr, hi = lax.fori_loop(0, tc, step, (st_s[0:1, :], st_s[1:2, :]), unroll=8)
    st_s[0:1, :] = hr
    st_s[1:2, :] = hi
    ys = []
    for j in range(n_blocks):
        h_blk = jnp.concatenate([hs_s[:, j * st:(j + 1) * st], hs_s[:, SSM_N + j * st:SSM_N + (j + 1) * st]], axis=1)
        ys.append(_dot(h_blk.astype(BF16), cc_ref[j]))
    y_ref[0] = jnp.concatenate(ys, axis=1) + d_ref[...] * u
    hr_ref[0] = hr
    hi_ref[0] = hi


def _ssm_prompt(u, bb, cc, d, ab_re, ab_im, tc=256):
    b, t, _ = u.shape
    st = jax.ShapeDtypeStruct((b, 1, SSM_N), F32)
    st_spec = pl.BlockSpec((1, 1, SSM_N), lambda i, j: (i, 0, 0))
    return pl.pallas_call(
        _ssm_prompt_body,
        grid=(b, t // tc),
        in_specs=[pl.BlockSpec((1, tc, SSM_WIDTH), lambda i, j: (i, j, 0)), _const_spec(bb.shape),
                  _const_spec(cc.shape), _const_spec(d.shape), _const_spec(ab_re.shape), _const_spec(ab_im.shape)],
        out_specs=[pl.BlockSpec((1, tc, SSM_WIDTH), lambda i, j: (i, j, 0)), st_spec, st_spec],
        out_shape=[jax.ShapeDtypeStruct((b, t, SSM_WIDTH), F32), st, st],
        scratch_shapes=[pltpu.VMEM((tc, 2 * SSM_N), F32), pltpu.VMEM((tc, 2 * SSM_N), F32),
                        pltpu.VMEM((8, SSM_N), F32)],
        compiler_params=_cparams("parallel", "arbitrary"),
        name="ssm_prompt",
    )(u, bb, cc, d, ab_re, ab_im)


FF_CHUNK = 256


def _mix_out(x, o_nsa, y_ssm, wglu_ref, bglu_ref, wout_ref):
    z = _dot(_gelu_tanh(y_ssm).astype(BF16), wglu_ref[...]) + bglu_ref[...]
    glu = z[:, :SSM_WIDTH] * _sigmoid(z[:, SSM_WIDTH:])
    return x + _dot(o_nsa, wout_ref[:NSA_WIDTH, :]) + _dot(glu.astype(BF16), wout_ref[NSA_WIDTH:, :])


def _ffn_chunks(hn, wup_ref, cw_ref, cb_ref, wdown_ref, prev_rows):
    acc = jnp.zeros((hn.shape[0], D_MODEL), F32)
    for j in range(D_FF // FF_CHUNK):
        conv = []
        for base in (0, D_FF):
            lo = base + j * FF_CHUNK
            hi = lo + FF_CHUNK
            hu = _dot(hn, wup_ref[:, lo:hi])
            hu2, hu1 = prev_rows(lo, hi, hu)
            conv.append(cw_ref[0:1, lo:hi] * hu2 + cw_ref[1:2, lo:hi] * hu1 + cw_ref[2:3, lo:hi] * hu
                        + cb_ref[:, lo:hi])
        a, g = conv
        act = (a * _sigmoid(a) * g).astype(BF16)
        acc = acc + _dot(act, wdown_ref[j * FF_CHUNK:(j + 1) * FF_CHUNK, :])
    return acc


def _tail_prompt_body(x_ref, o_ref, y_ref, wglu_ref, bglu_ref, wout_ref, nf_ref, wup_ref, cw_ref, cb_ref, wdown_ref,
                      nfin_ref, out_ref, cs_ref, prev_s):
    tm = x_ref.shape[1]

    @pl.when(pl.program_id(1) == 0)
    def _():
        prev_s[...] = jnp.zeros(prev_s.shape, F32)

    x1 = _mix_out(x_ref[0], o_ref[0], y_ref[0], wglu_ref, bglu_ref, wout_ref)
    hn = _rms(x1, nf_ref[...]).astype(BF16)
    row = lax.broadcasted_iota(jnp.int32, (8, 1), 0)

    def prev_rows(lo, hi, hu):
        p2, p1 = prev_s[6:7, lo:hi], prev_s[7:8, lo:hi]
        r1, r2 = pltpu.roll(hu, 1, 0), pltpu.roll(hu, 2, 0)
        hu1 = jnp.concatenate([jnp.where(row == 0, p1, r1[:8]), r1[8:]], axis=0)
        hu2 = jnp.concatenate([jnp.where(row == 0, p2, jnp.where(row == 1, p1, r2[:8])), r2[8:]], axis=0)
        prev_s[:, lo:hi] = hu[tm - 8:, :]
        cs_ref[0, :, lo:hi] = hu[tm - (CONV_W - 1):, :]
        return hu2, hu1

    x2 = x1 + _ffn_chunks(hn, wup_ref, cw_ref, cb_ref, wdown_ref, prev_rows)
    out_ref[0] = _rms(x2, nfin_ref[...])


def _tail_prompt(x, o_nsa, y_ssm, w, tm=512):
    b, t, _ = x.shape
    tile = lambda width: pl.BlockSpec((1, tm, width), lambda i, j: (i, j, 0))
    consts = [w["w_glu"], w["b_glu"], w["w_out"], w["norm_ffn"], w["w_up"], w["conv_w"], w["conv_b"], w["w_down"],
              w["norm_final"]]
    return pl.pallas_call(
        _tail_prompt_body,
        grid=(b, t // tm),
        in_specs=[tile(D_MODEL), tile(NSA_WIDTH), tile(SSM_WIDTH)] + [_const_spec(c.shape) for c in consts],
        out_specs=[tile(D_MODEL), pl.BlockSpec((1, CONV_W - 1, 2 * D_FF), lambda i, j: (i, 0, 0))],
        out_shape=[jax.ShapeDtypeStruct((b, t, D_MODEL), F32), jax.ShapeDtypeStruct((b, CONV_W - 1, 2 * D_FF), F32)],
        scratch_shapes=[pltpu.VMEM((8, 2 * D_FF), F32)],
        compiler_params=_cparams("parallel", "arbitrary"),
        name="tail_prompt",
    )(x, o_nsa, y_ssm, *consts)


def _pair_rows(q8, lane):
    row = lax.broadcasted_iota(jnp.int32, (8, 1), 0)
    out = jnp.zeros((8, LANES), F32)
    for r in range(N_HEADS):
        pair = q8[:, (r // 2) * LANES:(r // 2 + 1) * LANES]
        want_hi = r // GQA
        if r % 2 != want_hi:
            pair = pltpu.roll(pair, HEAD_DIM, 1)
        keep = (lane >= HEAD_DIM) if want_hi else (lane < HEAD_DIM)
        out = out + jnp.where(row == r, jnp.where(keep, pair, 0.0), 0.0)
    return out


def _unpair_rows(o, lane):
    pieces = []
    for j in range(N_HEADS // 2):
        lo, hi = o[2 * j:2 * j + 1], o[2 * j + 1:2 * j + 2]
        if (2 * j) // GQA == 0:
            hi = pltpu.roll(hi, HEAD_DIM, 1)
        else:
            lo = pltpu.roll(lo, HEAD_DIM, 1)
        pieces.append(jnp.where(lane < HEAD_DIM, lo, hi))
    return jnp.concatenate(pieces, axis=1)


PAIR_GROUP = 4


def _cmp_sample_body(pt_ref, pool_ref, new_ref, q_ref, bias_ref, w_ref, perm_ref, selmap_ref, oc_ref, idx_ref, buf, xs,
                     sem, *,
                     n_pages, n_seq):
    b = pl.program_id(0)
    n_pos = n_pages * PAGE_SIZE
    n_sub = n_pos // CMP_STRIDE
    half = N_KV_HEADS * HEAD_DIM
    sub_per_page = PAGE_SIZE // CMP_STRIDE

    def page_copy(seq, slot, p):
        dst = buf.at[slot, p // 2, :, :, :, pl.ds((p % 2) * PAGE_SIZE, PAGE_SIZE)]
        return pltpu.make_async_copy(pool_ref.at[pt_ref[seq * n_pages + p]], dst, sem.at[slot])

    def fetch(seq, slot):
        for p in range(n_pages):
            page_copy(seq, slot, p).start()

    @pl.when(b == 0)
    def _():
        fetch(0, 0)

    @pl.when(b + 1 < n_seq)
    def _():
        fetch(b + 1, (b + 1) % 2)

    slot = b % 2
    for p in range(n_pages):
        page_copy(b, slot, p).wait()

    row = lax.broadcasted_iota(jnp.int32, (n_sub, 1), 0)
    kv_c = []
    for x in range(2):
        for g0 in range(0, n_pages // 2, PAIR_GROUP):
            pairs = buf[slot, g0:g0 + PAIR_GROUP, x].reshape(PAIR_GROUP * half, 2 * PAGE_SIZE).astype(BF16)
            regrouped = _dot(pairs, perm_ref[...])
            for j in range(PAIR_GROUP):
                for k in range(2):
                    xs[x, 2 * (g0 + j) + k] = regrouped[j * half:(j + 1) * half, k * PAGE_SIZE:(k + 1) * PAGE_SIZE].T
        sub = jnp.concatenate([xs[x, :, s * sub_per_page:(s + 1) * sub_per_page, :].reshape(n_sub, half)
                               for s in range(CMP_STRIDE)], axis=1)
        parts = _dot(sub.astype(BF16), w_ref[x])
        new = jnp.broadcast_to(new_ref[0][:, x * half:(x + 1) * half], (8, half)).astype(BF16)
        new_part = _dot(new, w_ref[x, :half, :])[0:1, half:]
        nxt = jnp.where(row == n_sub - 1, new_part, pltpu.roll(parts[:, half:], n_sub - 1, 0))
        kv_c.append((parts[:, :half] + nxt + bias_ref[0:1, x * half:(x + 1) * half]).astype(BF16))

    q_pos = n_pos
    n_sel = selmap_ref.shape[1]
    n_real = q_pos // SEL_BLOCK + 1
    cmp_end = lax.broadcasted_iota(jnp.int32, (1, n_sub), 1) * CMP_STRIDE + (CMP_LEN - 1)
    blk = lax.broadcasted_iota(jnp.int32, (1, n_sel), 1)
    lane = lax.broadcasted_iota(jnp.int32, (1, LANES), 1)
    row8 = lax.broadcasted_iota(jnp.int32, (8, 1), 0)
    q2 = _pair_rows(jnp.broadcast_to(q_ref[0].astype(F32), (8, NSA_WIDTH)), lane).astype(BF16)
    p_c = _masked_softmax_rows(_nt_dot(q2, kv_c[0]), cmp_end <= q_pos)
    oc_ref[0] = _unpair_rows(_dot(p_c.astype(BF16), kv_c[1]), lane)
    p_sum = jnp.zeros((8, n_sub), F32)
    for h in range(N_KV_HEADS):
        in_h = (row8 >= h * GQA) & (row8 < (h + 1) * GQA)
        p_sum = p_sum + jnp.where(row8 == h, jnp.sum(jnp.where(in_h, p_c, 0.0), axis=0, keepdims=True), 0.0)
    score = _block_scores(_split_dot(p_sum, selmap_ref[...]), blk, q_pos)
    score = jnp.where((blk < n_real) & (row8 < N_KV_HEADS), score, BELOW_NEG)
    blk_f = blk.astype(F32)
    picked = jnp.full((8, LANES), -1.0, F32)
    for it in range(min(N_SELECT, n_real)):
        mx = jnp.max(score, axis=-1, keepdims=True)
        idx = jnp.min(jnp.where(score == mx, blk_f, float(n_sel)), axis=-1, keepdims=True)
        picked = jnp.where(lane == it, jnp.where(mx > 0.5 * NEG, idx, -1.0), picked)
        score = jnp.where(blk_f == idx, BELOW_NEG, score)
    idx_ref[0] = picked.astype(jnp.int32)


def _cmp_sample(page_table, pool_t, kvc_new, qb, bias, w_pos, selmap):
    n_seq, n_pages = page_table.shape
    half = N_KV_HEADS * HEAD_DIM
    pos = jnp.arange(PAGE_SIZE)
    dest = (pos % CMP_STRIDE) * (PAGE_SIZE // CMP_STRIDE) + pos // CMP_STRIDE
    perm = (dest[:, None] == jnp.arange(PAGE_SIZE)[None, :]).astype(BF16)
    zero = jnp.zeros_like(perm)
    perm = jnp.concatenate([jnp.concatenate([perm, zero], axis=1), jnp.concatenate([zero, perm], axis=1)], axis=0)
    grid_spec = pltpu.PrefetchScalarGridSpec(
        num_scalar_prefetch=1,
        grid=(n_seq,),
        in_specs=[pl.BlockSpec(memory_space=pl.ANY),
                  pl.BlockSpec((1, 1, KV_WIDTH), lambda i, pt: (i, 0, 0)),
                  pl.BlockSpec((1, 1, NSA_WIDTH), lambda i, pt: (i, 0, 0)),
                  _const_spec(bias.shape), _const_spec(w_pos.shape), _const_spec(perm.shape), _const_spec(selmap.shape)],
        out_specs=[pl.BlockSpec((1, 1, NSA_WIDTH), lambda i, pt: (i, 0, 0)),
                   pl.BlockSpec((1, 8, LANES), lambda i, pt: (i, 0, 0))],
        scratch_shapes=[pltpu.VMEM((2, n_pages // 2, 2, N_KV_HEADS, HEAD_DIM, 2 * PAGE_SIZE), F32),
                        pltpu.VMEM((2, n_pages, PAGE_SIZE, half), F32), pltpu.SemaphoreType.DMA((2,))],
    )
    return pl.pallas_call(
        functools.partial(_cmp_sample_body, n_pages=n_pages, n_seq=n_seq),
        grid_spec=grid_spec,
        out_shape=[jax.ShapeDtypeStruct((n_seq, 1, NSA_WIDTH), F32), jax.ShapeDtypeStruct((n_seq, 8, LANES), jnp.int32)],
        compiler_params=_cparams("arbitrary"),
        name="cmp_sample",
    )(page_table.reshape(-1), pool_t, kvc_new, qb, bias, w_pos, perm, selmap)


def _attn_sample_body(idx_ref, pg_ref, pool_ref, q_ref, gl_ref, oc_ref, ksn_ref, kwn_ref, win_ref, o_ref, wout_ref,
                      buf, sem, *, n_pages, n_seq, k_sel):
    b = pl.program_id(0)
    blk_per_page = PAGE_SIZE // SEL_BLOCK
    n_past = n_pages * blk_per_page
    n_blk = N_KV_HEADS * k_sel

    def fetch(seq, slot):
        for j in range(n_blk):
            page = pg_ref[seq * n_blk + j]
            pltpu.make_async_copy(pool_ref.at[page, :, j // k_sel], buf.at[slot, j], sem.at[slot]).start()

    @pl.when(b == 0)
    def _():
        fetch(0, 0)

    @pl.when(b + 1 < n_seq)
    def _():
        fetch(b + 1, (b + 1) % 2)

    slot = b % 2
    pltpu.make_async_copy(pool_ref.at[pl.ds(0, n_blk), :, 0], buf.at[slot], sem.at[slot]).wait()

    q = q_ref[0].astype(F32)
    gate = _sigmoid(gl_ref[0])
    o_c = oc_ref[0]
    ks_new = ksn_ref[0].astype(BF16).astype(F32)
    kw_new = kwn_ref[0].astype(BF16).astype(F32)
    n_win = win_ref.shape[-1]
    lane = lax.broadcasted_iota(jnp.int32, (1, PAGE_SIZE), 1)
    lane_blk = lax.shift_right_logical(lane, SEL_BLOCK.bit_length() - 1)
    head_out = []
    for h in range(N_KV_HEADS):
        k_lo, v_lo = h * HEAD_DIM, (N_KV_HEADS + h) * HEAD_DIM
        qs = _stack_heads_single(q, h)
        qf = qs.astype(F32)
        s_blocks, m_blocks = [], []
        has_new = jnp.zeros((1, 1), F32)
        for j in range(k_sel):
            i = idx_ref[b * n_blk + h * k_sel + j]
            ok = jnp.where((i >= 0) & (i < n_past), 1.0, 0.0)
            m_blocks.append(jnp.where(lane_blk == i % blk_per_page, ok, 0.0))
            has_new = jnp.maximum(has_new, jnp.where(i >= n_past, 1.0, 0.0))
            s_blocks.append(_dot(qs, buf[slot, h * k_sel + j, 0].astype(BF16)))
        past = jnp.concatenate(m_blocks, axis=1) > 0.5
        s = jnp.where(past, jnp.concatenate(s_blocks, axis=1), NEG)
        s_new = jnp.where(has_new > 0.5, jnp.sum(qf * ks_new[:, k_lo:k_lo + HEAD_DIM], axis=-1, keepdims=True), NEG)
        m = jnp.maximum(jnp.max(s, axis=-1, keepdims=True), s_new)
        p = jnp.where(past, jnp.exp(s - m), 0.0)
        p_new = jnp.where(has_new > 0.5, jnp.exp(s_new - m), 0.0)
        p_bf = p.astype(BF16)
        o_s = p_new.astype(BF16).astype(F32) * ks_new[:, v_lo:v_lo + HEAD_DIM]
        for j in range(k_sel):
            o_s = o_s + _nt_dot(p_bf[:, j * PAGE_SIZE:(j + 1) * PAGE_SIZE], buf[slot, h * k_sel + j, 1].astype(BF16))
        o_s = o_s * (1.0 / (jnp.sum(p, axis=-1, keepdims=True) + p_new))
        s = _dot(qs, win_ref[0, 0, h].astype(BF16))
        s_new = jnp.sum(qf * kw_new[:, k_lo:k_lo + HEAD_DIM], axis=-1, keepdims=True)
        m = jnp.maximum(jnp.max(s, axis=-1, keepdims=True), s_new)
        p, p_new = jnp.exp(s - m), jnp.exp(s_new - m)
        inv = 1.0 / (jnp.sum(p, axis=-1, keepdims=True) + p_new)
        o_w = (_nt_dot(p.astype(BF16), win_ref[0, 1, h].astype(BF16))
               + p_new.astype(BF16).astype(F32) * kw_new[:, v_lo:v_lo + HEAD_DIM]) * inv
        for g in range(GQA):
            hd = GQA * h + g
            col = hd * 3
            head_out.append(gate[:, col:col + 1] * o_c[:, hd * HEAD_DIM:(hd + 1) * HEAD_DIM]
                            + gate[:, col + 1:col + 2] * o_s[g:g + 1] + gate[:, col + 2:col + 3] * o_w[g:g + 1])
    o_ref[0] = jnp.concatenate(head_out, axis=-1).astype(BF16)
    d_row = lax.broadcasted_iota(jnp.int32, (HEAD_DIM, HEAD_DIM), 0)
    d_col = lax.broadcasted_iota(jnp.int32, (HEAD_DIM, HEAD_DIM), 1)
    pos = lax.broadcasted_iota(jnp.int32, (1, n_win), 1)
    new_rows = jnp.broadcast_to(kwn_ref[0], (HEAD_DIM, KV_WIDTH))
    for x in range(2):
        for h in range(N_KV_HEADS):
            lo = (x * N_KV_HEADS + h) * HEAD_DIM
            new_col = jnp.sum(jnp.where(d_row == d_col, new_rows[:, lo:lo + HEAD_DIM], 0.0), axis=-1, keepdims=True)
            wout_ref[0, x, h] = jnp.where(pos == n_win - 1, new_col, pltpu.roll(win_ref[0, x, h], n_win - 1, 1))


def _attn_sample(idx, page_table, pool_t, qb, gl, o_c, kvs_new, kvw_new, win_t, k_sel):
    n_seq, n_pages = page_table.shape
    n_win = win_t.shape[-1]
    n_blk = N_KV_HEADS * k_sel
    past_blk = jnp.clip(idx.reshape(n_seq, n_blk), 0, n_pages * (PAGE_SIZE // SEL_BLOCK) - 1)
    pages = jnp.take_along_axis(page_table, past_blk // (PAGE_SIZE // SEL_BLOCK), axis=1)
    one = lambda w: pl.BlockSpec((1, 1, w), lambda i, a, p: (i, 0, 0))
    win_spec = pl.BlockSpec((1, 2, N_KV_HEADS, HEAD_DIM, n_win), lambda i, a, p: (i, 0, 0, 0, 0))
    grid_spec = pltpu.PrefetchScalarGridSpec(
        num_scalar_prefetch=2,
        grid=(n_seq,),
        in_specs=[pl.BlockSpec(memory_space=pl.ANY), one(NSA_WIDTH), one(GATE_PAD), one(NSA_WIDTH), one(KV_WIDTH),
                  one(KV_WIDTH), win_spec],
        out_specs=[one(NSA_WIDTH), win_spec],
        scratch_shapes=[pltpu.VMEM((2, n_blk, 2, HEAD_DIM, PAGE_SIZE), F32), pltpu.SemaphoreType.DMA((2,))],
    )
    return pl.pallas_call(
        functools.partial(_attn_sample_body, n_pages=n_pages, n_seq=n_seq, k_sel=k_sel),
        grid_spec=grid_spec,
        out_shape=[jax.ShapeDtypeStruct((n_seq, 1, NSA_WIDTH), BF16), jax.ShapeDtypeStruct(win_t.shape, F32)],
        compiler_params=_cparams("arbitrary"),
        name="attn_sample",
    )(idx, pages.reshape(-1), pool_t, qb, gl, o_c, kvs_new, kvw_new, win_t)


def _tail_sample_body(x_ref, o_ref, u_ref, h0r_ref, h0i_ref, hist2_ref, hist1_ref, bb_ref, cc_ref, d_ref, abr_ref,
                      abi_ref, wglu_ref, bglu_ref, wout_ref, nf_ref, wup_ref, cw_ref, cb_ref, wdown_ref, nfin_ref,
                      out_ref, hr_ref, hi_ref, cs_ref):
    u = u_ref[...]
    bu = _split_dot(u, bb_ref[...])
    ar, ai = abr_ref[...], abi_ref[...]
    h0r, h0i = h0r_ref[...], h0i_ref[...]
    hr = ar * h0r - ai * h0i + bu[:, :SSM_N]
    hi = ar * h0i + ai * h0r + bu[:, SSM_N:]
    hr_ref[...] = hr
    hi_ref[...] = hi
    y = _dot(hr.astype(BF16), cc_ref[:SSM_N, :]) + _dot(hi.astype(BF16), cc_ref[SSM_N:, :]) + d_ref[...] * u
    x1 = _mix_out(x_ref[...], o_ref[...], y, wglu_ref, bglu_ref, wout_ref)
    hn = _rms(x1, nf_ref[...]).astype(BF16)

    def prev_rows(lo, hi_col, hu):
        cs_ref[:, lo:hi_col] = hist1_ref[:, lo:hi_col]
        cs_ref[:, 2 * D_FF + lo:2 * D_FF + hi_col] = hu
        return hist2_ref[:, lo:hi_col], hist1_ref[:, lo:hi_col]

    x2 = x1 + _ffn_chunks(hn, wup_ref, cw_ref, cb_ref, wdown_ref, prev_rows)
    out_ref[...] = _rms(x2, nfin_ref[...])


def _tail_sample(x, o_nsa, u, h0r, h0i, hist2, hist1, bb, cc, d, ab_re, ab_im, w):
    n = x.shape[0]
    sds = lambda width: jax.ShapeDtypeStruct((n, width), F32)
    return pl.pallas_call(
        _tail_sample_body,
        out_shape=[sds(D_MODEL), sds(SSM_N), sds(SSM_N), sds((CONV_W - 1) * 2 * D_FF)],
        compiler_params=pltpu.CompilerParams(vmem_limit_bytes=VMEM_LIMIT),
        name="tail_sample",
    )(x, o_nsa, u, h0r, h0i, hist2, hist1, bb, cc, d, ab_re, ab_im, w["w_glu"], w["b_glu"], w["w_out"],
      w["norm_ffn"], w["w_up"], w["conv_w"], w["conv_b"], w["w_down"], w["norm_final"])


def _pad_w_in(w_in):
    c = NSA_WIDTH + 3 * KV_WIDTH
    return jnp.concatenate([w_in[:, :c], w_in[:, c + N_GATES:], w_in[:, c:c + N_GATES],
                            jnp.zeros((D_MODEL, GATE_PAD - N_GATES), w_in.dtype)], axis=1).astype(BF16)


def _cmp_weight(w_cmp):
    w = w_cmp.reshape(2, CMP_LEN // CMP_STRIDE, CMP_STRIDE, HEAD_DIM, HEAD_DIM)
    eye_x = jnp.eye(2, dtype=w.dtype)
    eye_h = jnp.eye(N_KV_HEADS, dtype=w.dtype)
    big = (w.transpose(2, 0, 3, 1, 4)[:, :, None, :, :, None, None, :]
           * eye_x[None, :, None, None, None, :, None, None] * eye_h[None, None, :, None, None, None, :, None])
    return big.reshape(SUB_W, 2 * KV_WIDTH).astype(BF16)


def _cmp_weight_pos(w_cmp):
    w = w_cmp.reshape(2, CMP_LEN // CMP_STRIDE, CMP_STRIDE, HEAD_DIM, HEAD_DIM)
    eye_h = jnp.eye(N_KV_HEADS, dtype=w.dtype)
    big = (w.transpose(0, 2, 3, 1, 4)[:, :, None, :, :, None, :]
           * eye_h[None, None, :, None, None, :, None])
    half = N_KV_HEADS * HEAD_DIM
    return big.reshape(2, CMP_STRIDE * half, 2 * half).astype(BF16)


def _pe_sub(pe_cmp):
    rows = jnp.broadcast_to(pe_cmp.transpose(1, 0, 2)[:, :, None, :], (CMP_LEN, 2, N_KV_HEADS, HEAD_DIM))
    sub = rows.reshape(CMP_LEN // CMP_STRIDE, SUB_W)
    return jnp.concatenate([sub, jnp.zeros((8 - sub.shape[0], SUB_W), sub.dtype)], axis=0)


def _sel_map(n_cmp, n_sel, n_sel_pad):
    c0 = (jnp.arange(n_cmp) * CMP_STRIDE)[:, None]
    s0 = (jnp.arange(n_sel_pad) * SEL_BLOCK)[None, :]
    hit = (c0 < s0 + SEL_BLOCK) & (c0 + CMP_LEN > s0) & (jnp.arange(n_sel_pad)[None, :] < n_sel)
    return hit.astype(BF16)


def _expand_map(n_sel, t, kc_len):
    hit = jnp.arange(t)[None, :] // SEL_BLOCK == jnp.arange(n_sel)[:, None]
    return hit.astype(BF16).reshape(n_sel, t // kc_len, kc_len).transpose(1, 0, 2)


def kernel(x_prompt, x_sample, cache_kv_cmp, cache_kv_sel, cache_kv_win, state_ssm_re, state_ssm_im, state_ffn_conv, page_table, norm_mix, w_in, pe_cmp, w_cmp, ssm_a_re, ssm_a_im, ssm_log_dt, ssm_b_re, ssm_b_im, ssm_c_re, ssm_c_im, ssm_d, w_glu, b_glu, w_out, norm_ffn, w_up, conv_w, conv_b, w_down, norm_final):
    depth = w_in.shape[0]
    assert depth == 1, "single-layer trunk"
    b, t, _ = x_prompt.shape
    bd, s, _ = x_sample.shape
    assert s == 1, "one new position per sample sequence"
    n_pages = page_table.shape[1]
    l = 0
    w_pad = _pad_w_in(w_in[l])
    g_mix = norm_mix[l].reshape(1, D_MODEL)
    w_big = _cmp_weight(w_cmp[l])
    pe_sub = _pe_sub(pe_cmp[l])
    ab_re, ab_im, bb_re, bb_im = _ssm_params(ssm_log_dt[l], ssm_a_re[l], ssm_a_im[l], ssm_b_re[l], ssm_b_im[l])
    bb = jnp.concatenate([_block_diag_in(bb_re), _block_diag_in(bb_im)], axis=1).astype(BF16)
    cc = jnp.concatenate([_block_diag_out(ssm_c_re[l]), -_block_diag_out(ssm_c_im[l])], axis=0).astype(BF16)
    d_row = ssm_d[l].reshape(1, SSM_WIDTH)
    n_blk = SSM_WIDTH // LANES
    ch, st = LANES, SSM_N // n_blk
    bb_blk = jnp.stack([jnp.concatenate([bb[j * ch:(j + 1) * ch, j * st:(j + 1) * st],
                                         bb[j * ch:(j + 1) * ch, SSM_N + j * st:SSM_N + (j + 1) * st]], axis=1)
                        for j in range(n_blk)])
    cc_blk = jnp.stack([jnp.concatenate([cc[j * st:(j + 1) * st, j * ch:(j + 1) * ch],
                                         cc[SSM_N + j * st:SSM_N + (j + 1) * st, j * ch:(j + 1) * ch]], axis=0)
                        for j in range(n_blk)])
    tail_w = {"w_glu": w_glu[l].astype(BF16), "b_glu": b_glu[l].reshape(1, -1), "w_out": w_out[l].astype(BF16),
              "norm_ffn": norm_ffn[l].reshape(1, -1), "w_up": w_up[l].astype(BF16), "conv_w": conv_w[l],
              "conv_b": conv_b[l].reshape(1, -1), "w_down": w_down[l].astype(BF16),
              "norm_final": norm_final.reshape(1, -1)}

    kvc, _, _, kvct, kvst, kvwt, kstb, kwtb, gl, u, qb = _inproj(x_prompt, g_mix, w_pad, 512)
    n_sub = t // CMP_STRIDE
    kc, cmp_bias = _cmp_prompt(kvc.reshape(b, n_sub, SUB_W), pe_sub, w_big)
    n_sel = t // SEL_BLOCK
    o_nsa = _attn_prompt(qb, gl, kc, kstb, kwtb, _sel_map(n_sub, n_sel, n_sel).T)
    y_ssm, p_hr, p_hi = _ssm_prompt(u, bb_blk, cc_blk, d_row, ab_re, ab_im)
    y_prompt, p_conv = _tail_prompt(x_prompt, o_nsa, y_ssm, tail_w)
    win_keep = min(WINDOW, t)
    kv_rows = lambda a: a.reshape(a.shape[0], 2, N_KV_HEADS, HEAD_DIM, a.shape[2]).transpose(0, 4, 1, 2, 3)[None]
    st_shape = (depth, b, SSM_GROUPS, SSM_STATE)

    kvc_n, kvs_n, kvw_n, kvct_n, kvst_n, _, _, _, gl_n, u_n, qb_n = _inproj(x_sample.reshape(1, bd, D_MODEL), g_mix, w_pad, bd)
    per_seq = lambda a: a.reshape(bd, 1, a.shape[-1])
    n_sub_s = n_pages * (PAGE_SIZE // CMP_STRIDE)
    n_sel_s = n_pages * (PAGE_SIZE // SEL_BLOCK) + 1
    n_sel_pad = -(-n_sel_s // LANES) * LANES
    k_sel = min(N_SELECT, n_sel_s)
    pos_minor = lambda a: a.transpose(0, 2, 3, 4, 1)
    o_c, picked = _cmp_sample(page_table, pos_minor(cache_kv_cmp[l]), per_seq(kvc_n), per_seq(qb_n), cmp_bias,
                              _cmp_weight_pos(w_cmp[l]), _sel_map(n_sub_s, n_sel_s, n_sel_pad))
    idx = picked[:, :N_KV_HEADS, :k_sel].reshape(-1)
    n_buf = cache_kv_win.shape[2]
    assert n_buf == WINDOW, "window buffer holds exactly WINDOW rows"
    o_nsa_s, s_win = _attn_sample(idx, page_table, pos_minor(cache_kv_sel[l]), per_seq(qb_n), per_seq(gl_n), o_c,
                                  per_seq(kvs_n), per_seq(kvw_n), pos_minor(cache_kv_win[l]), k_sel)
    hist = state_ffn_conv[l]
    y_sample, s_hr, s_hi, s_conv = _tail_sample(
        x_sample.reshape(bd, D_MODEL), o_nsa_s.reshape(bd, -1), u_n.reshape(bd, -1),
        state_ssm_re[l].reshape(bd, SSM_N), state_ssm_im[l].reshape(bd, SSM_N), hist[:, 0], hist[:, 1], bb, cc, d_row,
        ab_re, ab_im, tail_w)
    kv_rows_s = lambda a: kv_rows(a).reshape(depth, bd, s, 2, N_KV_HEADS, HEAD_DIM)
    st_shape_s = (depth, bd, SSM_GROUPS, SSM_STATE)
    return (y_prompt, y_sample.reshape(bd, s, D_MODEL),
            kv_rows(kvct), kv_rows(kvst), kv_rows(kvwt[:, :, t - win_keep:]),
            p_hr.reshape(st_shape), p_hi.reshape(st_shape), p_conv.reshape(depth, b, CONV_W - 1, 2 * D_FF),
            kv_rows_s(kvct_n), kv_rows_s(kvst_n),
            s_win.transpose(0, 4, 1, 2, 3)[None],
            s_hr.reshape(st_shape_s), s_hi.reshape(st_shape_s), s_conv.reshape(depth, bd, CONV_W - 1, 2 * D_FF))
```

```python
import functools
import math

import jax
import jax.numpy as jnp
from jax import lax
from jax.experimental import pallas as pl
from jax.experimental.pallas import tpu as pltpu

D_MODEL = 1024
N_HEADS = 8
N_KV_HEADS = 2
GQA = N_HEADS // N_KV_HEADS
HEAD_DIM = 64
NSA_WIDTH = N_HEADS * HEAD_DIM
KV_WIDTH = 2 * N_KV_HEADS * HEAD_DIM
N_GATES = 3 * N_HEADS
CMP_LEN = 32
CMP_STRIDE = 16
SEL_BLOCK = 64
N_SELECT = 16
WINDOW = 512
Q_BLOCK = 128
PAGE_SIZE = 128
SSM_WIDTH = D_MODEL - NSA_WIDTH
SSM_GROUP = 16
SSM_GROUPS = SSM_WIDTH // SSM_GROUP
SSM_STATE = 64
SSM_N = SSM_GROUPS * SSM_STATE
D_FF = (D_MODEL * 11 // 4 + 127) // 128 * 128
CONV_W = 3
EPS = 1e-6
NEG = -1e30
BIG = 1e30
BELOW_NEG = -3e38

LANES = 128
GATE_PAD = LANES
IN_PAD = NSA_WIDTH + 3 * KV_WIDTH + SSM_WIDTH + GATE_PAD
SUB_W = CMP_STRIDE * KV_WIDTH
VMEM_LIMIT = 56 * 1024 * 1024

F32 = jnp.float32
BF16 = jnp.bfloat16


def _nt_dot(a, b):
    return lax.dot_general(a, b, (((1,), (1,)), ((), ())), preferred_element_type=F32)


def _dot(a, b):
    return jnp.dot(a, b, preferred_element_type=F32)


def _sigmoid(x):
    return 1.0 / (1.0 + jnp.exp(-x))


def _gelu_tanh(x):
    return 0.5 * x * (1.0 + jnp.tanh(math.sqrt(2.0 / math.pi) * (x + 0.044715 * (x * x * x))))


def _rms(x, g):
    return x * lax.rsqrt(jnp.mean(x * x, axis=-1, keepdims=True) + EPS) * g


def _split_dot(x, w_bf):
    hi = x.astype(BF16)
    lo = (x - hi.astype(F32)).astype(BF16)
    return _dot(hi, w_bf) + _dot(lo, w_bf)


def _cparams(*sem):
    return pltpu.CompilerParams(dimension_semantics=sem, vmem_limit_bytes=VMEM_LIMIT)


def _const_spec(shape):
    nd = len(shape)
    return pl.BlockSpec(shape, lambda *_: (0,) * nd, pipeline_mode=pl.Buffered(1))


def _inproj_body(x_ref, g_ref, w_ref, kvc_ref, kvs_ref, kvw_ref, kvct_ref, kvst_ref, kvwt_ref, kstb_ref, kwtb_ref,
                 gl_ref, u_ref, qb_ref):
    h = _rms(x_ref[0], g_ref[...])
    z = _dot(h.astype(BF16), w_ref[...])
    tm = z.shape[0]
    c = NSA_WIDTH
    qb_ref[0] = (z[:, :c] * (HEAD_DIM ** -0.5)).astype(BF16)
    for rm_ref, t_ref, tb_ref in ((kvc_ref, kvct_ref, None), (kvs_ref, kvst_ref, kstb_ref),
                                  (kvw_ref, kvwt_ref, kwtb_ref)):
        rows = z[:, c:c + KV_WIDTH]
        c += KV_WIDTH
        rm_ref[0] = rows
        cols = rows.T
        t_ref[0] = cols
        if tb_ref is not None:
            cols_bf = cols.astype(BF16)
            for k in range(tm // LANES):
                tb_ref[0, k] = cols_bf[:, k * LANES:(k + 1) * LANES]
    u_ref[0] = z[:, c:c + SSM_WIDTH]
    gl_ref[0] = z[:, c + SSM_WIDTH:]


def _inproj(x, g, w_pad, tm):
    b, t, _ = x.shape
    row = lambda w: pl.BlockSpec((1, tm, w), lambda i, j: (i, j, 0))
    col = pl.BlockSpec((1, KV_WIDTH, tm), lambda i, j: (i, 0, j))
    chunk = pl.BlockSpec((1, tm // LANES, KV_WIDTH, LANES), lambda i, j: (i, j, 0, 0))
    rm_t = jax.ShapeDtypeStruct((b, t, KV_WIDTH), F32)
    col_t = jax.ShapeDtypeStruct((b, KV_WIDTH, t), F32)
    chunk_t = jax.ShapeDtypeStruct((b, t // LANES, KV_WIDTH, LANES), BF16)
    return pl.pallas_call(
        _inproj_body,
        grid=(b, t // tm),
        in_specs=[row(D_MODEL), _const_spec((1, D_MODEL)), _const_spec((D_MODEL, IN_PAD))],
        out_specs=[row(KV_WIDTH)] * 3 + [col] * 3 + [chunk] * 2 + [row(GATE_PAD), row(SSM_WIDTH), row(NSA_WIDTH)],
        out_shape=[rm_t] * 3 + [col_t] * 3 + [chunk_t] * 2
        + [jax.ShapeDtypeStruct((b, t, GATE_PAD), F32), jax.ShapeDtypeStruct((b, t, SSM_WIDTH), F32),
           jax.ShapeDtypeStruct((b, t, NSA_WIDTH), BF16)],
        compiler_params=_cparams("parallel", "parallel"),
        name="in_proj",
    )(x, g, w_pad)


def _cmp_prompt_body(sub_ref, pe_ref, w_ref, kc_ref, bias_ref):
    w = w_ref[...]
    parts = _dot(sub_ref[0].astype(BF16), w)
    pe = _dot(pe_ref[...].astype(BF16), w)
    bias = pe[0:1, :KV_WIDTH] + pe[1:2, KV_WIDTH:]
    n_sub = parts.shape[0]
    nxt = pltpu.roll(parts[:, KV_WIDTH:], n_sub - 1, 0)
    kc_ref[0] = (parts[:, :KV_WIDTH] + nxt + bias).astype(BF16)
    bias_ref[...] = jnp.broadcast_to(bias, bias_ref.shape)


def _cmp_prompt(sub, pe_sub, w_big):
    b, n_sub, _ = sub.shape
    return pl.pallas_call(
        _cmp_prompt_body,
        grid=(b,),
        in_specs=[pl.BlockSpec((1, n_sub, SUB_W), lambda i: (i, 0, 0)), _const_spec(pe_sub.shape),
                  _const_spec(w_big.shape)],
        out_specs=[pl.BlockSpec((1, n_sub, KV_WIDTH), lambda i: (i, 0, 0)),
                   pl.BlockSpec((8, KV_WIDTH), lambda i: (0, 0))],
        out_shape=[jax.ShapeDtypeStruct((b, n_sub, KV_WIDTH), BF16), jax.ShapeDtypeStruct((8, KV_WIDTH), F32)],
        compiler_params=_cparams("arbitrary"),
        name="cmp_prompt",
    )(sub, pe_sub, w_big)


def _stack_heads(q, h):
    return jnp.concatenate([q[:, (GQA * h + g) * HEAD_DIM:(GQA * h + g + 1) * HEAD_DIM] for g in range(GQA)], axis=0)


def _stack_heads_single(q, h):
    row = lax.broadcasted_iota(jnp.int32, (8, 1), 0)
    q8 = jnp.broadcast_to(q, (8, NSA_WIDTH))
    out = jnp.zeros((8, HEAD_DIM), F32)
    for g in range(GQA):
        lo = (GQA * h + g) * HEAD_DIM
        out = out + jnp.where(row == g, q8[:, lo:lo + HEAD_DIM], 0.0)
    return out.astype(BF16)


def _masked_softmax_rows(s, mask):
    sm = jnp.where(mask, s, NEG)
    m = jnp.max(sm, axis=-1, keepdims=True)
    p = jnp.where(mask, jnp.exp(sm - m), 0.0)
    l = jnp.sum(p, axis=-1, keepdims=True)
    return p * (1.0 / jnp.where(l > 0.0, l, 1.0))


def _block_scores(imp, blk, qpos):
    first = blk * SEL_BLOCK
    own_or_imp = jnp.where(first + SEL_BLOCK > qpos, BIG, imp)
    return jnp.where(blk == 0, BIG, jnp.where(first <= qpos, own_or_imp, NEG))


def _masked_softmax_cols(s, mask):
    sm = jnp.where(mask, s, NEG)
    m = jnp.max(sm, axis=0, keepdims=True)
    p = jnp.where(mask, jnp.exp(sm - m), 0.0)
    l = jnp.sum(p, axis=0, keepdims=True)
    return p * (1.0 / jnp.where(l > 0.0, l, 1.0))


def _top_k_mask_cols(score, blk, k):
    n = float(score.shape[0])
    sel = jnp.zeros(score.shape, F32)
    for _ in range(k):
        mx = jnp.max(score, axis=0, keepdims=True)
        idx = jnp.min(jnp.where(score == mx, blk, n), axis=0, keepdims=True)
        hit = blk == idx
        sel = sel + jnp.where(hit, jnp.where(mx > 0.5 * NEG, 1.0, 0.0), 0.0)
        score = jnp.where(hit, BELOW_NEG, score)
    return sel


KEY_BLOCKS = 4
ROW_BLOCK = 32


def _attn_prompt_body(q_ref, gl_ref, kc_ref, kst_ref, kwt_ref, selmap_ref, exp_ref, o_ref, m_s, l_s, a_s, acc_s, s_s,
                      p_s):
    qb_idx = pl.program_id(1)
    start = qb_idx * Q_BLOCK
    kc_len = KEY_BLOCKS * LANES
    q = q_ref[0]
    gate = _sigmoid(gl_ref[0])
    qpos = start + lax.broadcasted_iota(jnp.int32, (Q_BLOCK, 1), 0)
    qpos_l = start + lax.broadcasted_iota(jnp.int32, (1, Q_BLOCK), 1)
    qpos4_l = jnp.concatenate([qpos_l] * GQA, axis=1)
    n_cmp = kc_ref.shape[1]
    n_sel = selmap_ref.shape[0]
    cmp_end = lax.broadcasted_iota(jnp.int32, (n_cmp, 1), 0) * CMP_STRIDE + (CMP_LEN - 1)
    blk = lax.broadcasted_iota(jnp.int32, (n_sel, 1), 0)
    qs, o_c, scores = [], [], []
    for h in range(N_KV_HEADS):
        k_lo, v_lo = h * HEAD_DIM, (N_KV_HEADS + h) * HEAD_DIM
        qs.append(_stack_heads(q, h))
        p_t = _masked_softmax_cols(_nt_dot(kc_ref[0, :, k_lo:k_lo + HEAD_DIM], qs[h]), cmp_end <= qpos4_l)
        o_c.append(_dot(p_t.T.astype(BF16), kc_ref[0, :, v_lo:v_lo + HEAD_DIM]))
        p_sum = p_t[:, 0:Q_BLOCK]
        for g in range(1, GQA):
            p_sum = p_sum + p_t[:, g * Q_BLOCK:(g + 1) * Q_BLOCK]
        hi = p_sum.astype(BF16)
        lo = (p_sum - hi.astype(F32)).astype(BF16)
        imp = _dot(selmap_ref[...], hi) + _dot(selmap_ref[...], lo)
        scores.append(_block_scores(imp, blk, qpos_l))
    sel_t = _top_k_mask_cols(jnp.concatenate(scores, axis=1), blk.astype(F32), min(N_SELECT, n_sel))
    q_ext = []
    for h in range(N_KV_HEADS):
        picked = sel_t[:, h * Q_BLOCK:(h + 1) * Q_BLOCK].T
        neg = jnp.where(picked > 0.5, 0.0, NEG).astype(BF16)
        q_ext.append(jnp.concatenate([qs[h], jnp.concatenate([neg] * GQA, axis=0)], axis=1))
    m_s[...] = jnp.full(m_s.shape, NEG, F32)
    l_s[...] = jnp.zeros(l_s.shape, F32)
    acc_s[...] = jnp.zeros(acc_s.shape, F32)
    n_chunks = (start + Q_BLOCK + kc_len - 1) // kc_len

    def sel_chunk(c, causal):
        for h in range(N_KV_HEADS):
            k_lo, v_lo = h * HEAD_DIM, (N_KV_HEADS + h) * HEAD_DIM
            k_t = jnp.concatenate([kst_ref[0, c * KEY_BLOCKS + j, k_lo:k_lo + HEAD_DIM, :]
                                   for j in range(KEY_BLOCKS)], axis=1)
            v_t = jnp.concatenate([kst_ref[0, c * KEY_BLOCKS + j, v_lo:v_lo + HEAD_DIM, :]
                                   for j in range(KEY_BLOCKS)], axis=1)
            s = _dot(q_ext[h], jnp.concatenate([k_t, exp_ref[c]], axis=0))
            s_s[h] = s if causal is None else s + causal
            for r0 in range(0, GQA * Q_BLOCK, ROW_BLOCK):
                rows = slice(r0, r0 + ROW_BLOCK)
                m_prev = m_s[h, rows, :]
                m_next = jnp.maximum(m_prev, jnp.max(s_s[h, rows, :], axis=-1, keepdims=True))
                a_s[h, rows, :] = jnp.exp(m_prev - m_next)
                m_s[h, rows, :] = m_next
            for r0 in range(0, GQA * Q_BLOCK, ROW_BLOCK):
                rows = slice(r0, r0 + ROW_BLOCK)
                p = jnp.exp(s_s[h, rows, :] - jnp.tile(m_s[h, rows, :], (1, KEY_BLOCKS)))
                l_s[h, rows, :] = a_s[h, rows, :] * l_s[h, rows, :] + jnp.sum(p, axis=-1, keepdims=True)
                p_s[h, rows, :] = p.astype(BF16)
            acc_s[h] = acc_s[h] * a_s[h, :, :HEAD_DIM] + _nt_dot(p_s[h], v_t)

    kpos = (n_chunks - 1) * kc_len + lax.broadcasted_iota(jnp.int32, (1, kc_len), 1)
    sel_chunk(n_chunks - 1, jnp.concatenate([jnp.where(kpos <= qpos, 0.0, NEG)] * GQA, axis=0))

    def earlier_chunk(it, carry):
        sel_chunk(n_chunks - 1 - it, None)
        return carry

    lax.fori_loop(1, n_chunks, earlier_chunk, 0)
    wpos = start - WINDOW + lax.broadcasted_iota(jnp.int32, (1, WINDOW + Q_BLOCK), 1)
    visible = jnp.where(wpos <= qpos, jnp.where(wpos >= jnp.maximum(qpos - WINDOW, 0), 0.0, NEG), NEG)
    win_bias = jnp.concatenate([visible] * GQA, axis=0)
    head_out = []
    for h in range(N_KV_HEADS):
        k_lo, v_lo = h * HEAD_DIM, (N_KV_HEADS + h) * HEAD_DIM
        o_s = acc_s[h] * (1.0 / l_s[h])[:, :HEAD_DIM]
        kw, vw = [], []
        for c in range((WINDOW + Q_BLOCK) // LANES):
            src = jnp.maximum(qb_idx - WINDOW // LANES + c, 0)
            kw.append(kwt_ref[0, src, k_lo:k_lo + HEAD_DIM, :])
            vw.append(kwt_ref[0, src, v_lo:v_lo + HEAD_DIM, :])
        s = _dot(qs[h], jnp.concatenate(kw, axis=1))
        sm = s + win_bias
        p = jnp.exp(sm - jnp.max(sm, axis=-1, keepdims=True))
        o_w = _nt_dot(p.astype(BF16), jnp.concatenate(vw, axis=1)) * (1.0 / jnp.sum(p, axis=-1, keepdims=True))
        for g in range(GQA):
            col = (GQA * h + g) * 3
            r = slice(g * Q_BLOCK, (g + 1) * Q_BLOCK)
            head_out.append(gate[:, col:col + 1] * o_c[h][r] + gate[:, col + 1:col + 2] * o_s[r]
                            + gate[:, col + 2:col + 3] * o_w[r])
    o_ref[0] = jnp.concatenate(head_out, axis=-1).astype(BF16)


def _attn_prompt(qb, gl, kc, kst, kwt, selmap_t):
    b, t, _ = qb.shape
    kc_len = KEY_BLOCKS * LANES
    expand = _expand_map(selmap_t.shape[0], t, kc_len)
    rows = GQA * Q_BLOCK
    whole = lambda a: pl.BlockSpec((1,) + a.shape[1:], lambda i, j: (i,) + (0,) * (a.ndim - 1))
    return pl.pallas_call(
        _attn_prompt_body,
        grid=(b, t // Q_BLOCK),
        in_specs=[pl.BlockSpec((1, Q_BLOCK, NSA_WIDTH), lambda i, j: (i, j, 0)),
                  pl.BlockSpec((1, Q_BLOCK, GATE_PAD), lambda i, j: (i, j, 0)),
                  whole(kc), whole(kst), whole(kwt), _const_spec(selmap_t.shape), _const_spec(expand.shape)],
        out_specs=pl.BlockSpec((1, Q_BLOCK, NSA_WIDTH), lambda i, j: (i, j, 0)),
        out_shape=jax.ShapeDtypeStruct((b, t, NSA_WIDTH), BF16),
        scratch_shapes=[pltpu.VMEM((N_KV_HEADS, rows, LANES), F32), pltpu.VMEM((N_KV_HEADS, rows, LANES), F32),
                        pltpu.VMEM((N_KV_HEADS, rows, LANES), F32), pltpu.VMEM((N_KV_HEADS, rows, HEAD_DIM), F32),
                        pltpu.VMEM((N_KV_HEADS, rows, kc_len), F32), pltpu.VMEM((N_KV_HEADS, rows, kc_len), BF16)],
        compiler_params=_cparams("parallel", "arbitrary"),
        name="attn_prompt",
    )(qb, gl, kc, kst, kwt, selmap_t, expand)


def _ssm_param_body(ldt_ref, are_ref, aim_ref, bre_ref, bim_ref, abr_ref, abi_ref, bbr_ref, bbi_ref):
    dt = jnp.exp(ldt_ref[...])
    are, aim = are_ref[...], aim_ref[...]
    mag = jnp.exp(dt * are)
    ab_re, ab_im = mag * jnp.cos(dt * aim), mag * jnp.sin(dt * aim)
    den = are * are + aim * aim
    zr, zi = ab_re - 1.0, ab_im
    f_re = (zr * are + zi * aim) / den
    f_im = (zi * are - zr * aim) / den
    abr_ref[...] = ab_re
    abi_ref[...] = ab_im
    bbr_ref[...] = f_re * bre_ref[...] - f_im * bim_ref[...]
    bbi_ref[...] = f_re * bim_ref[...] + f_im * bre_ref[...]


def _ssm_params(log_dt, a_re, a_im, b_re, b_im):
    col = lambda a: a.reshape(SSM_N, 1)
    ldt = col(jnp.broadcast_to(log_dt[:, None], (SSM_GROUPS, SSM_STATE)))
    col_t = jax.ShapeDtypeStruct((SSM_N, 1), F32)
    mat_t = jax.ShapeDtypeStruct((SSM_N, SSM_GROUP), F32)
    ab_re, ab_im, bb_re, bb_im = pl.pallas_call(
        _ssm_param_body, out_shape=[col_t, col_t, mat_t, mat_t], name="ssm_params",
    )(ldt, col(a_re), col(a_im), b_re.reshape(SSM_N, SSM_GROUP), b_im.reshape(SSM_N, SSM_GROUP))
    return ab_re.reshape(1, SSM_N), ab_im.reshape(1, SSM_N), bb_re, bb_im


def _block_diag_in(bb):
    m = bb.reshape(SSM_GROUPS, SSM_STATE, SSM_GROUP).transpose(0, 2, 1)
    eye = jnp.eye(SSM_GROUPS, dtype=bb.dtype)
    return (eye[:, None, :, None] * m[:, :, None, :]).reshape(SSM_WIDTH, SSM_N)


def _block_diag_out(c):
    m = c.transpose(0, 2, 1)
    eye = jnp.eye(SSM_GROUPS, dtype=c.dtype)
    return (eye[:, None, :, None] * m[:, :, None, :]).reshape(SSM_N, SSM_WIDTH)


def _ssm_prompt_body(u_ref, bb_ref, cc_ref, d_ref, abr_ref, abi_ref, y_ref, hr_ref, hi_ref, bu_s, hs_s, st_s):
    tc = u_ref.shape[1]

    @pl.when(pl.program_id(1) == 0)
    def _():
        st_s[...] = jnp.zeros(st_s.shape, F32)

    u = u_ref[0]
    u_bf = u.astype(BF16)
    n_blocks = bb_ref.shape[0]
    ch, st = SSM_WIDTH // n_blocks, SSM_N // n_blocks
    for j in range(n_blocks):
        z = _dot(u_bf[:, j * ch:(j + 1) * ch], bb_ref[j])
        bu_s[:, j * st:(j + 1) * st] = z[:, :st]
        bu_s[:, SSM_N + j * st:SSM_N + (j + 1) * st] = z[:, st:]
    ar, ai = abr_ref[...], abi_ref[...]

    def step(t, carry):
        hr, hi = carry
        nr = ar * hr - ai * hi + bu_s[pl.ds(t, 1), :SSM_N]
        ni = ar * hi + ai * hr + bu_s[pl.ds(t, 1), SSM_N:]
        hs_s[pl.ds(t, 1), :SSM_N] = nr
        hs_s[pl.ds(t, 1), SSM_N:] = ni
        return nr, ni

    hr, hi = lax.fori_loop(0, tc, step, (st_s[0:1, :], st_s[1:2, :]), unroll=8)
    st_s[0:1, :] = hr
    st_s[1:2, :] = hi
    ys = []
    for j in range(n_blocks):
        h_blk = jnp.concatenate([hs_s[:, j * st:(j + 1) * st], hs_s[:, SSM_N + j * st:SSM_N + (j + 1) * st]], axis=1)
        ys.append(_dot(h_blk.astype(BF16), cc_ref[j]))
    y_ref[0] = jnp.concatenate(ys, axis=1) + d_ref[...] * u
    hr_ref[0] = hr
    hi_ref[0] = hi


def _ssm_prompt(u, bb, cc, d, ab_re, ab_im, tc=256):
    b, t, _ = u.shape
    st = jax.ShapeDtypeStruct((b, 1, SSM_N), F32)
    st_spec = pl.BlockSpec((1, 1, SSM_N), lambda i, j: (i, 0, 0))
    return pl.pallas_call(
        _ssm_prompt_body,
        grid=(b, t // tc),
        in_specs=[pl.BlockSpec((1, tc, SSM_WIDTH), lambda i, j: (i, j, 0)), _const_spec(bb.shape),
                  _const_spec(cc.shape), _const_spec(d.shape), _const_spec(ab_re.shape), _const_spec(ab_im.shape)],
        out_specs=[pl.BlockSpec((1, tc, SSM_WIDTH), lambda i, j: (i, j, 0)), st_spec, st_spec],
        out_shape=[jax.ShapeDtypeStruct((b, t, SSM_WIDTH), F32), st, st],
        scratch_shapes=[pltpu.VMEM((tc, 2 * SSM_N), F32), pltpu.VMEM((tc, 2 * SSM_N), F32),
                        pltpu.VMEM((8, SSM_N), F32)],
        compiler_params=_cparams("parallel", "arbitrary"),
        name="ssm_prompt",
    )(u, bb, cc, d, ab_re, ab_im)


FF_CHUNK = 256


def _mix_out(x, o_nsa, y_ssm, wglu_ref, bglu_ref, wout_ref):
    z = _dot(_gelu_tanh(y_ssm).astype(BF16), wglu_ref[...]) + bglu_ref[...]
    glu = z[:, :SSM_WIDTH] * _sigmoid(z[:, SSM_WIDTH:])
    return x + _dot(o_nsa, wout_ref[:NSA_WIDTH, :]) + _dot(glu.astype(BF16), wout_ref[NSA_WIDTH:, :])


def _ffn_chunks(hn, wup_ref, cw_ref, cb_ref, wdown_ref, prev_rows):
    acts = []
    for j in range(D_FF // FF_CHUNK):
        conv = []
        for base in (0, D_FF):
            lo = base + j * FF_CHUNK
            hi = lo + FF_CHUNK
            hu = _dot(hn, wup_ref[:, lo:hi])
            hu2, hu1 = prev_rows(lo, hi, hu)
            conv.append(cw_ref[0:1, lo:hi] * hu2 + cw_ref[1:2, lo:hi] * hu1 + cw_ref[2:3, lo:hi] * hu
                        + cb_ref[:, lo:hi])
        a, g = conv
        acts.append((a * _sigmoid(a) * g).astype(BF16))
    return _dot(jnp.concatenate(acts, axis=1), wdown_ref[...])


def _tail_prompt_body(x_ref, o_ref, y_ref, wglu_ref, bglu_ref, wout_ref, nf_ref, wup_ref, cw_ref, cb_ref, wdown_ref,
                      nfin_ref, out_ref, cs_ref, prev_s):
    tm = x_ref.shape[1]

    @pl.when(pl.program_id(1) == 0)
    def _():
        prev_s[...] = jnp.zeros(prev_s.shape, F32)

    x1 = _mix_out(x_ref[0], o_ref[0], y_ref[0], wglu_ref, bglu_ref, wout_ref)
    hn = _rms(x1, nf_ref[...]).astype(BF16)
    row = lax.broadcasted_iota(jnp.int32, (8, 1), 0)

    def prev_rows(lo, hi, hu):
        p2, p1 = prev_s[6:7, lo:hi], prev_s[7:8, lo:hi]
        r1, r2 = pltpu.roll(hu, 1, 0), pltpu.roll(hu, 2, 0)
        hu1 = jnp.concatenate([jnp.where(row == 0, p1, r1[:8]), r1[8:]], axis=0)
        hu2 = jnp.concatenate([jnp.where(row == 0, p2, jnp.where(row == 1, p1, r2[:8])), r2[8:]], axis=0)
        prev_s[:, lo:hi] = hu[tm - 8:, :]
        cs_ref[0, :, lo:hi] = hu[tm - (CONV_W - 1):, :]
        return hu2, hu1

    x2 = x1 + _ffn_chunks(hn, wup_ref, cw_ref, cb_ref, wdown_ref, prev_rows)
    out_ref[0] = _rms(x2, nfin_ref[...])


def _tail_prompt(x, o_nsa, y_ssm, w, tm=512):
    b, t, _ = x.shape
    tile = lambda width: pl.BlockSpec((1, tm, width), lambda i, j: (i, j, 0))
    consts = [w["w_glu"], w["b_glu"], w["w_out"], w["norm_ffn"], w["w_up"], w["conv_w"], w["conv_b"], w["w_down"],
              w["norm_final"]]
    return pl.pallas_call(
        _tail_prompt_body,
        grid=(b, t // tm),
        in_specs=[tile(D_MODEL), tile(NSA_WIDTH), tile(SSM_WIDTH)] + [_const_spec(c.shape) for c in consts],
        out_specs=[tile(D_MODEL), pl.BlockSpec((1, CONV_W - 1, 2 * D_FF), lambda i, j: (i, 0, 0))],
        out_shape=[jax.ShapeDtypeStruct((b, t, D_MODEL), F32), jax.ShapeDtypeStruct((b, CONV_W - 1, 2 * D_FF), F32)],
        scratch_shapes=[pltpu.VMEM((8, 2 * D_FF), F32)],
        compiler_params=_cparams("parallel", "arbitrary"),
        name="tail_prompt",
    )(x, o_nsa, y_ssm, *consts)


def _pair_rows(q8, lane):
    row = lax.broadcasted_iota(jnp.int32, (8, 1), 0)
    out = jnp.zeros((8, LANES), F32)
    for r in range(N_HEADS):
        pair = q8[:, (r // 2) * LANES:(r // 2 + 1) * LANES]
        want_hi = r // GQA
        if r % 2 != want_hi:
            pair = pltpu.roll(pair, HEAD_DIM, 1)
        keep = (lane >= HEAD_DIM) if want_hi else (lane < HEAD_DIM)
        out = out + jnp.where(row == r, jnp.where(keep, pair, 0.0), 0.0)
    return out


def _unpair_rows(o, lane):
    pieces = []
    for j in range(N_HEADS // 2):
        lo, hi = o[2 * j:2 * j + 1], o[2 * j + 1:2 * j + 2]
        if (2 * j) // GQA == 0:
            hi = pltpu.roll(hi, HEAD_DIM, 1)
        else:
            lo = pltpu.roll(lo, HEAD_DIM, 1)
        pieces.append(jnp.where(lane < HEAD_DIM, lo, hi))
    return jnp.concatenate(pieces, axis=1)


PAIR_GROUP = 4


def _cmp_sample_body(pt_ref, pool_ref, new_ref, q_ref, bias_ref, w_ref, perm_ref, selmap_ref, oc_ref, idx_ref, buf, xs,
                     sem, *,
                     n_pages, n_seq):
    b = pl.program_id(0)
    n_pos = n_pages * PAGE_SIZE
    n_sub = n_pos // CMP_STRIDE
    half = N_KV_HEADS * HEAD_DIM
    sub_per_page = PAGE_SIZE // CMP_STRIDE

    def page_copy(seq, slot, p):
        dst = buf.at[slot, p // 2, :, :, :, pl.ds((p % 2) * PAGE_SIZE, PAGE_SIZE)]
        return pltpu.make_async_copy(pool_ref.at[pt_ref[seq * n_pages + p]], dst, sem.at[slot])

    def fetch(seq, slot):
        for p in range(n_pages):
            page_copy(seq, slot, p).start()

    @pl.when(b == 0)
    def _():
        fetch(0, 0)

    @pl.when(b + 1 < n_seq)
    def _():
        fetch(b + 1, (b + 1) % 2)

    slot = b % 2
    for p in range(n_pages):
        page_copy(b, slot, p).wait()

    row = lax.broadcasted_iota(jnp.int32, (n_sub, 1), 0)
    kv_c = []
    for x in range(2):
        for g0 in range(0, n_pages // 2, PAIR_GROUP):
            pairs = buf[slot, g0:g0 + PAIR_GROUP, x].reshape(PAIR_GROUP * half, 2 * PAGE_SIZE).astype(BF16)
            regrouped = _dot(pairs, perm_ref[...])
            for j in range(PAIR_GROUP):
                for k in range(2):
                    xs[x, 2 * (g0 + j) + k] = regrouped[j * half:(j + 1) * half, k * PAGE_SIZE:(k + 1) * PAGE_SIZE].T
        sub = jnp.concatenate([xs[x, :, s * sub_per_page:(s + 1) * sub_per_page, :].reshape(n_sub, half)
                               for s in range(CMP_STRIDE)], axis=1)
        parts = _dot(sub.astype(BF16), w_ref[x])
        new = jnp.broadcast_to(new_ref[0][:, x * half:(x + 1) * half], (8, half)).astype(BF16)
        new_part = _dot(new, w_ref[x, :half, :])[0:1, half:]
        nxt = jnp.where(row == n_sub - 1, new_part, pltpu.roll(parts[:, half:], n_sub - 1, 0))
        kv_c.append((parts[:, :half] + nxt + bias_ref[0:1, x * half:(x + 1) * half]).astype(BF16))

    q_pos = n_pos
    n_sel = selmap_ref.shape[1]
    n_real = q_pos // SEL_BLOCK + 1
    cmp_end = lax.broadcasted_iota(jnp.int32, (1, n_sub), 1) * CMP_STRIDE + (CMP_LEN - 1)
    blk = lax.broadcasted_iota(jnp.int32, (1, n_sel), 1)
    lane = lax.broadcasted_iota(jnp.int32, (1, LANES), 1)
    row8 = lax.broadcasted_iota(jnp.int32, (8, 1), 0)
    q2 = _pair_rows(jnp.broadcast_to(q_ref[0].astype(F32), (8, NSA_WIDTH)), lane).astype(BF16)
    p_c = _masked_softmax_rows(_nt_dot(q2, kv_c[0]), cmp_end <= q_pos)
    oc_ref[0] = _unpair_rows(_dot(p_c.astype(BF16), kv_c[1]), lane)
    p_sum = jnp.zeros((8, n_sub), F32)
    for h in range(N_KV_HEADS):
        in_h = (row8 >= h * GQA) & (row8 < (h + 1) * GQA)
        p_sum = p_sum + jnp.where(row8 == h, jnp.sum(jnp.where(in_h, p_c, 0.0), axis=0, keepdims=True), 0.0)
    score = _block_scores(_split_dot(p_sum, selmap_ref[...]), blk, q_pos)
    score = jnp.where((blk < n_real) & (row8 < N_KV_HEADS), score, BELOW_NEG)
    blk_f = blk.astype(F32)
    picked = jnp.full((8, LANES), -1.0, F32)
    for it in range(min(N_SELECT, n_real)):
        mx = jnp.max(score, axis=-1, keepdims=True)
        idx = jnp.min(jnp.where(score == mx, blk_f, float(n_sel)), axis=-1, keepdims=True)
        picked = jnp.where(lane == it, jnp.where(mx > 0.5 * NEG, idx, -1.0), picked)
        score = jnp.where(blk_f == idx, BELOW_NEG, score)
    idx_ref[0] = picked.astype(jnp.int32)


def _cmp_sample(page_table, pool_t, kvc_new, qb, bias, w_pos, selmap):
    n_seq, n_pages = page_table.shape
    half = N_KV_HEADS * HEAD_DIM
    pos = jnp.arange(PAGE_SIZE)
    dest = (pos % CMP_STRIDE) * (PAGE_SIZE // CMP_STRIDE) + pos // CMP_STRIDE
    perm = (dest[:, None] == jnp.arange(PAGE_SIZE)[None, :]).astype(BF16)
    zero = jnp.zeros_like(perm)
    perm = jnp.concatenate([jnp.concatenate([perm, zero], axis=1), jnp.concatenate([zero, perm], axis=1)], axis=0)
    grid_spec = pltpu.PrefetchScalarGridSpec(
        num_scalar_prefetch=1,
        grid=(n_seq,),
        in_specs=[pl.BlockSpec(memory_space=pl.ANY),
                  pl.BlockSpec((1, 1, KV_WIDTH), lambda i, pt: (i, 0, 0)),
                  pl.BlockSpec((1, 1, NSA_WIDTH), lambda i, pt: (i, 0, 0)),
                  _const_spec(bias.shape), _const_spec(w_pos.shape), _const_spec(perm.shape), _const_spec(selmap.shape)],
        out_specs=[pl.BlockSpec((1, 1, NSA_WIDTH), lambda i, pt: (i, 0, 0)),
                   pl.BlockSpec((1, 8, LANES), lambda i, pt: (i, 0, 0))],
        scratch_shapes=[pltpu.VMEM((2, n_pages // 2, 2, N_KV_HEADS, HEAD_DIM, 2 * PAGE_SIZE), F32),
                        pltpu.VMEM((2, n_pages, PAGE_SIZE, half), F32), pltpu.SemaphoreType.DMA((2,))],
    )
    return pl.pallas_call(
        functools.partial(_cmp_sample_body, n_pages=n_pages, n_seq=n_seq),
        grid_spec=grid_spec,
        out_shape=[jax.ShapeDtypeStruct((n_seq, 1, NSA_WIDTH), F32), jax.ShapeDtypeStruct((n_seq, 8, LANES), jnp.int32)],
        compiler_params=_cparams("arbitrary"),
        name="cmp_sample",
    )(page_table.reshape(-1), pool_t, kvc_new, qb, bias, w_pos, perm, selmap)


def _attn_sample_body(idx_ref, pg_ref, pool_ref, q_ref, gl_ref, oc_ref, ksn_ref, kwn_ref, win_ref, o_ref, wout_ref,
                      buf, sem, *, n_pages, n_seq, k_sel):
    b = pl.program_id(0)
    blk_per_page = PAGE_SIZE // SEL_BLOCK
    n_past = n_pages * blk_per_page
    n_blk = N_KV_HEADS * k_sel

    def fetch(seq, slot):
        for j in range(n_blk):
            page = pg_ref[seq * n_blk + j]
            pltpu.make_async_copy(pool_ref.at[page, :, j // k_sel], buf.at[slot, j], sem.at[slot]).start()

    @pl.when(b == 0)
    def _():
        fetch(0, 0)

    @pl.when(b + 1 < n_seq)
    def _():
        fetch(b + 1, (b + 1) % 2)

    slot = b % 2
    pltpu.make_async_copy(pool_ref.at[pl.ds(0, n_blk), :, 0], buf.at[slot], sem.at[slot]).wait()

    q = q_ref[0].astype(F32)
    gate = _sigmoid(gl_ref[0])
    o_c = oc_ref[0]
    ks_new = ksn_ref[0].astype(BF16).astype(F32)
    kw_new = kwn_ref[0].astype(BF16).astype(F32)
    n_win = win_ref.shape[-1]
    lane = lax.broadcasted_iota(jnp.int32, (1, PAGE_SIZE), 1)
    lane_blk = lax.shift_right_logical(lane, SEL_BLOCK.bit_length() - 1)
    head_out = []
    for h in range(N_KV_HEADS):
        k_lo, v_lo = h * HEAD_DIM, (N_KV_HEADS + h) * HEAD_DIM
        qs = _stack_heads_single(q, h)
        qf = qs.astype(F32)
        s_blocks, m_blocks = [], []
        has_new = jnp.zeros((1, 1), F32)
        for j in range(k_sel):
            i = idx_ref[b * n_blk + h * k_sel + j]
            ok = jnp.where((i >= 0) & (i < n_past), 1.0, 0.0)
            m_blocks.append(jnp.where(lane_blk == i % blk_per_page, ok, 0.0))
            has_new = jnp.maximum(has_new, jnp.where(i >= n_past, 1.0, 0.0))
            s_blocks.append(_dot(qs, buf[slot, h * k_sel + j, 0].astype(BF16)))
        past = jnp.concatenate(m_blocks, axis=1) > 0.5
        s = jnp.where(past, jnp.concatenate(s_blocks, axis=1), NEG)
        s_new = jnp.where(has_new > 0.5, jnp.sum(qf * ks_new[:, k_lo:k_lo + HEAD_DIM], axis=-1, keepdims=True), NEG)
        m = jnp.maximum(jnp.max(s, axis=-1, keepdims=True), s_new)
        p = jnp.where(past, jnp.exp(s - m), 0.0)
        p_new = jnp.where(has_new > 0.5, jnp.exp(s_new - m), 0.0)
        p_bf = p.astype(BF16)
        o_s = p_new.astype(BF16).astype(F32) * ks_new[:, v_lo:v_lo + HEAD_DIM]
        for j in range(k_sel):
            o_s = o_s + _nt_dot(p_bf[:, j * PAGE_SIZE:(j + 1) * PAGE_SIZE], buf[slot, h * k_sel + j, 1].astype(BF16))
        o_s = o_s * (1.0 / (jnp.sum(p, axis=-1, keepdims=True) + p_new))
        s = _dot(qs, win_ref[0, 0, h].astype(BF16))
        s_new = jnp.sum(qf * kw_new[:, k_lo:k_lo + HEAD_DIM], axis=-1, keepdims=True)
        m = jnp.maximum(jnp.max(s, axis=-1, keepdims=True), s_new)
        p, p_new = jnp.exp(s - m), jnp.exp(s_new - m)
        inv = 1.0 / (jnp.sum(p, axis=-1, keepdims=True) + p_new)
        o_w = (_nt_dot(p.astype(BF16), win_ref[0, 1, h].astype(BF16))
               + p_new.astype(BF16).astype(F32) * kw_new[:, v_lo:v_lo + HEAD_DIM]) * inv
        for g in range(GQA):
            hd = GQA * h + g
            col = hd * 3
            head_out.append(gate[:, col:col + 1] * o_c[:, hd * HEAD_DIM:(hd + 1) * HEAD_DIM]
                            + gate[:, col + 1:col + 2] * o_s[g:g + 1] + gate[:, col + 2:col + 3] * o_w[g:g + 1])
    o_ref[0] = jnp.concatenate(head_out, axis=-1).astype(BF16)
    d_row = lax.broadcasted_iota(jnp.int32, (HEAD_DIM, HEAD_DIM), 0)
    d_col = lax.broadcasted_iota(jnp.int32, (HEAD_DIM, HEAD_DIM), 1)
    pos = lax.broadcasted_iota(jnp.int32, (1, n_win), 1)
    new_rows = jnp.broadcast_to(kwn_ref[0], (HEAD_DIM, KV_WIDTH))
    for x in range(2):
        for h in range(N_KV_HEADS):
            lo = (x * N_KV_HEADS + h) * HEAD_DIM
            new_col = jnp.sum(jnp.where(d_row == d_col, new_rows[:, lo:lo + HEAD_DIM], 0.0), axis=-1, keepdims=True)
            wout_ref[0, x, h] = jnp.where(pos == n_win - 1, new_col, pltpu.roll(win_ref[0, x, h], n_win - 1, 1))


def _attn_sample(idx, page_table, pool_t, qb, gl, o_c, kvs_new, kvw_new, win_t, k_sel):
    n_seq, n_pages = page_table.shape
    n_win = win_t.shape[-1]
    n_blk = N_KV_HEADS * k_sel
    past_blk = jnp.clip(idx.reshape(n_seq, n_blk), 0, n_pages * (PAGE_SIZE // SEL_BLOCK) - 1)
    pages = jnp.take_along_axis(page_table, past_blk // (PAGE_SIZE // SEL_BLOCK), axis=1)
    one = lambda w: pl.BlockSpec((1, 1, w), lambda i, a, p: (i, 0, 0))
    win_spec = pl.BlockSpec((1, 2, N_KV_HEADS, HEAD_DIM, n_win), lambda i, a, p: (i, 0, 0, 0, 0))
    grid_spec = pltpu.PrefetchScalarGridSpec(
        num_scalar_prefetch=2,
        grid=(n_seq,),
        in_specs=[pl.BlockSpec(memory_space=pl.ANY), one(NSA_WIDTH), one(GATE_PAD), one(NSA_WIDTH), one(KV_WIDTH),
                  one(KV_WIDTH), win_spec],
        out_specs=[one(NSA_WIDTH), win_spec],
        scratch_shapes=[pltpu.VMEM((2, n_blk, 2, HEAD_DIM, PAGE_SIZE), F32), pltpu.SemaphoreType.DMA((2,))],
    )
    return pl.pallas_call(
        functools.partial(_attn_sample_body, n_pages=n_pages, n_seq=n_seq, k_sel=k_sel),
        grid_spec=grid_spec,
        out_shape=[jax.ShapeDtypeStruct((n_seq, 1, NSA_WIDTH), BF16), jax.ShapeDtypeStruct(win_t.shape, F32)],
        compiler_params=_cparams("arbitrary"),
        name="attn_sample",
    )(idx, pages.reshape(-1), pool_t, qb, gl, o_c, kvs_new, kvw_new, win_t)


def _tail_sample_body(x_ref, o_ref, u_ref, h0r_ref, h0i_ref, hist2_ref, hist1_ref, bb_ref, cc_ref, d_ref, abr_ref,
                      abi_ref, wglu_ref, bglu_ref, wout_ref, nf_ref, wup_ref, cw_ref, cb_ref, wdown_ref, nfin_ref,
                      out_ref, hr_ref, hi_ref, cs_ref):
    u = u_ref[...]
    bu = _split_dot(u, bb_ref[...])
    ar, ai = abr_ref[...], abi_ref[...]
    h0r, h0i = h0r_ref[...], h0i_ref[...]
    hr = ar * h0r - ai * h0i + bu[:, :SSM_N]
    hi = ar * h0i + ai * h0r + bu[:, SSM_N:]
    hr_ref[...] = hr
    hi_ref[...] = hi
    y = _dot(hr.astype(BF16), cc_ref[:SSM_N, :]) + _dot(hi.astype(BF16), cc_ref[SSM_N:, :]) + d_ref[...] * u
    x1 = _mix_out(x_ref[...], o_ref[...], y, wglu_ref, bglu_ref, wout_ref)
    hn = _rms(x1, nf_ref[...]).astype(BF16)

    def prev_rows(lo, hi_col, hu):
        cs_ref[:, lo:hi_col] = hist1_ref[:, lo:hi_col]
        cs_ref[:, 2 * D_FF + lo:2 * D_FF + hi_col] = hu
        return hist2_ref[:, lo:hi_col], hist1_ref[:, lo:hi_col]

    x2 = x1 + _ffn_chunks(hn, wup_ref, cw_ref, cb_ref, wdown_ref, prev_rows)
    out_ref[...] = _rms(x2, nfin_ref[...])


def _tail_sample(x, o_nsa, u, h0r, h0i, hist2, hist1, bb, cc, d, ab_re, ab_im, w):
    n = x.shape[0]
    sds = lambda width: jax.ShapeDtypeStruct((n, width), F32)
    return pl.pallas_call(
        _tail_sample_body,
        out_shape=[sds(D_MODEL), sds(SSM_N), sds(SSM_N), sds((CONV_W - 1) * 2 * D_FF)],
        compiler_params=pltpu.CompilerParams(vmem_limit_bytes=VMEM_LIMIT),
        name="tail_sample",
    )(x, o_nsa, u, h0r, h0i, hist2, hist1, bb, cc, d, ab_re, ab_im, w["w_glu"], w["b_glu"], w["w_out"],
      w["norm_ffn"], w["w_up"], w["conv_w"], w["conv_b"], w["w_down"], w["norm_final"])


def _pad_w_in(w_in):
    c = NSA_WIDTH + 3 * KV_WIDTH
    return jnp.concatenate([w_in[:, :c], w_in[:, c + N_GATES:], w_in[:, c:c + N_GATES],
                            jnp.zeros((D_MODEL, GATE_PAD - N_GATES), w_in.dtype)], axis=1).astype(BF16)


def _cmp_weight(w_cmp):
    w = w_cmp.reshape(2, CMP_LEN // CMP_STRIDE, CMP_STRIDE, HEAD_DIM, HEAD_DIM)
    eye_x = jnp.eye(2, dtype=w.dtype)
    eye_h = jnp.eye(N_KV_HEADS, dtype=w.dtype)
    big = (w.transpose(2, 0, 3, 1, 4)[:, :, None, :, :, None, None, :]
           * eye_x[None, :, None, None, None, :, None, None] * eye_h[None, None, :, None, None, None, :, None])
    return big.reshape(SUB_W, 2 * KV_WIDTH).astype(BF16)


def _cmp_weight_pos(w_cmp):
    w = w_cmp.reshape(2, CMP_LEN // CMP_STRIDE, CMP_STRIDE, HEAD_DIM, HEAD_DIM)
    eye_h = jnp.eye(N_KV_HEADS, dtype=w.dtype)
    big = (w.transpose(0, 2, 3, 1, 4)[:, :, None, :, :, None, :]
           * eye_h[None, None, :, None, None, :, None])
    half = N_KV_HEADS * HEAD_DIM
    return big.reshape(2, CMP_STRIDE * half, 2 * half).astype(BF16)


def _pe_sub(pe_cmp):
    rows = jnp.broadcast_to(pe_cmp.transpose(1, 0, 2)[:, :, None, :], (CMP_LEN, 2, N_KV_HEADS, HEAD_DIM))
    sub = rows.reshape(CMP_LEN // CMP_STRIDE, SUB_W)
    return jnp.concatenate([sub, jnp.zeros((8 - sub.shape[0], SUB_W), sub.dtype)], axis=0)


def _sel_map(n_cmp, n_sel, n_sel_pad):
    c0 = (jnp.arange(n_cmp) * CMP_STRIDE)[:, None]
    s0 = (jnp.arange(n_sel_pad) * SEL_BLOCK)[None, :]
    hit = (c0 < s0 + SEL_BLOCK) & (c0 + CMP_LEN > s0) & (jnp.arange(n_sel_pad)[None, :] < n_sel)
    return hit.astype(BF16)


def _expand_map(n_sel, t, kc_len):
    hit = jnp.arange(t)[None, :] // SEL_BLOCK == jnp.arange(n_sel)[:, None]
    return hit.astype(BF16).reshape(n_sel, t // kc_len, kc_len).transpose(1, 0, 2)


def kernel(x_prompt, x_sample, cache_kv_cmp, cache_kv_sel, cache_kv_win, state_ssm_re, state_ssm_im, state_ffn_conv, page_table, norm_mix, w_in, pe_cmp, w_cmp, ssm_a_re, ssm_a_im, ssm_log_dt, ssm_b_re, ssm_b_im, ssm_c_re, ssm_c_im, ssm_d, w_glu, b_glu, w_out, norm_ffn, w_up, conv_w, conv_b, w_down, norm_final):
    depth = w_in.shape[0]
    assert depth == 1, "single-layer trunk"
    b, t, _ = x_prompt.shape
    bd, s, _ = x_sample.shape
    assert s == 1, "one new position per sample sequence"
    n_pages = page_table.shape[1]
    l = 0
    w_pad = _pad_w_in(w_in[l])
    g_mix = norm_mix[l].reshape(1, D_MODEL)
    w_big = _cmp_weight(w_cmp[l])
    pe_sub = _pe_sub(pe_cmp[l])
    ab_re, ab_im, bb_re, bb_im = _ssm_params(ssm_log_dt[l], ssm_a_re[l], ssm_a_im[l], ssm_b_re[l], ssm_b_im[l])
    bb = jnp.concatenate([_block_diag_in(bb_re), _block_diag_in(bb_im)], axis=1).astype(BF16)
    cc = jnp.concatenate([_block_diag_out(ssm_c_re[l]), -_block_diag_out(ssm_c_im[l])], axis=0).astype(BF16)
    d_row = ssm_d[l].reshape(1, SSM_WIDTH)
    n_blk = SSM_WIDTH // LANES
    ch, st = LANES, SSM_N // n_blk
    bb_blk = jnp.stack([jnp.concatenate([bb[j * ch:(j + 1) * ch, j * st:(j + 1) * st],
                                         bb[j * ch:(j + 1) * ch, SSM_N + j * st:SSM_N + (j + 1) * st]], axis=1)
                        for j in range(n_blk)])
    cc_blk = jnp.stack([jnp.concatenate([cc[j * st:(j + 1) * st, j * ch:(j + 1) * ch],
                                         cc[SSM_N + j * st:SSM_N + (j + 1) * st, j * ch:(j + 1) * ch]], axis=0)
                        for j in range(n_blk)])
    tail_w = {"w_glu": w_glu[l].astype(BF16), "b_glu": b_glu[l].reshape(1, -1), "w_out": w_out[l].astype(BF16),
              "norm_ffn": norm_ffn[l].reshape(1, -1), "w_up": w_up[l].astype(BF16), "conv_w": conv_w[l],
              "conv_b": conv_b[l].reshape(1, -1), "w_down": w_down[l].astype(BF16),
              "norm_final": norm_final.reshape(1, -1)}

    kvc, _, _, kvct, kvst, kvwt, kstb, kwtb, gl, u, qb = _inproj(x_prompt, g_mix, w_pad, 512)
    n_sub = t // CMP_STRIDE
    kc, cmp_bias = _cmp_prompt(kvc.reshape(b, n_sub, SUB_W), pe_sub, w_big)
    n_sel = t // SEL_BLOCK
    o_nsa = _attn_prompt(qb, gl, kc, kstb, kwtb, _sel_map(n_sub, n_sel, n_sel).T)
    y_ssm, p_hr, p_hi = _ssm_prompt(u, bb_blk, cc_blk, d_row, ab_re, ab_im)
    y_prompt, p_conv = _tail_prompt(x_prompt, o_nsa, y_ssm, tail_w)
    win_keep = min(WINDOW, t)
    kv_rows = lambda a: a.reshape(a.shape[0], 2, N_KV_HEADS, HEAD_DIM, a.shape[2]).transpose(0, 4, 1, 2, 3)[None]
    st_shape = (depth, b, SSM_GROUPS, SSM_STATE)

    kvc_n, kvs_n, kvw_n, kvct_n, kvst_n, _, _, _, gl_n, u_n, qb_n = _inproj(x_sample.reshape(1, bd, D_MODEL), g_mix, w_pad, bd)
    per_seq = lambda a: a.reshape(bd, 1, a.shape[-1])
    n_sub_s = n_pages * (PAGE_SIZE // CMP_STRIDE)
    n_sel_s = n_pages * (PAGE_SIZE // SEL_BLOCK) + 1
    n_sel_pad = -(-n_sel_s // LANES) * LANES
    k_sel = min(N_SELECT, n_sel_s)
    pos_minor = lambda a: a.transpose(0, 2, 3, 4, 1)
    o_c, picked = _cmp_sample(page_table, pos_minor(cache_kv_cmp[l]), per_seq(kvc_n), per_seq(qb_n), cmp_bias,
                              _cmp_weight_pos(w_cmp[l]), _sel_map(n_sub_s, n_sel_s, n_sel_pad))
    idx = picked[:, :N_KV_HEADS, :k_sel].reshape(-1)
    n_buf = cache_kv_win.shape[2]
    assert n_buf == WINDOW, "window buffer holds exactly WINDOW rows"
    o_nsa_s, s_win = _attn_sample(idx, page_table, pos_minor(cache_kv_sel[l]), per_seq(qb_n), per_seq(gl_n), o_c,
                                  per_seq(kvs_n), per_seq(kvw_n), pos_minor(cache_kv_win[l]), k_sel)
    hist = state_ffn_conv[l]
    y_sample, s_hr, s_hi, s_conv = _tail_sample(
        x_sample.reshape(bd, D_MODEL), o_nsa_s.reshape(bd, -1), u_n.reshape(bd, -1),
        state_ssm_re[l].reshape(bd, SSM_N), state_ssm_im[l].reshape(bd, SSM_N), hist[:, 0], hist[:, 1], bb, cc, d_row,
        ab_re, ab_im, tail_w)
    kv_rows_s = lambda a: kv_rows(a).reshape(depth, bd, s, 2, N_KV_HEADS, HEAD_DIM)
    st_shape_s = (depth, bd, SSM_GROUPS, SSM_STATE)
    return (y_prompt, y_sample.reshape(bd, s, D_MODEL),
            kv_rows(kvct), kv_rows(kvst), kv_rows(kvwt[:, :, t - win_keep:]),
            p_hr.reshape(st_shape), p_hi.reshape(st_shape), p_conv.reshape(depth, b, CONV_W - 1, 2 * D_FF),
            kv_rows_s(kvct_n), kv_rows_s(kvst_n),
            s_win.transpose(0, 4, 1, 2, 3)[None],
            s_hr.reshape(st_shape_s), s_hi.reshape(st_shape_s), s_conv.reshape(depth, bd, CONV_W - 1, 2 * D_FF))
```

```python
import functools
import math

import jax
import jax.numpy as jnp
from jax import lax
from jax.experimental import pallas as pl
from jax.experimental.pallas import tpu as pltpu

D_MODEL = 1024
N_HEADS = 8
N_KV_HEADS = 2
GQA = N_HEADS // N_KV_HEADS
HEAD_DIM = 64
NSA_WIDTH = N_HEADS * HEAD_DIM
KV_WIDTH = 2 * N_KV_HEADS * HEAD_DIM
N_GATES = 3 * N_HEADS
CMP_LEN = 32
CMP_STRIDE = 16
SEL_BLOCK = 64
N_SELECT = 16
WINDOW = 512
Q_BLOCK = 128
PAGE_SIZE = 128
SSM_WIDTH = D_MODEL - NSA_WIDTH
SSM_GROUP = 16
SSM_GROUPS = SSM_WIDTH // SSM_GROUP
SSM_STATE = 64
SSM_N = SSM_GROUPS * SSM_STATE
D_FF = (D_MODEL * 11 // 4 + 127) // 128 * 128
CONV_W = 3
EPS = 1e-6
NEG = -1e30
BIG = 1e30
BELOW_NEG = -3e38

LANES = 128
GATE_PAD = LANES
IN_PAD = NSA_WIDTH + 3 * KV_WIDTH + SSM_WIDTH + GATE_PAD
SUB_W = CMP_STRIDE * KV_WIDTH
VMEM_LIMIT = 56 * 1024 * 1024

F32 = jnp.float32
BF16 = jnp.bfloat16


def _nt_dot(a, b):
    return lax.dot_general(a, b, (((1,), (1,)), ((), ())), preferred_element_type=F32)


def _dot(a, b):
    return jnp.dot(a, b, preferred_element_type=F32)


def _sigmoid(x):
    return 1.0 / (1.0 + jnp.exp(-x))


def _gelu_tanh(x):
    return 0.5 * x * (1.0 + jnp.tanh(math.sqrt(2.0 / math.pi) * (x + 0.044715 * (x * x * x))))


def _rms(x, g):
    return x * lax.rsqrt(jnp.mean(x * x, axis=-1, keepdims=True) + EPS) * g


def _split_dot(x, w_bf):
    hi = x.astype(BF16)
    lo = (x - hi.astype(F32)).astype(BF16)
    return _dot(hi, w_bf) + _dot(lo, w_bf)


def _cparams(*sem):
    return pltpu.CompilerParams(dimension_semantics=sem, vmem_limit_bytes=VMEM_LIMIT)


def _const_spec(shape):
    nd = len(shape)
    return pl.BlockSpec(shape, lambda *_: (0,) * nd, pipeline_mode=pl.Buffered(1))


def _inproj_body(x_ref, g_ref, w_ref, kvc_ref, kvs_ref, kvw_ref, kvct_ref, kvst_ref, kvwt_ref, kstb_ref, kwtb_ref,
                 gl_ref, u_ref, qb_ref):
    h = _rms(x_ref[0], g_ref[...])
    z = _dot(h.astype(BF16), w_ref[...])
    tm = z.shape[0]
    c = NSA_WIDTH
    qb_ref[0] = (z[:, :c] * (HEAD_DIM ** -0.5)).astype(BF16)
    for rm_ref, t_ref, tb_ref in ((kvc_ref, kvct_ref, None), (kvs_ref, kvst_ref, kstb_ref),
                                  (kvw_ref, kvwt_ref, kwtb_ref)):
        rows = z[:, c:c + KV_WIDTH]
        c += KV_WIDTH
        rm_ref[0] = rows
        cols = rows.T
        t_ref[0] = cols
        if tb_ref is not None:
            cols_bf = cols.astype(BF16)
            for k in range(tm // LANES):
                tb_ref[0, k] = cols_bf[:, k * LANES:(k + 1) * LANES]
    u_ref[0] = z[:, c:c + SSM_WIDTH]
    gl_ref[0] = z[:, c + SSM_WIDTH:]


def _inproj(x, g, w_pad, tm):
    b, t, _ = x.shape
    row = lambda w: pl.BlockSpec((1, tm, w), lambda i, j: (i, j, 0))
    col = pl.BlockSpec((1, KV_WIDTH, tm), lambda i, j: (i, 0, j))
    chunk = pl.BlockSpec((1, tm // LANES, KV_WIDTH, LANES), lambda i, j: (i, j, 0, 0))
    rm_t = jax.ShapeDtypeStruct((b, t, KV_WIDTH), F32)
    col_t = jax.ShapeDtypeStruct((b, KV_WIDTH, t), F32)
    chunk_t = jax.ShapeDtypeStruct((b, t // LANES, KV_WIDTH, LANES), BF16)
    return pl.pallas_call(
        _inproj_body,
        grid=(b, t // tm),
        in_specs=[row(D_MODEL), _const_spec((1, D_MODEL)), _const_spec((D_MODEL, IN_PAD))],
        out_specs=[row(KV_WIDTH)] * 3 + [col] * 3 + [chunk] * 2 + [row(GATE_PAD), row(SSM_WIDTH), row(NSA_WIDTH)],
        out_shape=[rm_t] * 3 + [col_t] * 3 + [chunk_t] * 2
        + [jax.ShapeDtypeStruct((b, t, GATE_PAD), F32), jax.ShapeDtypeStruct((b, t, SSM_WIDTH), F32),
           jax.ShapeDtypeStruct((b, t, NSA_WIDTH), BF16)],
        compiler_params=_cparams("parallel", "parallel"),
        name="in_proj",
    )(x, g, w_pad)


def _cmp_prompt_body(sub_ref, pe_ref, w_ref, kc_ref, bias_ref):
    sub, pe = sub_ref[0], pe_ref[...]
    n_sub = sub.shape[0]
    half = N_KV_HEADS * HEAD_DIM
    tokens, biases = [], []
    for x in range(2):
        pick = lambda a: jnp.concatenate([a[:, s * KV_WIDTH + x * half:s * KV_WIDTH + (x + 1) * half]
                                          for s in range(CMP_STRIDE)], axis=1).astype(BF16)
        parts = _dot(pick(sub), w_ref[x])
        pe_parts = _dot(pick(pe), w_ref[x])
        bias = pe_parts[0:1, :half] + pe_parts[1:2, half:]
        nxt = pltpu.roll(parts[:, half:], n_sub - 1, 0)
        tokens.append(parts[:, :half] + nxt + bias)
        biases.append(bias)
    kc_ref[0] = jnp.concatenate(tokens, axis=1).astype(BF16)
    bias_ref[...] = jnp.broadcast_to(jnp.concatenate(biases, axis=1), bias_ref.shape)


def _cmp_prompt(sub, pe_sub, w_pos):
    b, n_sub, _ = sub.shape
    return pl.pallas_call(
        _cmp_prompt_body,
        grid=(b,),
        in_specs=[pl.BlockSpec((1, n_sub, SUB_W), lambda i: (i, 0, 0)), _const_spec(pe_sub.shape),
                  _const_spec(w_pos.shape)],
        out_specs=[pl.BlockSpec((1, n_sub, KV_WIDTH), lambda i: (i, 0, 0)),
                   pl.BlockSpec((8, KV_WIDTH), lambda i: (0, 0))],
        out_shape=[jax.ShapeDtypeStruct((b, n_sub, KV_WIDTH), BF16), jax.ShapeDtypeStruct((8, KV_WIDTH), F32)],
        compiler_params=_cparams("arbitrary"),
        name="cmp_prompt",
    )(sub, pe_sub, w_pos)


def _stack_heads(q, h):
    return jnp.concatenate([q[:, (GQA * h + g) * HEAD_DIM:(GQA * h + g + 1) * HEAD_DIM] for g in range(GQA)], axis=0)


def _stack_heads_single(q, h):
    row = lax.broadcasted_iota(jnp.int32, (8, 1), 0)
    q8 = jnp.broadcast_to(q, (8, NSA_WIDTH))
    out = jnp.zeros((8, HEAD_DIM), F32)
    for g in range(GQA):
        lo = (GQA * h + g) * HEAD_DIM
        out = out + jnp.where(row == g, q8[:, lo:lo + HEAD_DIM], 0.0)
    return out.astype(BF16)


def _masked_softmax_rows(s, mask):
    sm = jnp.where(mask, s, NEG)
    m = jnp.max(sm, axis=-1, keepdims=True)
    p = jnp.where(mask, jnp.exp(sm - m), 0.0)
    l = jnp.sum(p, axis=-1, keepdims=True)
    return p * (1.0 / jnp.where(l > 0.0, l, 1.0))


def _block_scores(imp, blk, qpos):
    first = blk * SEL_BLOCK
    own_or_imp = jnp.where(first + SEL_BLOCK > qpos, BIG, imp)
    return jnp.where(blk == 0, BIG, jnp.where(first <= qpos, own_or_imp, NEG))


def _masked_softmax_cols(s, mask):
    sm = jnp.where(mask, s, NEG)
    m = jnp.max(sm, axis=0, keepdims=True)
    p = jnp.where(mask, jnp.exp(sm - m), 0.0)
    l = jnp.sum(p, axis=0, keepdims=True)
    return p * (1.0 / jnp.where(l > 0.0, l, 1.0))


def _top_k_mask_cols(score, blk, k):
    n = float(score.shape[0])
    sel = jnp.zeros(score.shape, F32)
    for _ in range(k):
        mx = jnp.max(score, axis=0, keepdims=True)
        idx = jnp.min(jnp.where(score == mx, blk, n), axis=0, keepdims=True)
        hit = blk == idx
        sel = sel + jnp.where(hit, jnp.where(mx > 0.5 * NEG, 1.0, 0.0), 0.0)
        score = jnp.where(hit, BELOW_NEG, score)
    return sel


KEY_BLOCKS = 4
ROW_BLOCK = 32


def _attn_prompt_body(q_ref, gl_ref, kc_ref, kst_ref, kwt_ref, selmap_ref, exp_ref, o_ref, m_s, l_s, a_s, acc_s, s_s,
                      p_s):
    qb_idx = pl.program_id(1)
    start = qb_idx * Q_BLOCK
    kc_len = KEY_BLOCKS * LANES
    q = q_ref[0]
    gate = _sigmoid(gl_ref[0])
    qpos = start + lax.broadcasted_iota(jnp.int32, (Q_BLOCK, 1), 0)
    qpos_l = start + lax.broadcasted_iota(jnp.int32, (1, Q_BLOCK), 1)
    qpos4_l = jnp.concatenate([qpos_l] * GQA, axis=1)
    n_cmp = kc_ref.shape[1]
    n_sel = selmap_ref.shape[0]
    cmp_end = lax.broadcasted_iota(jnp.int32, (n_cmp, 1), 0) * CMP_STRIDE + (CMP_LEN - 1)
    blk = lax.broadcasted_iota(jnp.int32, (n_sel, 1), 0)
    qs, o_c, scores = [], [], []
    for h in range(N_KV_HEADS):
        k_lo, v_lo = h * HEAD_DIM, (N_KV_HEADS + h) * HEAD_DIM
        qs.append(_stack_heads(q, h))
        p_t = _masked_softmax_cols(_nt_dot(kc_ref[0, :, k_lo:k_lo + HEAD_DIM], qs[h]), cmp_end <= qpos4_l)
        o_c.append(_dot(p_t.T.astype(BF16), kc_ref[0, :, v_lo:v_lo + HEAD_DIM]))
        p_sum = p_t[:, 0:Q_BLOCK]
        for g in range(1, GQA):
            p_sum = p_sum + p_t[:, g * Q_BLOCK:(g + 1) * Q_BLOCK]
        hi = p_sum.astype(BF16)
        lo = (p_sum - hi.astype(F32)).astype(BF16)
        imp = _dot(selmap_ref[...], hi) + _dot(selmap_ref[...], lo)
        scores.append(_block_scores(imp, blk, qpos_l))
    sel_t = _top_k_mask_cols(jnp.concatenate(scores, axis=1), blk.astype(F32), min(N_SELECT, n_sel))
    q_ext = []
    for h in range(N_KV_HEADS):
        picked = sel_t[:, h * Q_BLOCK:(h + 1) * Q_BLOCK].T
        neg = jnp.where(picked > 0.5, 0.0, NEG).astype(BF16)
        q_ext.append(jnp.concatenate([qs[h], jnp.concatenate([neg] * GQA, axis=0)], axis=1))
    m_s[...] = jnp.full(m_s.shape, NEG, F32)
    l_s[...] = jnp.zeros(l_s.shape, F32)
    acc_s[...] = jnp.zeros(acc_s.shape, F32)
    n_chunks = (start + Q_BLOCK + kc_len - 1) // kc_len

    def sel_chunk(c, causal):
        for h in range(N_KV_HEADS):
            k_lo, v_lo = h * HEAD_DIM, (N_KV_HEADS + h) * HEAD_DIM
            k_t = jnp.concatenate([kst_ref[0, c * KEY_BLOCKS + j, k_lo:k_lo + HEAD_DIM, :]
                                   for j in range(KEY_BLOCKS)], axis=1)
            v_t = jnp.concatenate([kst_ref[0, c * KEY_BLOCKS + j, v_lo:v_lo + HEAD_DIM, :]
                                   for j in range(KEY_BLOCKS)], axis=1)
            s = _dot(q_ext[h], jnp.concatenate([k_t, exp_ref[c]], axis=0))
            s_s[h] = s if causal is None else s + causal
            for r0 in range(0, GQA * Q_BLOCK, ROW_BLOCK):
                rows = slice(r0, r0 + ROW_BLOCK)
                m_prev = m_s[h, rows, :]
                m_next = jnp.maximum(m_prev, jnp.max(s_s[h, rows, :], axis=-1, keepdims=True))
                a_s[h, rows, :] = jnp.exp(m_prev - m_next)
                m_s[h, rows, :] = m_next
            for r0 in range(0, GQA * Q_BLOCK, ROW_BLOCK):
                rows = slice(r0, r0 + ROW_BLOCK)
                p = jnp.exp(s_s[h, rows, :] - jnp.tile(m_s[h, rows, :], (1, KEY_BLOCKS)))
                l_s[h, rows, :] = a_s[h, rows, :] * l_s[h, rows, :] + jnp.sum(p, axis=-1, keepdims=True)
                p_s[h, rows, :] = p.astype(BF16)
            acc_s[h] = acc_s[h] * a_s[h, :, :HEAD_DIM] + _nt_dot(p_s[h], v_t)

    kpos = (n_chunks - 1) * kc_len + lax.broadcasted_iota(jnp.int32, (1, kc_len), 1)
    sel_chunk(n_chunks - 1, jnp.concatenate([jnp.where(kpos <= qpos, 0.0, NEG)] * GQA, axis=0))

    def earlier_chunk(it, carry):
        sel_chunk(n_chunks - 1 - it, None)
        return carry

    lax.fori_loop(1, n_chunks, earlier_chunk, 0)
    wpos = start - WINDOW + lax.broadcasted_iota(jnp.int32, (1, WINDOW + Q_BLOCK), 1)
    visible = jnp.where(wpos <= qpos, jnp.where(wpos >= jnp.maximum(qpos - WINDOW, 0), 0.0, NEG), NEG)
    win_bias = jnp.concatenate([visible] * GQA, axis=0)
    head_out = []
    for h in range(N_KV_HEADS):
        k_lo, v_lo = h * HEAD_DIM, (N_KV_HEADS + h) * HEAD_DIM
        o_s = acc_s[h] * (1.0 / l_s[h])[:, :HEAD_DIM]
        kw, vw = [], []
        for c in range((WINDOW + Q_BLOCK) // LANES):
            src = jnp.maximum(qb_idx - WINDOW // LANES + c, 0)
            kw.append(kwt_ref[0, src, k_lo:k_lo + HEAD_DIM, :])
            vw.append(kwt_ref[0, src, v_lo:v_lo + HEAD_DIM, :])
        s = _dot(qs[h], jnp.concatenate(kw, axis=1))
        sm = s + win_bias
        p = jnp.exp(sm - jnp.max(sm, axis=-1, keepdims=True))
        o_w = _nt_dot(p.astype(BF16), jnp.concatenate(vw, axis=1)) * (1.0 / jnp.sum(p, axis=-1, keepdims=True))
        for g in range(GQA):
            col = (GQA * h + g) * 3
            r = slice(g * Q_BLOCK, (g + 1) * Q_BLOCK)
            head_out.append(gate[:, col:col + 1] * o_c[h][r] + gate[:, col + 1:col + 2] * o_s[r]
                            + gate[:, col + 2:col + 3] * o_w[r])
    o_ref[0] = jnp.concatenate(head_out, axis=-1).astype(BF16)


def _attn_prompt(qb, gl, kc, kst, kwt, selmap_t):
    b, t, _ = qb.shape
    kc_len = KEY_BLOCKS * LANES
    expand = _expand_map(selmap_t.shape[0], t, kc_len)
    rows = GQA * Q_BLOCK
    whole = lambda a: pl.BlockSpec((1,) + a.shape[1:], lambda i, j: (i,) + (0,) * (a.ndim - 1))
    return pl.pallas_call(
        _attn_prompt_body,
        grid=(b, t // Q_BLOCK),
        in_specs=[pl.BlockSpec((1, Q_BLOCK, NSA_WIDTH), lambda i, j: (i, j, 0)),
                  pl.BlockSpec((1, Q_BLOCK, GATE_PAD), lambda i, j: (i, j, 0)),
                  whole(kc), whole(kst), whole(kwt), _const_spec(selmap_t.shape), _const_spec(expand.shape)],
        out_specs=pl.BlockSpec((1, Q_BLOCK, NSA_WIDTH), lambda i, j: (i, j, 0)),
        out_shape=jax.ShapeDtypeStruct((b, t, NSA_WIDTH), BF16),
        scratch_shapes=[pltpu.VMEM((N_KV_HEADS, rows, LANES), F32), pltpu.VMEM((N_KV_HEADS, rows, LANES), F32),
                        pltpu.VMEM((N_KV_HEADS, rows, LANES), F32), pltpu.VMEM((N_KV_HEADS, rows, HEAD_DIM), F32),
                        pltpu.VMEM((N_KV_HEADS, rows, kc_len), F32), pltpu.VMEM((N_KV_HEADS, rows, kc_len), BF16)],
        compiler_params=_cparams("parallel", "arbitrary"),
        name="attn_prompt",
    )(qb, gl, kc, kst, kwt, selmap_t, expand)


def _ssm_param_body(ldt_ref, are_ref, aim_ref, bre_ref, bim_ref, abr_ref, abi_ref, bbr_ref, bbi_ref):
    dt = jnp.exp(ldt_ref[...])
    are, aim = are_ref[...], aim_ref[...]
    mag = jnp.exp(dt * are)
    ab_re, ab_im = mag * jnp.cos(dt * aim), mag * jnp.sin(dt * aim)
    den = are * are + aim * aim
    zr, zi = ab_re - 1.0, ab_im
    f_re = (zr * are + zi * aim) / den
    f_im = (zi * are - zr * aim) / den
    abr_ref[...] = ab_re
    abi_ref[...] = ab_im
    bbr_ref[...] = f_re * bre_ref[...] - f_im * bim_ref[...]
    bbi_ref[...] = f_re * bim_ref[...] + f_im * bre_ref[...]


def _ssm_params(log_dt, a_re, a_im, b_re, b_im):
    col = lambda a: a.reshape(SSM_N, 1)
    ldt = col(jnp.broadcast_to(log_dt[:, None], (SSM_GROUPS, SSM_STATE)))
    col_t = jax.ShapeDtypeStruct((SSM_N, 1), F32)
    mat_t = jax.ShapeDtypeStruct((SSM_N, SSM_GROUP), F32)
    ab_re, ab_im, bb_re, bb_im = pl.pallas_call(
        _ssm_param_body, out_shape=[col_t, col_t, mat_t, mat_t], name="ssm_params",
    )(ldt, col(a_re), col(a_im), b_re.reshape(SSM_N, SSM_GROUP), b_im.reshape(SSM_N, SSM_GROUP))
    return ab_re.reshape(1, SSM_N), ab_im.reshape(1, SSM_N), bb_re, bb_im


def _block_diag_in(bb):
    m = bb.reshape(SSM_GROUPS, SSM_STATE, SSM_GROUP).transpose(0, 2, 1)
    eye = jnp.eye(SSM_GROUPS, dtype=bb.dtype)
    return (eye[:, None, :, None] * m[:, :, None, :]).reshape(SSM_WIDTH, SSM_N)


def _block_diag_out(c):
    m = c.transpose(0, 2, 1)
    eye = jnp.eye(SSM_GROUPS, dtype=c.dtype)
    return (eye[:, None, :, None] * m[:, :, None, :]).reshape(SSM_N, SSM_WIDTH)


def _ssm_prompt_body(u_ref, bb_ref, cc_ref, d_ref, abr_ref, abi_ref, y_ref, hr_ref, hi_ref, bu_s, hs_s, st_s):
    tc = u_ref.shape[1]

    @pl.when(pl.program_id(1) == 0)
    def _():
        st_s[...] = jnp.zeros(st_s.shape, F32)

    u = u_ref[0]
    u_bf = u.astype(BF16)
    n_blocks = bb_ref.shape[0]
    ch, st = SSM_WIDTH // n_blocks, SSM_N // n_blocks
    for j in range(n_blocks):
        z = _dot(u_bf[:, j * ch:(j + 1) * ch], bb_ref[j])
        bu_s[:, j * st:(j + 1) * st] = z[:, :st]
        bu_s[:, SSM_N + j * st:SSM_N + (j + 1) * st] = z[:, st:]
    ar, ai = abr_ref[...], abi_ref[...]

    def step(t, carry):
        hr, hi = carry
        nr = ar * hr - ai * hi + bu_s[pl.ds(t, 1), :SSM_N]
        ni = ar * hi + ai * hr + bu_s[pl.ds(t, 1), SSM_N:]
        hs_s[pl.ds(t, 1), :SSM_N] = nr
        hs_s[pl.ds(t, 1), SSM_N:] = ni
        return nr, ni

    hr, hi = lax.fori_loop(0, tc, step, (st_s[0:1, :], st_s[1:2, :]), unroll=8)
    st_s[0:1, :] = hr
    st_s[1:2, :] = hi
    ys = []
    for j in range(n_blocks):
        h_blk = jnp.concatenate([hs_s[:, j * st:(j + 1) * st], hs_s[:, SSM_N + j * st:SSM_N + (j + 1) * st]], axis=1)
        ys.append(_dot(h_blk.astype(BF16), cc_ref[j]))
    y_ref[0] = jnp.concatenate(ys, axis=1) + d_ref[...] * u
    hr_ref[0] = hr
    hi_ref[0] = hi


def _ssm_prompt(u, bb, cc, d, ab_re, ab_im, tc=256):
    b, t, _ = u.shape
    st = jax.ShapeDtypeStruct((b, 1, SSM_N), F32)
    st_spec = pl.BlockSpec((1, 1, SSM_N), lambda i, j: (i, 0, 0))
    return pl.pallas_call(
        _ssm_prompt_body,
        grid=(b, t // tc),
        in_specs=[pl.BlockSpec((1, tc, SSM_WIDTH), lambda i, j: (i, j, 0)), _const_spec(bb.shape),
                  _const_spec(cc.shape), _const_spec(d.shape), _const_spec(ab_re.shape), _const_spec(ab_im.shape)],
        out_specs=[pl.BlockSpec((1, tc, SSM_WIDTH), lambda i, j: (i, j, 0)), st_spec, st_spec],
        out_shape=[jax.ShapeDtypeStruct((b, t, SSM_WIDTH), F32), st, st],
        scratch_shapes=[pltpu.VMEM((tc, 2 * SSM_N), F32), pltpu.VMEM((tc, 2 * SSM_N), F32),
                        pltpu.VMEM((8, SSM_N), F32)],
        compiler_params=_cparams("parallel", "arbitrary"),
        name="ssm_prompt",
    )(u, bb, cc, d, ab_re, ab_im)


FF_CHUNK = 256


def _mix_out(x, o_nsa, y_ssm, wglu_ref, bglu_ref, wout_ref):
    z = _dot(_gelu_tanh(y_ssm).astype(BF16), wglu_ref[...]) + bglu_ref[...]
    glu = z[:, :SSM_WIDTH] * _sigmoid(z[:, SSM_WIDTH:])
    return x + _dot(o_nsa, wout_ref[:NSA_WIDTH, :]) + _dot(glu.astype(BF16), wout_ref[NSA_WIDTH:, :])


def _ffn_chunks(hn, wup_ref, cw_ref, cb_ref, wdown_ref, prev_rows):
    acts = []
    for j in range(D_FF // FF_CHUNK):
        conv = []
        for base in (0, D_FF):
            lo = base + j * FF_CHUNK
            hi = lo + FF_CHUNK
            hu = _dot(hn, wup_ref[:, lo:hi])
            hu2, hu1 = prev_rows(lo, hi, hu)
            conv.append(cw_ref[0:1, lo:hi] * hu2 + cw_ref[1:2, lo:hi] * hu1 + cw_ref[2:3, lo:hi] * hu
                        + cb_ref[:, lo:hi])
        a, g = conv
        acts.append((a * _sigmoid(a) * g).astype(BF16))
    return _dot(jnp.concatenate(acts, axis=1), wdown_ref[...])


def _tail_prompt_body(x_ref, o_ref, y_ref, wglu_ref, bglu_ref, wout_ref, nf_ref, wup_ref, cw_ref, cb_ref, wdown_ref,
                      nfin_ref, out_ref, cs_ref, prev_s):
    tm = x_ref.shape[1]

    @pl.when(pl.program_id(1) == 0)
    def _():
        prev_s[...] = jnp.zeros(prev_s.shape, F32)

    x1 = _mix_out(x_ref[0], o_ref[0], y_ref[0], wglu_ref, bglu_ref, wout_ref)
    hn = _rms(x1, nf_ref[...]).astype(BF16)
    row = lax.broadcasted_iota(jnp.int32, (8, 1), 0)

    def prev_rows(lo, hi, hu):
        p2, p1 = prev_s[6:7, lo:hi], prev_s[7:8, lo:hi]
        r1, r2 = pltpu.roll(hu, 1, 0), pltpu.roll(hu, 2, 0)
        hu1 = jnp.concatenate([jnp.where(row == 0, p1, r1[:8]), r1[8:]], axis=0)
        hu2 = jnp.concatenate([jnp.where(row == 0, p2, jnp.where(row == 1, p1, r2[:8])), r2[8:]], axis=0)
        prev_s[:, lo:hi] = hu[tm - 8:, :]
        cs_ref[0, :, lo:hi] = hu[tm - (CONV_W - 1):, :]
        return hu2, hu1

    x2 = x1 + _ffn_chunks(hn, wup_ref, cw_ref, cb_ref, wdown_ref, prev_rows)
    out_ref[0] = _rms(x2, nfin_ref[...])


def _tail_prompt(x, o_nsa, y_ssm, w, tm=512):
    b, t, _ = x.shape
    tile = lambda width: pl.BlockSpec((1, tm, width), lambda i, j: (i, j, 0))
    consts = [w["w_glu"], w["b_glu"], w["w_out"], w["norm_ffn"], w["w_up"], w["conv_w"], w["conv_b"], w["w_down"],
              w["norm_final"]]
    return pl.pallas_call(
        _tail_prompt_body,
        grid=(b, t // tm),
        in_specs=[tile(D_MODEL), tile(NSA_WIDTH), tile(SSM_WIDTH)] + [_const_spec(c.shape) for c in consts],
        out_specs=[tile(D_MODEL), pl.BlockSpec((1, CONV_W - 1, 2 * D_FF), lambda i, j: (i, 0, 0))],
        out_shape=[jax.ShapeDtypeStruct((b, t, D_MODEL), F32), jax.ShapeDtypeStruct((b, CONV_W - 1, 2 * D_FF), F32)],
        scratch_shapes=[pltpu.VMEM((8, 2 * D_FF), F32)],
        compiler_params=_cparams("parallel", "arbitrary"),
        name="tail_prompt",
    )(x, o_nsa, y_ssm, *consts)


def _pair_rows(q8, lane):
    row = lax.broadcasted_iota(jnp.int32, (8, 1), 0)
    out = jnp.zeros((8, LANES), F32)
    for r in range(N_HEADS):
        pair = q8[:, (r // 2) * LANES:(r // 2 + 1) * LANES]
        want_hi = r // GQA
        if r % 2 != want_hi:
            pair = pltpu.roll(pair, HEAD_DIM, 1)
        keep = (lane >= HEAD_DIM) if want_hi else (lane < HEAD_DIM)
        out = out + jnp.where(row == r, jnp.where(keep, pair, 0.0), 0.0)
    return out


def _unpair_rows(o, lane):
    pieces = []
    for j in range(N_HEADS // 2):
        lo, hi = o[2 * j:2 * j + 1], o[2 * j + 1:2 * j + 2]
        if (2 * j) // GQA == 0:
            hi = pltpu.roll(hi, HEAD_DIM, 1)
        else:
            lo = pltpu.roll(lo, HEAD_DIM, 1)
        pieces.append(jnp.where(lane < HEAD_DIM, lo, hi))
    return jnp.concatenate(pieces, axis=1)


PAIR_GROUP = 4


def _cmp_sample_body(pt_ref, pool_ref, new_ref, q_ref, bias_ref, w_ref, perm_ref, selmap_ref, oc_ref, idx_ref, buf, xs,
                     sem, *,
                     n_pages, n_seq):
    b = pl.program_id(0)
    n_pos = n_pages * PAGE_SIZE
    n_sub = n_pos // CMP_STRIDE
    half = N_KV_HEADS * HEAD_DIM
    sub_per_page = PAGE_SIZE // CMP_STRIDE

    def page_copy(seq, slot, p):
        dst = buf.at[slot, p // 2, :, :, :, pl.ds((p % 2) * PAGE_SIZE, PAGE_SIZE)]
        return pltpu.make_async_copy(pool_ref.at[pt_ref[seq * n_pages + p]], dst, sem.at[slot])

    def fetch(seq, slot):
        for p in range(n_pages):
            page_copy(seq, slot, p).start()

    @pl.when(b == 0)
    def _():
        fetch(0, 0)

    @pl.when(b + 1 < n_seq)
    def _():
        fetch(b + 1, (b + 1) % 2)

    slot = b % 2
    for p in range(n_pages):
        page_copy(b, slot, p).wait()

    row = lax.broadcasted_iota(jnp.int32, (n_sub, 1), 0)
    kv_c = []
    for x in range(2):
        for g0 in range(0, n_pages // 2, PAIR_GROUP):
            pairs = buf[slot, g0:g0 + PAIR_GROUP, x].reshape(PAIR_GROUP * half, 2 * PAGE_SIZE).astype(BF16)
            regrouped = _dot(pairs, perm_ref[...])
            for j in range(PAIR_GROUP):
                for k in range(2):
                    xs[x, 2 * (g0 + j) + k] = regrouped[j * half:(j + 1) * half, k * PAGE_SIZE:(k + 1) * PAGE_SIZE].T
        sub = jnp.concatenate([xs[x, :, s * sub_per_page:(s + 1) * sub_per_page, :].reshape(n_sub, half)
                               for s in range(CMP_STRIDE)], axis=1)
        parts = _dot(sub.astype(BF16), w_ref[x])
        new = jnp.broadcast_to(new_ref[0][:, x * half:(x + 1) * half], (8, half)).astype(BF16)
        new_part = _dot(new, w_ref[x, :half, :])[0:1, half:]
        nxt = jnp.where(row == n_sub - 1, new_part, pltpu.roll(parts[:, half:], n_sub - 1, 0))
        kv_c.append((parts[:, :half] + nxt + bias_ref[0:1, x * half:(x + 1) * half]).astype(BF16))

    q_pos = n_pos
    n_sel = selmap_ref.shape[1]
    n_real = q_pos // SEL_BLOCK + 1
    cmp_end = lax.broadcasted_iota(jnp.int32, (1, n_sub), 1) * CMP_STRIDE + (CMP_LEN - 1)
    blk = lax.broadcasted_iota(jnp.int32, (1, n_sel), 1)
    lane = lax.broadcasted_iota(jnp.int32, (1, LANES), 1)
    row8 = lax.broadcasted_iota(jnp.int32, (8, 1), 0)
    q2 = _pair_rows(jnp.broadcast_to(q_ref[0].astype(F32), (8, NSA_WIDTH)), lane).astype(BF16)
    p_c = _masked_softmax_rows(_nt_dot(q2, kv_c[0]), cmp_end <= q_pos)
    oc_ref[0] = _unpair_rows(_dot(p_c.astype(BF16), kv_c[1]), lane)
    p_sum = jnp.zeros((8, n_sub), F32)
    for h in range(N_KV_HEADS):
        in_h = (row8 >= h * GQA) & (row8 < (h + 1) * GQA)
        p_sum = p_sum + jnp.where(row8 == h, jnp.sum(jnp.where(in_h, p_c, 0.0), axis=0, keepdims=True), 0.0)
    score = _block_scores(_split_dot(p_sum, selmap_ref[...]), blk, q_pos)
    score = jnp.where((blk < n_real) & (row8 < N_KV_HEADS), score, BELOW_NEG)
    blk_f = blk.astype(F32)
    picked = jnp.full((8, LANES), -1.0, F32)
    for it in range(min(N_SELECT, n_real)):
        mx = jnp.max(score, axis=-1, keepdims=True)
        idx = jnp.min(jnp.where(score == mx, blk_f, float(n_sel)), axis=-1, keepdims=True)
        picked = jnp.where(lane == it, jnp.where(mx > 0.5 * NEG, idx, -1.0), picked)
        score = jnp.where(blk_f == idx, BELOW_NEG, score)
    idx_ref[0] = picked.astype(jnp.int32)


def _cmp_sample(page_table, pool_t, kvc_new, qb, bias, w_pos, selmap):
    n_seq, n_pages = page_table.shape
    half = N_KV_HEADS * HEAD_DIM
    pos = jnp.arange(PAGE_SIZE)
    dest = (pos % CMP_STRIDE) * (PAGE_SIZE // CMP_STRIDE) + pos // CMP_STRIDE
    perm = (dest[:, None] == jnp.arange(PAGE_SIZE)[None, :]).astype(BF16)
    zero = jnp.zeros_like(perm)
    perm = jnp.concatenate([jnp.concatenate([perm, zero], axis=1), jnp.concatenate([zero, perm], axis=1)], axis=0)
    grid_spec = pltpu.PrefetchScalarGridSpec(
        num_scalar_prefetch=1,
        grid=(n_seq,),
        in_specs=[pl.BlockSpec(memory_space=pl.ANY),
                  pl.BlockSpec((1, 1, KV_WIDTH), lambda i, pt: (i, 0, 0)),
                  pl.BlockSpec((1, 1, NSA_WIDTH), lambda i, pt: (i, 0, 0)),
                  _const_spec(bias.shape), _const_spec(w_pos.shape), _const_spec(perm.shape), _const_spec(selmap.shape)],
        out_specs=[pl.BlockSpec((1, 1, NSA_WIDTH), lambda i, pt: (i, 0, 0)),
                   pl.BlockSpec((1, 8, LANES), lambda i, pt: (i, 0, 0))],
        scratch_shapes=[pltpu.VMEM((2, n_pages // 2, 2, N_KV_HEADS, HEAD_DIM, 2 * PAGE_SIZE), F32),
                        pltpu.VMEM((2, n_pages, PAGE_SIZE, half), F32), pltpu.SemaphoreType.DMA((2,))],
    )
    return pl.pallas_call(
        functools.partial(_cmp_sample_body, n_pages=n_pages, n_seq=n_seq),
        grid_spec=grid_spec,
        out_shape=[jax.ShapeDtypeStruct((n_seq, 1, NSA_WIDTH), F32), jax.ShapeDtypeStruct((n_seq, 8, LANES), jnp.int32)],
        compiler_params=_cparams("arbitrary"),
        name="cmp_sample",
    )(page_table.reshape(-1), pool_t, kvc_new, qb, bias, w_pos, perm, selmap)


def _attn_sample_body(idx_ref, pg_ref, pool_ref, q_ref, gl_ref, oc_ref, ksn_ref, kwn_ref, win_ref, o_ref, wout_ref,
                      buf, sem, *, n_pages, n_seq, k_sel):
    b = pl.program_id(0)
    blk_per_page = PAGE_SIZE // SEL_BLOCK
    n_past = n_pages * blk_per_page
    n_blk = N_KV_HEADS * k_sel

    def fetch(seq, slot):
        for j in range(n_blk):
            page = pg_ref[seq * n_blk + j]
            pltpu.make_async_copy(pool_ref.at[page, :, j // k_sel], buf.at[slot, j], sem.at[slot]).start()

    @pl.when(b == 0)
    def _():
        fetch(0, 0)

    @pl.when(b + 1 < n_seq)
    def _():
        fetch(b + 1, (b + 1) % 2)

    slot = b % 2
    pltpu.make_async_copy(pool_ref.at[pl.ds(0, n_blk), :, 0], buf.at[slot], sem.at[slot]).wait()

    q = q_ref[0].astype(F32)
    gate = _sigmoid(gl_ref[0])
    o_c = oc_ref[0]
    ks_new = ksn_ref[0].astype(BF16).astype(F32)
    kw_new = kwn_ref[0].astype(BF16).astype(F32)
    n_win = win_ref.shape[-1]
    lane = lax.broadcasted_iota(jnp.int32, (1, PAGE_SIZE), 1)
    lane_blk = lax.shift_right_logical(lane, SEL_BLOCK.bit_length() - 1)
    head_out = []
    for h in range(N_KV_HEADS):
        k_lo, v_lo = h * HEAD_DIM, (N_KV_HEADS + h) * HEAD_DIM
        qs = _stack_heads_single(q, h)
        qf = qs.astype(F32)
        s_blocks, m_blocks = [], []
        has_new = jnp.zeros((1, 1), F32)
        for j in range(k_sel):
            i = idx_ref[b * n_blk + h * k_sel + j]
            ok = jnp.where((i >= 0) & (i < n_past), 1.0, 0.0)
            m_blocks.append(jnp.where(lane_blk == i % blk_per_page, ok, 0.0))
            has_new = jnp.maximum(has_new, jnp.where(i >= n_past, 1.0, 0.0))
            s_blocks.append(_dot(qs, buf[slot, h * k_sel + j, 0].astype(BF16)))
        past = jnp.concatenate(m_blocks, axis=1) > 0.5
        s = jnp.where(past, jnp.concatenate(s_blocks, axis=1), NEG)
        s_new = jnp.where(has_new > 0.5, jnp.sum(qf * ks_new[:, k_lo:k_lo + HEAD_DIM], axis=-1, keepdims=True), NEG)
        m = jnp.maximum(jnp.max(s, axis=-1, keepdims=True), s_new)
        p = jnp.where(past, jnp.exp(s - m), 0.0)
        p_new = jnp.where(has_new > 0.5, jnp.exp(s_new - m), 0.0)
        p_bf = p.astype(BF16)
        o_s = p_new.astype(BF16).astype(F32) * ks_new[:, v_lo:v_lo + HEAD_DIM]
        for j in range(k_sel):
            o_s = o_s + _nt_dot(p_bf[:, j * PAGE_SIZE:(j + 1) * PAGE_SIZE], buf[slot, h * k_sel + j, 1].astype(BF16))
        o_s = o_s * (1.0 / (jnp.sum(p, axis=-1, keepdims=True) + p_new))
        s = _dot(qs, win_ref[0, 0, h].astype(BF16))
        s_new = jnp.sum(qf * kw_new[:, k_lo:k_lo + HEAD_DIM], axis=-1, keepdims=True)
        m = jnp.maximum(jnp.max(s, axis=-1, keepdims=True), s_new)
        p, p_new = jnp.exp(s - m), jnp.exp(s_new - m)
        inv = 1.0 / (jnp.sum(p, axis=-1, keepdims=True) + p_new)
        o_w = (_nt_dot(p.astype(BF16), win_ref[0, 1, h].astype(BF16))
               + p_new.astype(BF16).astype(F32) * kw_new[:, v_lo:v_lo + HEAD_DIM]) * inv
        for g in range(GQA):
            hd = GQA * h + g
            col = hd * 3
            head_out.append(gate[:, col:col + 1] * o_c[:, hd * HEAD_DIM:(hd + 1) * HEAD_DIM]
                            + gate[:, col + 1:col + 2] * o_s[g:g + 1] + gate[:, col + 2:col + 3] * o_w[g:g + 1])
    o_ref[0] = jnp.concatenate(head_out, axis=-1).astype(BF16)
    d_row = lax.broadcasted_iota(jnp.int32, (HEAD_DIM, HEAD_DIM), 0)
    d_col = lax.broadcasted_iota(jnp.int32, (HEAD_DIM, HEAD_DIM), 1)
    pos = lax.broadcasted_iota(jnp.int32, (1, n_win), 1)
    new_rows = jnp.broadcast_to(kwn_ref[0], (HEAD_DIM, KV_WIDTH))
    for x in range(2):
        for h in range(N_KV_HEADS):
            lo = (x * N_KV_HEADS + h) * HEAD_DIM
            new_col = jnp.sum(jnp.where(d_row == d_col, new_rows[:, lo:lo + HEAD_DIM], 0.0), axis=-1, keepdims=True)
            wout_ref[0, x, h] = jnp.where(pos == n_win - 1, new_col, pltpu.roll(win_ref[0, x, h], n_win - 1, 1))


def _attn_sample(idx, page_table, pool_t, qb, gl, o_c, kvs_new, kvw_new, win_t, k_sel):
    n_seq, n_pages = page_table.shape
    n_win = win_t.shape[-1]
    n_blk = N_KV_HEADS * k_sel
    past_blk = jnp.clip(idx.reshape(n_seq, n_blk), 0, n_pages * (PAGE_SIZE // SEL_BLOCK) - 1)
    pages = jnp.take_along_axis(page_table, past_blk // (PAGE_SIZE // SEL_BLOCK), axis=1)
    one = lambda w: pl.BlockSpec((1, 1, w), lambda i, a, p: (i, 0, 0))
    win_spec = pl.BlockSpec((1, 2, N_KV_HEADS, HEAD_DIM, n_win), lambda i, a, p: (i, 0, 0, 0, 0))
    grid_spec = pltpu.PrefetchScalarGridSpec(
        num_scalar_prefetch=2,
        grid=(n_seq,),
        in_specs=[pl.BlockSpec(memory_space=pl.ANY), one(NSA_WIDTH), one(GATE_PAD), one(NSA_WIDTH), one(KV_WIDTH),
                  one(KV_WIDTH), win_spec],
        out_specs=[one(NSA_WIDTH), win_spec],
        scratch_shapes=[pltpu.VMEM((2, n_blk, 2, HEAD_DIM, PAGE_SIZE), F32), pltpu.SemaphoreType.DMA((2,))],
    )
    return pl.pallas_call(
        functools.partial(_attn_sample_body, n_pages=n_pages, n_seq=n_seq, k_sel=k_sel),
        grid_spec=grid_spec,
        out_shape=[jax.ShapeDtypeStruct((n_seq, 1, NSA_WIDTH), BF16), jax.ShapeDtypeStruct(win_t.shape, F32)],
        compiler_params=_cparams("arbitrary"),
        name="attn_sample",
    )(idx, pages.reshape(-1), pool_t, qb, gl, o_c, kvs_new, kvw_new, win_t)


def _tail_sample_body(x_ref, o_ref, u_ref, h0r_ref, h0i_ref, hist2_ref, hist1_ref, bb_ref, cc_ref, d_ref, abr_ref,
                      abi_ref, wglu_ref, bglu_ref, wout_ref, nf_ref, wup_ref, cw_ref, cb_ref, wdown_ref, nfin_ref,
                      out_ref, hr_ref, hi_ref, cs_ref):
    u = u_ref[...]
    bu = _split_dot(u, bb_ref[...])
    ar, ai = abr_ref[...], abi_ref[...]
    h0r, h0i = h0r_ref[...], h0i_ref[...]
    hr = ar * h0r - ai * h0i + bu[:, :SSM_N]
    hi = ar * h0i + ai * h0r + bu[:, SSM_N:]
    hr_ref[...] = hr
    hi_ref[...] = hi
    y = _dot(hr.astype(BF16), cc_ref[:SSM_N, :]) + _dot(hi.astype(BF16), cc_ref[SSM_N:, :]) + d_ref[...] * u
    x1 = _mix_out(x_ref[...], o_ref[...], y, wglu_ref, bglu_ref, wout_ref)
    hn = _rms(x1, nf_ref[...]).astype(BF16)

    def prev_rows(lo, hi_col, hu):
        cs_ref[:, lo:hi_col] = hist1_ref[:, lo:hi_col]
        cs_ref[:, 2 * D_FF + lo:2 * D_FF + hi_col] = hu
        return hist2_ref[:, lo:hi_col], hist1_ref[:, lo:hi_col]

    x2 = x1 + _ffn_chunks(hn, wup_ref, cw_ref, cb_ref, wdown_ref, prev_rows)
    out_ref[...] = _rms(x2, nfin_ref[...])


def _tail_sample(x, o_nsa, u, h0r, h0i, hist2, hist1, bb, cc, d, ab_re, ab_im, w):
    n = x.shape[0]
    sds = lambda width: jax.ShapeDtypeStruct((n, width), F32)
    return pl.pallas_call(
        _tail_sample_body,
        out_shape=[sds(D_MODEL), sds(SSM_N), sds(SSM_N), sds((CONV_W - 1) * 2 * D_FF)],
        compiler_params=pltpu.CompilerParams(vmem_limit_bytes=VMEM_LIMIT),
        name="tail_sample",
    )(x, o_nsa, u, h0r, h0i, hist2, hist1, bb, cc, d, ab_re, ab_im, w["w_glu"], w["b_glu"], w["w_out"],
      w["norm_ffn"], w["w_up"], w["conv_w"], w["conv_b"], w["w_down"], w["norm_final"])


def _pad_w_in(w_in):
    c = NSA_WIDTH + 3 * KV_WIDTH
    return jnp.concatenate([w_in[:, :c], w_in[:, c + N_GATES:], w_in[:, c:c + N_GATES],
                            jnp.zeros((D_MODEL, GATE_PAD - N_GATES), w_in.dtype)], axis=1).astype(BF16)


def _cmp_weight_pos(w_cmp):
    w = w_cmp.reshape(2, CMP_LEN // CMP_STRIDE, CMP_STRIDE, HEAD_DIM, HEAD_DIM)
    eye_h = jnp.eye(N_KV_HEADS, dtype=w.dtype)
    big = (w.transpose(0, 2, 3, 1, 4)[:, :, None, :, :, None, :]
           * eye_h[None, None, :, None, None, :, None])
    half = N_KV_HEADS * HEAD_DIM
    return big.reshape(2, CMP_STRIDE * half, 2 * half).astype(BF16)


def _pe_sub(pe_cmp):
    rows = jnp.broadcast_to(pe_cmp.transpose(1, 0, 2)[:, :, None, :], (CMP_LEN, 2, N_KV_HEADS, HEAD_DIM))
    sub = rows.reshape(CMP_LEN // CMP_STRIDE, SUB_W)
    return jnp.concatenate([sub, jnp.zeros((8 - sub.shape[0], SUB_W), sub.dtype)], axis=0)


def _sel_map(n_cmp, n_sel, n_sel_pad):
    c0 = (jnp.arange(n_cmp) * CMP_STRIDE)[:, None]
    s0 = (jnp.arange(n_sel_pad) * SEL_BLOCK)[None, :]
    hit = (c0 < s0 + SEL_BLOCK) & (c0 + CMP_LEN > s0) & (jnp.arange(n_sel_pad)[None, :] < n_sel)
    return hit.astype(BF16)


def _expand_map(n_sel, t, kc_len):
    hit = jnp.arange(t)[None, :] // SEL_BLOCK == jnp.arange(n_sel)[:, None]
    return hit.astype(BF16).reshape(n_sel, t // kc_len, kc_len).transpose(1, 0, 2)


def kernel(x_prompt, x_sample, cache_kv_cmp, cache_kv_sel, cache_kv_win, state_ssm_re, state_ssm_im, state_ffn_conv, page_table, norm_mix, w_in, pe_cmp, w_cmp, ssm_a_re, ssm_a_im, ssm_log_dt, ssm_b_re, ssm_b_im, ssm_c_re, ssm_c_im, ssm_d, w_glu, b_glu, w_out, norm_ffn, w_up, conv_w, conv_b, w_down, norm_final):
    depth = w_in.shape[0]
    assert depth == 1, "single-layer trunk"
    b, t, _ = x_prompt.shape
    bd, s, _ = x_sample.shape
    assert s == 1, "one new position per sample sequence"
    n_pages = page_table.shape[1]
    l = 0
    w_pad = _pad_w_in(w_in[l])
    g_mix = norm_mix[l].reshape(1, D_MODEL)
    w_pos = _cmp_weight_pos(w_cmp[l])
    pe_sub = _pe_sub(pe_cmp[l])
    ab_re, ab_im, bb_re, bb_im = _ssm_params(ssm_log_dt[l], ssm_a_re[l], ssm_a_im[l], ssm_b_re[l], ssm_b_im[l])
    bb = jnp.concatenate([_block_diag_in(bb_re), _block_diag_in(bb_im)], axis=1).astype(BF16)
    cc = jnp.concatenate([_block_diag_out(ssm_c_re[l]), -_block_diag_out(ssm_c_im[l])], axis=0).astype(BF16)
    d_row = ssm_d[l].reshape(1, SSM_WIDTH)
    n_blk = SSM_WIDTH // LANES
    ch, st = LANES, SSM_N // n_blk
    bb_blk = jnp.stack([jnp.concatenate([bb[j * ch:(j + 1) * ch, j * st:(j + 1) * st],
                                         bb[j * ch:(j + 1) * ch, SSM_N + j * st:SSM_N + (j + 1) * st]], axis=1)
                        for j in range(n_blk)])
    cc_blk = jnp.stack([jnp.concatenate([cc[j * st:(j + 1) * st, j * ch:(j + 1) * ch],
                                         cc[SSM_N + j * st:SSM_N + (j + 1) * st, j * ch:(j + 1) * ch]], axis=0)
                        for j in range(n_blk)])
    tail_w = {"w_glu": w_glu[l].astype(BF16), "b_glu": b_glu[l].reshape(1, -1), "w_out": w_out[l].astype(BF16),
              "norm_ffn": norm_ffn[l].reshape(1, -1), "w_up": w_up[l].astype(BF16), "conv_w": conv_w[l],
              "conv_b": conv_b[l].reshape(1, -1), "w_down": w_down[l].astype(BF16),
              "norm_final": norm_final.reshape(1, -1)}

    kvc, _, _, kvct, kvst, kvwt, kstb, kwtb, gl, u, qb = _inproj(x_prompt, g_mix, w_pad, 512)
    n_sub = t // CMP_STRIDE
    kc, cmp_bias = _cmp_prompt(kvc.reshape(b, n_sub, SUB_W), pe_sub, w_pos)
    n_sel = t // SEL_BLOCK
    o_nsa = _attn_prompt(qb, gl, kc, kstb, kwtb, _sel_map(n_sub, n_sel, n_sel).T)
    y_ssm, p_hr, p_hi = _ssm_prompt(u, bb_blk, cc_blk, d_row, ab_re, ab_im)
    y_prompt, p_conv = _tail_prompt(x_prompt, o_nsa, y_ssm, tail_w)
    win_keep = min(WINDOW, t)
    kv_rows = lambda a: a.reshape(a.shape[0], 2, N_KV_HEADS, HEAD_DIM, a.shape[2]).transpose(0, 4, 1, 2, 3)[None]
    st_shape = (depth, b, SSM_GROUPS, SSM_STATE)

    kvc_n, kvs_n, kvw_n, kvct_n, kvst_n, _, _, _, gl_n, u_n, qb_n = _inproj(x_sample.reshape(1, bd, D_MODEL), g_mix, w_pad, bd)
    per_seq = lambda a: a.reshape(bd, 1, a.shape[-1])
    n_sub_s = n_pages * (PAGE_SIZE // CMP_STRIDE)
    n_sel_s = n_pages * (PAGE_SIZE // SEL_BLOCK) + 1
    n_sel_pad = -(-n_sel_s // LANES) * LANES
    k_sel = min(N_SELECT, n_sel_s)
    pos_minor = lambda a: a.transpose(0, 2, 3, 4, 1)
    o_c, picked = _cmp_sample(page_table, pos_minor(cache_kv_cmp[l]), per_seq(kvc_n), per_seq(qb_n), cmp_bias,
                              w_pos, _sel_map(n_sub_s, n_sel_s, n_sel_pad))
    idx = picked[:, :N_KV_HEADS, :k_sel].reshape(-1)
    n_buf = cache_kv_win.shape[2]
    assert n_buf == WINDOW, "window buffer holds exactly WINDOW rows"
    o_nsa_s, s_win = _attn_sample(idx, page_table, pos_minor(cache_kv_sel[l]), per_seq(qb_n), per_seq(gl_n), o_c,
                                  per_seq(kvs_n), per_seq(kvw_n), pos_minor(cache_kv_win[l]), k_sel)
    hist = state_ffn_conv[l]
    y_sample, s_hr, s_hi, s_conv = _tail_sample(
        x_sample.reshape(bd, D_MODEL), o_nsa_s.reshape(bd, -1), u_n.reshape(bd, -1),
        state_ssm_re[l].reshape(bd, SSM_N), state_ssm_im[l].reshape(bd, SSM_N), hist[:, 0], hist[:, 1], bb, cc, d_row,
        ab_re, ab_im, tail_w)
    kv_rows_s = lambda a: kv_rows(a).reshape(depth, bd, s, 2, N_KV_HEADS, HEAD_DIM)
    st_shape_s = (depth, bd, SSM_GROUPS, SSM_STATE)
    return (y_prompt, y_sample.reshape(bd, s, D_MODEL),
            kv_rows(kvct), kv_rows(kvst), kv_rows(kvwt[:, :, t - win_keep:]),
            p_hr.reshape(st_shape), p_hi.reshape(st_shape), p_conv.reshape(depth, b, CONV_W - 1, 2 * D_FF),
            kv_rows_s(kvct_n), kv_rows_s(kvst_n),
            s_win.transpose(0, 4, 1, 2, 3)[None],
            s_hr.reshape(st_shape_s), s_hi.reshape(st_shape_s), s_conv.reshape(depth, bd, CONV_W - 1, 2 * D_FF))
```

```python
import functools
import math

import jax
import jax.numpy as jnp
from jax import lax
from jax.experimental import pallas as pl
from jax.experimental.pallas import tpu as pltpu

D_MODEL = 1024
N_HEADS = 8
N_KV_HEADS = 2
GQA = N_HEADS // N_KV_HEADS
HEAD_DIM = 64
NSA_WIDTH = N_HEADS * HEAD_DIM
KV_WIDTH = 2 * N_KV_HEADS * HEAD_DIM
N_GATES = 3 * N_HEADS
CMP_LEN = 32
CMP_STRIDE = 16
SEL_BLOCK = 64
N_SELECT = 16
WINDOW = 512
Q_BLOCK = 128
PAGE_SIZE = 128
SSM_WIDTH = D_MODEL - NSA_WIDTH
SSM_GROUP = 16
SSM_GROUPS = SSM_WIDTH // SSM_GROUP
SSM_STATE = 64
SSM_N = SSM_GROUPS * SSM_STATE
D_FF = (D_MODEL * 11 // 4 + 127) // 128 * 128
CONV_W = 3
EPS = 1e-6
NEG = -1e30
BIG = 1e30
BELOW_NEG = -3e38

LANES = 128
GATE_PAD = LANES
IN_PAD = NSA_WIDTH + 3 * KV_WIDTH + SSM_WIDTH + GATE_PAD
SUB_W = CMP_STRIDE * KV_WIDTH
VMEM_LIMIT = 56 * 1024 * 1024

F32 = jnp.float32
BF16 = jnp.bfloat16


def _nt_dot(a, b):
    return lax.dot_general(a, b, (((1,), (1,)), ((), ())), preferred_element_type=F32)


def _dot(a, b):
    return jnp.dot(a, b, preferred_element_type=F32)


def _sigmoid(x):
    return 1.0 / (1.0 + jnp.exp(-x))


def _gelu_tanh(x):
    return 0.5 * x * (1.0 + jnp.tanh(math.sqrt(2.0 / math.pi) * (x + 0.044715 * (x * x * x))))


def _rms(x, g):
    return x * lax.rsqrt(jnp.mean(x * x, axis=-1, keepdims=True) + EPS) * g


def _split_dot(x, w_bf):
    hi = x.astype(BF16)
    lo = (x - hi.astype(F32)).astype(BF16)
    return _dot(hi, w_bf) + _dot(lo, w_bf)


def _cparams(*sem):
    return pltpu.CompilerParams(dimension_semantics=sem, vmem_limit_bytes=VMEM_LIMIT)


def _const_spec(shape):
    nd = len(shape)
    return pl.BlockSpec(shape, lambda *_: (0,) * nd, pipeline_mode=pl.Buffered(1))


def _inproj_body(x_ref, g_ref, w_ref, kvc_ref, kvs_ref, kvw_ref, kvct_ref, kvst_ref, kvwt_ref, kstb_ref, kwtb_ref,
                 gl_ref, u_ref, qb_ref):
    h = _rms(x_ref[0], g_ref[...])
    z = _dot(h.astype(BF16), w_ref[...])
    tm = z.shape[0]
    c = NSA_WIDTH
    qb_ref[0] = (z[:, :c] * (HEAD_DIM ** -0.5)).astype(BF16)
    for rm_ref, t_ref, tb_ref in ((kvc_ref, kvct_ref, None), (kvs_ref, kvst_ref, kstb_ref),
                                  (kvw_ref, kvwt_ref, kwtb_ref)):
        rows = z[:, c:c + KV_WIDTH]
        c += KV_WIDTH
        rm_ref[0] = rows
        cols = rows.T
        t_ref[0] = cols
        if tb_ref is not None:
            cols_bf = cols.astype(BF16)
            for k in range(tm // LANES):
                tb_ref[0, k] = cols_bf[:, k * LANES:(k + 1) * LANES]
    u_ref[0] = z[:, c:c + SSM_WIDTH]
    gl_ref[0] = z[:, c + SSM_WIDTH:]


def _inproj(x, g, w_pad, tm):
    b, t, _ = x.shape
    row = lambda w: pl.BlockSpec((1, tm, w), lambda i, j: (i, j, 0))
    col = pl.BlockSpec((1, KV_WIDTH, tm), lambda i, j: (i, 0, j))
    chunk = pl.BlockSpec((1, tm // LANES, KV_WIDTH, LANES), lambda i, j: (i, j, 0, 0))
    rm_t = jax.ShapeDtypeStruct((b, t, KV_WIDTH), F32)
    col_t = jax.ShapeDtypeStruct((b, KV_WIDTH, t), F32)
    chunk_t = jax.ShapeDtypeStruct((b, t // LANES, KV_WIDTH, LANES), BF16)
    return pl.pallas_call(
        _inproj_body,
        grid=(b, t // tm),
        in_specs=[row(D_MODEL), _const_spec((1, D_MODEL)), _const_spec((D_MODEL, IN_PAD))],
        out_specs=[row(KV_WIDTH)] * 3 + [col] * 3 + [chunk] * 2 + [row(GATE_PAD), row(SSM_WIDTH), row(NSA_WIDTH)],
        out_shape=[rm_t] * 3 + [col_t] * 3 + [chunk_t] * 2
        + [jax.ShapeDtypeStruct((b, t, GATE_PAD), F32), jax.ShapeDtypeStruct((b, t, SSM_WIDTH), F32),
           jax.ShapeDtypeStruct((b, t, NSA_WIDTH), BF16)],
        compiler_params=_cparams("parallel", "parallel"),
        name="in_proj",
    )(x, g, w_pad)


def _cmp_prompt_body(sub_ref, pe_ref, w_ref, kc_ref, bias_ref):
    sub, pe = sub_ref[0], pe_ref[...]
    n_sub = sub.shape[0]
    half = N_KV_HEADS * HEAD_DIM
    tokens, biases = [], []
    for x in range(2):
        pick = lambda a: jnp.concatenate([a[:, s * KV_WIDTH + x * half:s * KV_WIDTH + (x + 1) * half]
                                          for s in range(CMP_STRIDE)], axis=1).astype(BF16)
        parts = _dot(pick(sub), w_ref[x])
        pe_parts = _dot(pick(pe), w_ref[x])
        bias = pe_parts[0:1, :half] + pe_parts[1:2, half:]
        nxt = pltpu.roll(parts[:, half:], n_sub - 1, 0)
        tokens.append(parts[:, :half] + nxt + bias)
        biases.append(bias)
    kc_ref[0] = jnp.concatenate(tokens, axis=1).astype(BF16)
    bias_ref[...] = jnp.broadcast_to(jnp.concatenate(biases, axis=1), bias_ref.shape)


def _cmp_prompt(sub, pe_sub, w_pos):
    b, n_sub, _ = sub.shape
    return pl.pallas_call(
        _cmp_prompt_body,
        grid=(b,),
        in_specs=[pl.BlockSpec((1, n_sub, SUB_W), lambda i: (i, 0, 0)), _const_spec(pe_sub.shape),
                  _const_spec(w_pos.shape)],
        out_specs=[pl.BlockSpec((1, n_sub, KV_WIDTH), lambda i: (i, 0, 0)),
                   pl.BlockSpec((8, KV_WIDTH), lambda i: (0, 0))],
        out_shape=[jax.ShapeDtypeStruct((b, n_sub, KV_WIDTH), BF16), jax.ShapeDtypeStruct((8, KV_WIDTH), F32)],
        compiler_params=_cparams("arbitrary"),
        name="cmp_prompt",
    )(sub, pe_sub, w_pos)


def _stack_heads(q, h):
    return jnp.concatenate([q[:, (GQA * h + g) * HEAD_DIM:(GQA * h + g + 1) * HEAD_DIM] for g in range(GQA)], axis=0)


def _stack_heads_single(q, h):
    row = lax.broadcasted_iota(jnp.int32, (8, 1), 0)
    q8 = jnp.broadcast_to(q, (8, NSA_WIDTH))
    out = jnp.zeros((8, HEAD_DIM), F32)
    for g in range(GQA):
        lo = (GQA * h + g) * HEAD_DIM
        out = out + jnp.where(row == g, q8[:, lo:lo + HEAD_DIM], 0.0)
    return out.astype(BF16)


def _masked_softmax_rows(s, mask):
    sm = jnp.where(mask, s, NEG)
    m = jnp.max(sm, axis=-1, keepdims=True)
    p = jnp.where(mask, jnp.exp(sm - m), 0.0)
    l = jnp.sum(p, axis=-1, keepdims=True)
    return p * (1.0 / jnp.where(l > 0.0, l, 1.0))


def _block_scores(imp, blk, qpos):
    first = blk * SEL_BLOCK
    own_or_imp = jnp.where(first + SEL_BLOCK > qpos, BIG, imp)
    return jnp.where(blk == 0, BIG, jnp.where(first <= qpos, own_or_imp, NEG))


def _masked_softmax_cols(s, mask):
    sm = jnp.where(mask, s, NEG)
    m = jnp.max(sm, axis=0, keepdims=True)
    p = jnp.where(mask, jnp.exp(sm - m), 0.0)
    l = jnp.sum(p, axis=0, keepdims=True)
    return p * (1.0 / jnp.where(l > 0.0, l, 1.0))


def _top_k_mask_cols(score, blk, k):
    n = float(score.shape[0])
    sel = jnp.zeros(score.shape, F32)
    for _ in range(k):
        mx = jnp.max(score, axis=0, keepdims=True)
        idx = jnp.min(jnp.where(score == mx, blk, n), axis=0, keepdims=True)
        hit = blk == idx
        sel = sel + jnp.where(hit, jnp.where(mx > 0.5 * NEG, 1.0, 0.0), 0.0)
        score = jnp.where(hit, BELOW_NEG, score)
    return sel


KEY_BLOCKS = 8
ROW_BLOCK = 32


def _attn_prompt_body(q_ref, gl_ref, kc_ref, kst_ref, kwt_ref, selmap_ref, exp_ref, o_ref, m_s, l_s, a_s, acc_s, s_s,
                      p_s):
    qb_idx = pl.program_id(1)
    start = qb_idx * Q_BLOCK
    kc_len = KEY_BLOCKS * LANES
    q = q_ref[0]
    gate = _sigmoid(gl_ref[0])
    qpos = start + lax.broadcasted_iota(jnp.int32, (Q_BLOCK, 1), 0)
    qpos_l = start + lax.broadcasted_iota(jnp.int32, (1, Q_BLOCK), 1)
    qpos4_l = jnp.concatenate([qpos_l] * GQA, axis=1)
    n_cmp = kc_ref.shape[1]
    n_sel = selmap_ref.shape[0]
    cmp_end = lax.broadcasted_iota(jnp.int32, (n_cmp, 1), 0) * CMP_STRIDE + (CMP_LEN - 1)
    blk = lax.broadcasted_iota(jnp.int32, (n_sel, 1), 0)
    qs, o_c, scores = [], [], []
    for h in range(N_KV_HEADS):
        k_lo, v_lo = h * HEAD_DIM, (N_KV_HEADS + h) * HEAD_DIM
        qs.append(_stack_heads(q, h))
        p_t = _masked_softmax_cols(_nt_dot(kc_ref[0, :, k_lo:k_lo + HEAD_DIM], qs[h]), cmp_end <= qpos4_l)
        o_c.append(_dot(p_t.T.astype(BF16), kc_ref[0, :, v_lo:v_lo + HEAD_DIM]))
        p_sum = p_t[:, 0:Q_BLOCK]
        for g in range(1, GQA):
            p_sum = p_sum + p_t[:, g * Q_BLOCK:(g + 1) * Q_BLOCK]
        hi = p_sum.astype(BF16)
        lo = (p_sum - hi.astype(F32)).astype(BF16)
        imp = _dot(selmap_ref[...], hi) + _dot(selmap_ref[...], lo)
        scores.append(_block_scores(imp, blk, qpos_l))
    sel_t = _top_k_mask_cols(jnp.concatenate(scores, axis=1), blk.astype(F32), min(N_SELECT, n_sel))
    q_ext = []
    for h in range(N_KV_HEADS):
        picked = sel_t[:, h * Q_BLOCK:(h + 1) * Q_BLOCK].T
        neg = jnp.where(picked > 0.5, 0.0, NEG).astype(BF16)
        q_ext.append(jnp.concatenate([qs[h], jnp.concatenate([neg] * GQA, axis=0)], axis=1))
    m_s[...] = jnp.full(m_s.shape, NEG, F32)
    l_s[...] = jnp.zeros(l_s.shape, F32)
    acc_s[...] = jnp.zeros(acc_s.shape, F32)
    n_chunks = (start + Q_BLOCK + kc_len - 1) // kc_len

    def sel_chunk(c, causal):
        for h in range(N_KV_HEADS):
            k_lo, v_lo = h * HEAD_DIM, (N_KV_HEADS + h) * HEAD_DIM
            k_t = jnp.concatenate([kst_ref[0, c * KEY_BLOCKS + j, k_lo:k_lo + HEAD_DIM, :]
                                   for j in range(KEY_BLOCKS)], axis=1)
            v_t = jnp.concatenate([kst_ref[0, c * KEY_BLOCKS + j, v_lo:v_lo + HEAD_DIM, :]
                                   for j in range(KEY_BLOCKS)], axis=1)
            s = _dot(q_ext[h], jnp.concatenate([k_t, exp_ref[c]], axis=0))
            s_s[h] = s if causal is None else s + causal
            for r0 in range(0, GQA * Q_BLOCK, ROW_BLOCK):
                rows = slice(r0, r0 + ROW_BLOCK)
                m_prev = m_s[h, rows, :]
                m_next = jnp.maximum(m_prev, jnp.max(s_s[h, rows, :], axis=-1, keepdims=True))
                a_s[h, rows, :] = jnp.exp(m_prev - m_next)
                m_s[h, rows, :] = m_next
            for r0 in range(0, GQA * Q_BLOCK, ROW_BLOCK):
                rows = slice(r0, r0 + ROW_BLOCK)
                p = jnp.exp(s_s[h, rows, :] - jnp.tile(m_s[h, rows, :], (1, KEY_BLOCKS)))
                l_s[h, rows, :] = a_s[h, rows, :] * l_s[h, rows, :] + jnp.sum(p, axis=-1, keepdims=True)
                p_s[h, rows, :] = p.astype(BF16)
            acc_s[h] = acc_s[h] * a_s[h, :, :HEAD_DIM] + _nt_dot(p_s[h], v_t)

    kpos = (n_chunks - 1) * kc_len + lax.broadcasted_iota(jnp.int32, (1, kc_len), 1)
    sel_chunk(n_chunks - 1, jnp.concatenate([jnp.where(kpos <= qpos, 0.0, NEG)] * GQA, axis=0))

    def earlier_chunk(it, carry):
        sel_chunk(n_chunks - 1 - it, None)
        return carry

    lax.fori_loop(1, n_chunks, earlier_chunk, 0)
    wpos = start - WINDOW + lax.broadcasted_iota(jnp.int32, (1, WINDOW + Q_BLOCK), 1)
    visible = jnp.where(wpos <= qpos, jnp.where(wpos >= jnp.maximum(qpos - WINDOW, 0), 0.0, NEG), NEG)
    win_bias = jnp.concatenate([visible] * GQA, axis=0)
    head_out = []
    for h in range(N_KV_HEADS):
        k_lo, v_lo = h * HEAD_DIM, (N_KV_HEADS + h) * HEAD_DIM
        o_s = acc_s[h] * (1.0 / l_s[h])[:, :HEAD_DIM]
        kw, vw = [], []
        for c in range((WINDOW + Q_BLOCK) // LANES):
            src = jnp.maximum(qb_idx - WINDOW // LANES + c, 0)
            kw.append(kwt_ref[0, src, k_lo:k_lo + HEAD_DIM, :])
            vw.append(kwt_ref[0, src, v_lo:v_lo + HEAD_DIM, :])
        s = _dot(qs[h], jnp.concatenate(kw, axis=1))
        sm = s + win_bias
        p = jnp.exp(sm - jnp.max(sm, axis=-1, keepdims=True))
        o_w = _nt_dot(p.astype(BF16), jnp.concatenate(vw, axis=1)) * (1.0 / jnp.sum(p, axis=-1, keepdims=True))
        for g in range(GQA):
            col = (GQA * h + g) * 3
            r = slice(g * Q_BLOCK, (g + 1) * Q_BLOCK)
            head_out.append(gate[:, col:col + 1] * o_c[h][r] + gate[:, col + 1:col + 2] * o_s[r]
                            + gate[:, col + 2:col + 3] * o_w[r])
    o_ref[0] = jnp.concatenate(head_out, axis=-1).astype(BF16)


def _attn_prompt(qb, gl, kc, kst, kwt, selmap_t):
    b, t, _ = qb.shape
    kc_len = KEY_BLOCKS * LANES
    expand = _expand_map(selmap_t.shape[0], t, kc_len)
    rows = GQA * Q_BLOCK
    whole = lambda a: pl.BlockSpec((1,) + a.shape[1:], lambda i, j: (i,) + (0,) * (a.ndim - 1))
    return pl.pallas_call(
        _attn_prompt_body,
        grid=(b, t // Q_BLOCK),
        in_specs=[pl.BlockSpec((1, Q_BLOCK, NSA_WIDTH), lambda i, j: (i, j, 0)),
                  pl.BlockSpec((1, Q_BLOCK, GATE_PAD), lambda i, j: (i, j, 0)),
                  whole(kc), whole(kst), whole(kwt), _const_spec(selmap_t.shape), _const_spec(expand.shape)],
        out_specs=pl.BlockSpec((1, Q_BLOCK, NSA_WIDTH), lambda i, j: (i, j, 0)),
        out_shape=jax.ShapeDtypeStruct((b, t, NSA_WIDTH), BF16),
        scratch_shapes=[pltpu.VMEM((N_KV_HEADS, rows, LANES), F32), pltpu.VMEM((N_KV_HEADS, rows, LANES), F32),
                        pltpu.VMEM((N_KV_HEADS, rows, LANES), F32), pltpu.VMEM((N_KV_HEADS, rows, HEAD_DIM), F32),
                        pltpu.VMEM((N_KV_HEADS, rows, kc_len), F32), pltpu.VMEM((N_KV_HEADS, rows, kc_len), BF16)],
        compiler_params=_cparams("parallel", "arbitrary"),
        name="attn_prompt",
    )(qb, gl, kc, kst, kwt, selmap_t, expand)


def _ssm_param_body(ldt_ref, are_ref, aim_ref, bre_ref, bim_ref, abr_ref, abi_ref, bbr_ref, bbi_ref):
    dt = jnp.exp(ldt_ref[...])
    are, aim = are_ref[...], aim_ref[...]
    mag = jnp.exp(dt * are)
    ab_re, ab_im = mag * jnp.cos(dt * aim), mag * jnp.sin(dt * aim)
    den = are * are + aim * aim
    zr, zi = ab_re - 1.0, ab_im
    f_re = (zr * are + zi * aim) / den
    f_im = (zi * are - zr * aim) / den
    abr_ref[...] = ab_re
    abi_ref[...] = ab_im
    bbr_ref[...] = f_re * bre_ref[...] - f_im * bim_ref[...]
    bbi_ref[...] = f_re * bim_ref[...] + f_im * bre_ref[...]


def _ssm_params(log_dt, a_re, a_im, b_re, b_im):
    col = lambda a: a.reshape(SSM_N, 1)
    ldt = col(jnp.broadcast_to(log_dt[:, None], (SSM_GROUPS, SSM_STATE)))
    col_t = jax.ShapeDtypeStruct((SSM_N, 1), F32)
    mat_t = jax.ShapeDtypeStruct((SSM_N, SSM_GROUP), F32)
    ab_re, ab_im, bb_re, bb_im = pl.pallas_call(
        _ssm_param_body, out_shape=[col_t, col_t, mat_t, mat_t], name="ssm_params",
    )(ldt, col(a_re), col(a_im), b_re.reshape(SSM_N, SSM_GROUP), b_im.reshape(SSM_N, SSM_GROUP))
    return ab_re.reshape(1, SSM_N), ab_im.reshape(1, SSM_N), bb_re, bb_im


def _block_diag_in(bb):
    m = bb.reshape(SSM_GROUPS, SSM_STATE, SSM_GROUP).transpose(0, 2, 1)
    eye = jnp.eye(SSM_GROUPS, dtype=bb.dtype)
    return (eye[:, None, :, None] * m[:, :, None, :]).reshape(SSM_WIDTH, SSM_N)


def _block_diag_out(c):
    m = c.transpose(0, 2, 1)
    eye = jnp.eye(SSM_GROUPS, dtype=c.dtype)
    return (eye[:, None, :, None] * m[:, :, None, :]).reshape(SSM_N, SSM_WIDTH)


def _ssm_prompt_body(u_ref, bb_ref, cc_ref, d_ref, abr_ref, abi_ref, y_ref, hr_ref, hi_ref, bu_s, hs_s, st_s):
    tc = u_ref.shape[1]

    @pl.when(pl.program_id(1) == 0)
    def _():
        st_s[...] = jnp.zeros(st_s.shape, F32)

    u = u_ref[0]
    u_bf = u.astype(BF16)
    n_blocks = bb_ref.shape[0]
    ch, st = SSM_WIDTH // n_blocks, SSM_N // n_blocks
    for j in range(n_blocks):
        z = _dot(u_bf[:, j * ch:(j + 1) * ch], bb_ref[j])
        bu_s[:, j * st:(j + 1) * st] = z[:, :st]
        bu_s[:, SSM_N + j * st:SSM_N + (j + 1) * st] = z[:, st:]
    ar, ai = abr_ref[...], abi_ref[...]

    def step(t, carry):
        hr, hi = carry
        nr = ar * hr - ai * hi + bu_s[pl.ds(t, 1), :SSM_N]
        ni = ar * hi + ai * hr + bu_s[pl.ds(t, 1), SSM_N:]
        hs_s[pl.ds(t, 1), :SSM_N] = nr
        hs_s[pl.ds(t, 1), SSM_N:] = ni
        return nr, ni

    hr, hi = lax.fori_loop(0, tc, step, (st_s[0:1, :], st_s[1:2, :]), unroll=8)
    st_s[0:1, :] = hr
    st_s[1:2, :] = hi
    ys = []
    for j in range(n_blocks):
        h_blk = jnp.concatenate([hs_s[:, j * st:(j + 1) * st], hs_s[:, SSM_N + j * st:SSM_N + (j + 1) * st]], axis=1)
        ys.append(_dot(h_blk.astype(BF16), cc_ref[j]))
    y_ref[0] = jnp.concatenate(ys, axis=1) + d_ref[...] * u
    hr_ref[0] = hr
    hi_ref[0] = hi


def _ssm_prompt(u, bb, cc, d, ab_re, ab_im, tc=256):
    b, t, _ = u.shape
    st = jax.ShapeDtypeStruct((b, 1, SSM_N), F32)
    st_spec = pl.BlockSpec((1, 1, SSM_N), lambda i, j: (i, 0, 0))
    return pl.pallas_call(
        _ssm_prompt_body,
        grid=(b, t // tc),
        in_specs=[pl.BlockSpec((1, tc, SSM_WIDTH), lambda i, j: (i, j, 0)), _const_spec(bb.shape),
                  _const_spec(cc.shape), _const_spec(d.shape), _const_spec(ab_re.shape), _const_spec(ab_im.shape)],
        out_specs=[pl.BlockSpec((1, tc, SSM_WIDTH), lambda i, j: (i, j, 0)), st_spec, st_spec],
        out_shape=[jax.ShapeDtypeStruct((b, t, SSM_WIDTH), F32), st, st],
        scratch_shapes=[pltpu.VMEM((tc, 2 * SSM_N), F32), pltpu.VMEM((tc, 2 * SSM_N), F32),
                        pltpu.VMEM((8, SSM_N), F32)],
        compiler_params=_cparams("parallel", "arbitrary"),
        name="ssm_prompt",
    )(u, bb, cc, d, ab_re, ab_im)


FF_CHUNK = 256


def _mix_out(x, o_nsa, y_ssm, wglu_ref, bglu_ref, wout_ref):
    z = _dot(_gelu_tanh(y_ssm).astype(BF16), wglu_ref[...]) + bglu_ref[...]
    glu = z[:, :SSM_WIDTH] * _sigmoid(z[:, SSM_WIDTH:])
    return x + _dot(o_nsa, wout_ref[:NSA_WIDTH, :]) + _dot(glu.astype(BF16), wout_ref[NSA_WIDTH:, :])


def _ffn_chunks(hn, wup_ref, cw_ref, cb_ref, wdown_ref, prev_rows):
    acts = []
    for j in range(D_FF // FF_CHUNK):
        conv = []
        for base in (0, D_FF):
            lo = base + j * FF_CHUNK
            hi = lo + FF_CHUNK
            hu = _dot(hn, wup_ref[:, lo:hi])
            hu2, hu1 = prev_rows(lo, hi, hu)
            conv.append(cw_ref[0:1, lo:hi] * hu2 + cw_ref[1:2, lo:hi] * hu1 + cw_ref[2:3, lo:hi] * hu
                        + cb_ref[:, lo:hi])
        a, g = conv
        acts.append((a * _sigmoid(a) * g).astype(BF16))
    return _dot(jnp.concatenate(acts, axis=1), wdown_ref[...])


def _tail_prompt_body(x_ref, o_ref, y_ref, wglu_ref, bglu_ref, wout_ref, nf_ref, wup_ref, cw_ref, cb_ref, wdown_ref,
                      nfin_ref, out_ref, cs_ref, prev_s):
    tm = x_ref.shape[1]

    @pl.when(pl.program_id(1) == 0)
    def _():
        prev_s[...] = jnp.zeros(prev_s.shape, F32)

    x1 = _mix_out(x_ref[0], o_ref[0], y_ref[0], wglu_ref, bglu_ref, wout_ref)
    hn = _rms(x1, nf_ref[...]).astype(BF16)
    row = lax.broadcasted_iota(jnp.int32, (8, 1), 0)

    def prev_rows(lo, hi, hu):
        p2, p1 = prev_s[6:7, lo:hi], prev_s[7:8, lo:hi]
        r1, r2 = pltpu.roll(hu, 1, 0), pltpu.roll(hu, 2, 0)
        hu1 = jnp.concatenate([jnp.where(row == 0, p1, r1[:8]), r1[8:]], axis=0)
        hu2 = jnp.concatenate([jnp.where(row == 0, p2, jnp.where(row == 1, p1, r2[:8])), r2[8:]], axis=0)
        prev_s[:, lo:hi] = hu[tm - 8:, :]
        cs_ref[0, :, lo:hi] = hu[tm - (CONV_W - 1):, :]
        return hu2, hu1

    x2 = x1 + _ffn_chunks(hn, wup_ref, cw_ref, cb_ref, wdown_ref, prev_rows)
    out_ref[0] = _rms(x2, nfin_ref[...])


def _tail_prompt(x, o_nsa, y_ssm, w, tm=512):
    b, t, _ = x.shape
    tile = lambda width: pl.BlockSpec((1, tm, width), lambda i, j: (i, j, 0))
    consts = [w["w_glu"], w["b_glu"], w["w_out"], w["norm_ffn"], w["w_up"], w["conv_w"], w["conv_b"], w["w_down"],
              w["norm_final"]]
    return pl.pallas_call(
        _tail_prompt_body,
        grid=(b, t // tm),
        in_specs=[tile(D_MODEL), tile(NSA_WIDTH), tile(SSM_WIDTH)] + [_const_spec(c.shape) for c in consts],
        out_specs=[tile(D_MODEL), pl.BlockSpec((1, CONV_W - 1, 2 * D_FF), lambda i, j: (i, 0, 0))],
        out_shape=[jax.ShapeDtypeStruct((b, t, D_MODEL), F32), jax.ShapeDtypeStruct((b, CONV_W - 1, 2 * D_FF), F32)],
        scratch_shapes=[pltpu.VMEM((8, 2 * D_FF), F32)],
        compiler_params=_cparams("parallel", "arbitrary"),
        name="tail_prompt",
    )(x, o_nsa, y_ssm, *consts)


def _pair_rows(q8, lane):
    row = lax.broadcasted_iota(jnp.int32, (8, 1), 0)
    out = jnp.zeros((8, LANES), F32)
    for r in range(N_HEADS):
        pair = q8[:, (r // 2) * LANES:(r // 2 + 1) * LANES]
        want_hi = r // GQA
        if r % 2 != want_hi:
            pair = pltpu.roll(pair, HEAD_DIM, 1)
        keep = (lane >= HEAD_DIM) if want_hi else (lane < HEAD_DIM)
        out = out + jnp.where(row == r, jnp.where(keep, pair, 0.0), 0.0)
    return out


def _unpair_rows(o, lane):
    pieces = []
    for j in range(N_HEADS // 2):
        lo, hi = o[2 * j:2 * j + 1], o[2 * j + 1:2 * j + 2]
        if (2 * j) // GQA == 0:
            hi = pltpu.roll(hi, HEAD_DIM, 1)
        else:
            lo = pltpu.roll(lo, HEAD_DIM, 1)
        pieces.append(jnp.where(lane < HEAD_DIM, lo, hi))
    return jnp.concatenate(pieces, axis=1)


PAIR_GROUP = 4


def _cmp_sample_body(pt_ref, pool_ref, new_ref, q_ref, bias_ref, w_ref, perm_ref, selmap_ref, oc_ref, idx_ref, buf, xs,
                     sem, *,
                     n_pages, n_seq):
    b = pl.program_id(0)
    n_pos = n_pages * PAGE_SIZE
    n_sub = n_pos // CMP_STRIDE
    half = N_KV_HEADS * HEAD_DIM
    sub_per_page = PAGE_SIZE // CMP_STRIDE

    def page_copy(seq, slot, p):
        dst = buf.at[slot, p // 2, :, :, :, pl.ds((p % 2) * PAGE_SIZE, PAGE_SIZE)]
        return pltpu.make_async_copy(pool_ref.at[pt_ref[seq * n_pages + p]], dst, sem.at[slot])

    def fetch(seq, slot):
        for p in range(n_pages):
            page_copy(seq, slot, p).start()

    @pl.when(b == 0)
    def _():
        fetch(0, 0)

    @pl.when(b + 1 < n_seq)
    def _():
        fetch(b + 1, (b + 1) % 2)

    slot = b % 2
    for p in range(n_pages):
        page_copy(b, slot, p).wait()

    row = lax.broadcasted_iota(jnp.int32, (n_sub, 1), 0)
    kv_c = []
    for x in range(2):
        for g0 in range(0, n_pages // 2, PAIR_GROUP):
            pairs = buf[slot, g0:g0 + PAIR_GROUP, x].reshape(PAIR_GROUP * half, 2 * PAGE_SIZE).astype(BF16)
            regrouped = _dot(pairs, perm_ref[...])
            for j in range(PAIR_GROUP):
                for k in range(2):
                    xs[x, 2 * (g0 + j) + k] = regrouped[j * half:(j + 1) * half, k * PAGE_SIZE:(k + 1) * PAGE_SIZE].T
        sub = jnp.concatenate([xs[x, :, s * sub_per_page:(s + 1) * sub_per_page, :].reshape(n_sub, half)
                               for s in range(CMP_STRIDE)], axis=1)
        parts = _dot(sub.astype(BF16), w_ref[x])
        new = jnp.broadcast_to(new_ref[0][:, x * half:(x + 1) * half], (8, half)).astype(BF16)
        new_part = _dot(new, w_ref[x, :half, :])[0:1, half:]
        nxt = jnp.where(row == n_sub - 1, new_part, pltpu.roll(parts[:, half:], n_sub - 1, 0))
        kv_c.append((parts[:, :half] + nxt + bias_ref[0:1, x * half:(x + 1) * half]).astype(BF16))

    q_pos = n_pos
    n_sel = selmap_ref.shape[1]
    n_real = q_pos // SEL_BLOCK + 1
    cmp_end = lax.broadcasted_iota(jnp.int32, (1, n_sub), 1) * CMP_STRIDE + (CMP_LEN - 1)
    blk = lax.broadcasted_iota(jnp.int32, (1, n_sel), 1)
    lane = lax.broadcasted_iota(jnp.int32, (1, LANES), 1)
    row8 = lax.broadcasted_iota(jnp.int32, (8, 1), 0)
    q2 = _pair_rows(jnp.broadcast_to(q_ref[0].astype(F32), (8, NSA_WIDTH)), lane).astype(BF16)
    p_c = _masked_softmax_rows(_nt_dot(q2, kv_c[0]), cmp_end <= q_pos)
    oc_ref[0] = _unpair_rows(_dot(p_c.astype(BF16), kv_c[1]), lane)
    p_sum = jnp.zeros((8, n_sub), F32)
    for h in range(N_KV_HEADS):
        in_h = (row8 >= h * GQA) & (row8 < (h + 1) * GQA)
        p_sum = p_sum + jnp.where(row8 == h, jnp.sum(jnp.where(in_h, p_c, 0.0), axis=0, keepdims=True), 0.0)
    score = _block_scores(_split_dot(p_sum, selmap_ref[...]), blk, q_pos)
    score = jnp.where((blk < n_real) & (row8 < N_KV_HEADS), score, BELOW_NEG)
    blk_f = blk.astype(F32)
    picked = jnp.full((8, LANES), -1.0, F32)
    for it in range(min(N_SELECT, n_real)):
        mx = jnp.max(score, axis=-1, keepdims=True)
        idx = jnp.min(jnp.where(score == mx, blk_f, float(n_sel)), axis=-1, keepdims=True)
        picked = jnp.where(lane == it, jnp.where(mx > 0.5 * NEG, idx, -1.0), picked)
        score = jnp.where(blk_f == idx, BELOW_NEG, score)
    idx_ref[0] = picked.astype(jnp.int32)


def _cmp_sample(page_table, pool_t, kvc_new, qb, bias, w_pos, selmap):
    n_seq, n_pages = page_table.shape
    half = N_KV_HEADS * HEAD_DIM
    pos = jnp.arange(PAGE_SIZE)
    dest = (pos % CMP_STRIDE) * (PAGE_SIZE // CMP_STRIDE) + pos // CMP_STRIDE
    perm = (dest[:, None] == jnp.arange(PAGE_SIZE)[None, :]).astype(BF16)
    zero = jnp.zeros_like(perm)
    perm = jnp.concatenate([jnp.concatenate([perm, zero], axis=1), jnp.concatenate([zero, perm], axis=1)], axis=0)
    grid_spec = pltpu.PrefetchScalarGridSpec(
        num_scalar_prefetch=1,
        grid=(n_seq,),
        in_specs=[pl.BlockSpec(memory_space=pl.ANY),
                  pl.BlockSpec((1, 1, KV_WIDTH), lambda i, pt: (i, 0, 0)),
                  pl.BlockSpec((1, 1, NSA_WIDTH), lambda i, pt: (i, 0, 0)),
                  _const_spec(bias.shape), _const_spec(w_pos.shape), _const_spec(perm.shape), _const_spec(selmap.shape)],
        out_specs=[pl.BlockSpec((1, 1, NSA_WIDTH), lambda i, pt: (i, 0, 0)),
                   pl.BlockSpec((1, 8, LANES), lambda i, pt: (i, 0, 0))],
        scratch_shapes=[pltpu.VMEM((2, n_pages // 2, 2, N_KV_HEADS, HEAD_DIM, 2 * PAGE_SIZE), F32),
                        pltpu.VMEM((2, n_pages, PAGE_SIZE, half), F32), pltpu.SemaphoreType.DMA((2,))],
    )
    return pl.pallas_call(
        functools.partial(_cmp_sample_body, n_pages=n_pages, n_seq=n_seq),
        grid_spec=grid_spec,
        out_shape=[jax.ShapeDtypeStruct((n_seq, 1, NSA_WIDTH), F32), jax.ShapeDtypeStruct((n_seq, 8, LANES), jnp.int32)],
        compiler_params=_cparams("arbitrary"),
        name="cmp_sample",
    )(page_table.reshape(-1), pool_t, kvc_new, qb, bias, w_pos, perm, selmap)


def _attn_sample_body(idx_ref, pg_ref, pool_ref, q_ref, gl_ref, oc_ref, ksn_ref, kwn_ref, win_ref, o_ref, wout_ref,
                      buf, sem, *, n_pages, n_seq, k_sel):
    b = pl.program_id(0)
    blk_per_page = PAGE_SIZE // SEL_BLOCK
    n_past = n_pages * blk_per_page
    n_blk = N_KV_HEADS * k_sel

    def fetch(seq, slot):
        for j in range(n_blk):
            page = pg_ref[seq * n_blk + j]
            pltpu.make_async_copy(pool_ref.at[page, :, j // k_sel], buf.at[slot, j], sem.at[slot]).start()

    @pl.when(b == 0)
    def _():
        fetch(0, 0)

    @pl.when(b + 1 < n_seq)
    def _():
        fetch(b + 1, (b + 1) % 2)

    slot = b % 2
    pltpu.make_async_copy(pool_ref.at[pl.ds(0, n_blk), :, 0], buf.at[slot], sem.at[slot]).wait()

    q = q_ref[0].astype(F32)
    gate = _sigmoid(gl_ref[0])
    o_c = oc_ref[0]
    ks_new = ksn_ref[0].astype(BF16).astype(F32)
    kw_new = kwn_ref[0].astype(BF16).astype(F32)
    n_win = win_ref.shape[-1]
    lane = lax.broadcasted_iota(jnp.int32, (1, PAGE_SIZE), 1)
    lane_blk = lax.shift_right_logical(lane, SEL_BLOCK.bit_length() - 1)
    head_out = []
    for h in range(N_KV_HEADS):
        k_lo, v_lo = h * HEAD_DIM, (N_KV_HEADS + h) * HEAD_DIM
        qs = _stack_heads_single(q, h)
        qf = qs.astype(F32)
        s_blocks, m_blocks = [], []
        has_new = jnp.zeros((1, 1), F32)
        for j in range(k_sel):
            i = idx_ref[b * n_blk + h * k_sel + j]
            ok = jnp.where((i >= 0) & (i < n_past), 1.0, 0.0)
            m_blocks.append(jnp.where(lane_blk == i % blk_per_page, ok, 0.0))
            has_new = jnp.maximum(has_new, jnp.where(i >= n_past, 1.0, 0.0))
            s_blocks.append(_dot(qs, buf[slot, h * k_sel + j, 0].astype(BF16)))
        past = jnp.concatenate(m_blocks, axis=1) > 0.5
        s = jnp.where(past, jnp.concatenate(s_blocks, axis=1), NEG)
        s_new = jnp.where(has_new > 0.5, jnp.sum(qf * ks_new[:, k_lo:k_lo + HEAD_DIM], axis=-1, keepdims=True), NEG)
        m = jnp.maximum(jnp.max(s, axis=-1, keepdims=True), s_new)
        p = jnp.where(past, jnp.exp(s - m), 0.0)
        p_new = jnp.where(has_new > 0.5, jnp.exp(s_new - m), 0.0)
        p_bf = p.astype(BF16)
        o_s = p_new.astype(BF16).astype(F32) * ks_new[:, v_lo:v_lo + HEAD_DIM]
        for j in range(k_sel):
            o_s = o_s + _nt_dot(p_bf[:, j * PAGE_SIZE:(j + 1) * PAGE_SIZE], buf[slot, h * k_sel + j, 1].astype(BF16))
        o_s = o_s * (1.0 / (jnp.sum(p, axis=-1, keepdims=True) + p_new))
        s = _dot(qs, win_ref[0, 0, h].astype(BF16))
        s_new = jnp.sum(qf * kw_new[:, k_lo:k_lo + HEAD_DIM], axis=-1, keepdims=True)
        m = jnp.maximum(jnp.max(s, axis=-1, keepdims=True), s_new)
        p, p_new = jnp.exp(s - m), jnp.exp(s_new - m)
        inv = 1.0 / (jnp.sum(p, axis=-1, keepdims=True) + p_new)
        o_w = (_nt_dot(p.astype(BF16), win_ref[0, 1, h].astype(BF16))
               + p_new.astype(BF16).astype(F32) * kw_new[:, v_lo:v_lo + HEAD_DIM]) * inv
        for g in range(GQA):
            hd = GQA * h + g
            col = hd * 3
            head_out.append(gate[:, col:col + 1] * o_c[:, hd * HEAD_DIM:(hd + 1) * HEAD_DIM]
                            + gate[:, col + 1:col + 2] * o_s[g:g + 1] + gate[:, col + 2:col + 3] * o_w[g:g + 1])
    o_ref[0] = jnp.concatenate(head_out, axis=-1).astype(BF16)
    d_row = lax.broadcasted_iota(jnp.int32, (HEAD_DIM, HEAD_DIM), 0)
    d_col = lax.broadcasted_iota(jnp.int32, (HEAD_DIM, HEAD_DIM), 1)
    pos = lax.broadcasted_iota(jnp.int32, (1, n_win), 1)
    new_rows = jnp.broadcast_to(kwn_ref[0], (HEAD_DIM, KV_WIDTH))
    for x in range(2):
        for h in range(N_KV_HEADS):
            lo = (x * N_KV_HEADS + h) * HEAD_DIM
            new_col = jnp.sum(jnp.where(d_row == d_col, new_rows[:, lo:lo + HEAD_DIM], 0.0), axis=-1, keepdims=True)
            wout_ref[0, x, h] = jnp.where(pos == n_win - 1, new_col, pltpu.roll(win_ref[0, x, h], n_win - 1, 1))


def _attn_sample(idx, page_table, pool_t, qb, gl, o_c, kvs_new, kvw_new, win_t, k_sel):
    n_seq, n_pages = page_table.shape
    n_win = win_t.shape[-1]
    n_blk = N_KV_HEADS * k_sel
    past_blk = jnp.clip(idx.reshape(n_seq, n_blk), 0, n_pages * (PAGE_SIZE // SEL_BLOCK) - 1)
    pages = jnp.take_along_axis(page_table, past_blk // (PAGE_SIZE // SEL_BLOCK), axis=1)
    one = lambda w: pl.BlockSpec((1, 1, w), lambda i, a, p: (i, 0, 0))
    win_spec = pl.BlockSpec((1, 2, N_KV_HEADS, HEAD_DIM, n_win), lambda i, a, p: (i, 0, 0, 0, 0))
    grid_spec = pltpu.PrefetchScalarGridSpec(
        num_scalar_prefetch=2,
        grid=(n_seq,),
        in_specs=[pl.BlockSpec(memory_space=pl.ANY), one(NSA_WIDTH), one(GATE_PAD), one(NSA_WIDTH), one(KV_WIDTH),
                  one(KV_WIDTH), win_spec],
        out_specs=[one(NSA_WIDTH), win_spec],
        scratch_shapes=[pltpu.VMEM((2, n_blk, 2, HEAD_DIM, PAGE_SIZE), F32), pltpu.SemaphoreType.DMA((2,))],
    )
    return pl.pallas_call(
        functools.partial(_attn_sample_body, n_pages=n_pages, n_seq=n_seq, k_sel=k_sel),
        grid_spec=grid_spec,
        out_shape=[jax.ShapeDtypeStruct((n_seq, 1, NSA_WIDTH), BF16), jax.ShapeDtypeStruct(win_t.shape, F32)],
        compiler_params=_cparams("arbitrary"),
        name="attn_sample",
    )(idx, pages.reshape(-1), pool_t, qb, gl, o_c, kvs_new, kvw_new, win_t)


def _tail_sample_body(x_ref, o_ref, u_ref, h0r_ref, h0i_ref, hist2_ref, hist1_ref, bb_ref, cc_ref, d_ref, abr_ref,
                      abi_ref, wglu_ref, bglu_ref, wout_ref, nf_ref, wup_ref, cw_ref, cb_ref, wdown_ref, nfin_ref,
                      out_ref, hr_ref, hi_ref, cs_ref):
    u = u_ref[...]
    bu = _split_dot(u, bb_ref[...])
    ar, ai = abr_ref[...], abi_ref[...]
    h0r, h0i = h0r_ref[...], h0i_ref[...]
    hr = ar * h0r - ai * h0i + bu[:, :SSM_N]
    hi = ar * h0i + ai * h0r + bu[:, SSM_N:]
    hr_ref[...] = hr
    hi_ref[...] = hi
    y = _dot(hr.astype(BF16), cc_ref[:SSM_N, :]) + _dot(hi.astype(BF16), cc_ref[SSM_N:, :]) + d_ref[...] * u
    x1 = _mix_out(x_ref[...], o_ref[...], y, wglu_ref, bglu_ref, wout_ref)
    hn = _rms(x1, nf_ref[...]).astype(BF16)

    def prev_rows(lo, hi_col, hu):
        cs_ref[:, lo:hi_col] = hist1_ref[:, lo:hi_col]
        cs_ref[:, 2 * D_FF + lo:2 * D_FF + hi_col] = hu
        return hist2_ref[:, lo:hi_col], hist1_ref[:, lo:hi_col]

    x2 = x1 + _ffn_chunks(hn, wup_ref, cw_ref, cb_ref, wdown_ref, prev_rows)
    out_ref[...] = _rms(x2, nfin_ref[...])


def _tail_sample(x, o_nsa, u, h0r, h0i, hist2, hist1, bb, cc, d, ab_re, ab_im, w):
    n = x.shape[0]
    sds = lambda width: jax.ShapeDtypeStruct((n, width), F32)
    return pl.pallas_call(
        _tail_sample_body,
        out_shape=[sds(D_MODEL), sds(SSM_N), sds(SSM_N), sds((CONV_W - 1) * 2 * D_FF)],
        compiler_params=pltpu.CompilerParams(vmem_limit_bytes=VMEM_LIMIT),
        name="tail_sample",
    )(x, o_nsa, u, h0r, h0i, hist2, hist1, bb, cc, d, ab_re, ab_im, w["w_glu"], w["b_glu"], w["w_out"],
      w["norm_ffn"], w["w_up"], w["conv_w"], w["conv_b"], w["w_down"], w["norm_final"])


def _pad_w_in(w_in):
    c = NSA_WIDTH + 3 * KV_WIDTH
    return jnp.concatenate([w_in[:, :c], w_in[:, c + N_GATES:], w_in[:, c:c + N_GATES],
                            jnp.zeros((D_MODEL, GATE_PAD - N_GATES), w_in.dtype)], axis=1).astype(BF16)


def _cmp_weight_pos(w_cmp):
    w = w_cmp.reshape(2, CMP_LEN // CMP_STRIDE, CMP_STRIDE, HEAD_DIM, HEAD_DIM)
    eye_h = jnp.eye(N_KV_HEADS, dtype=w.dtype)
    big = (w.transpose(0, 2, 3, 1, 4)[:, :, None, :, :, None, :]
           * eye_h[None, None, :, None, None, :, None])
    half = N_KV_HEADS * HEAD_DIM
    return big.reshape(2, CMP_STRIDE * half, 2 * half).astype(BF16)


def _pe_sub(pe_cmp):
    rows = jnp.broadcast_to(pe_cmp.transpose(1, 0, 2)[:, :, None, :], (CMP_LEN, 2, N_KV_HEADS, HEAD_DIM))
    sub = rows.reshape(CMP_LEN // CMP_STRIDE, SUB_W)
    return jnp.concatenate([sub, jnp.zeros((8 - sub.shape[0], SUB_W), sub.dtype)], axis=0)


def _sel_map(n_cmp, n_sel, n_sel_pad):
    c0 = (jnp.arange(n_cmp) * CMP_STRIDE)[:, None]
    s0 = (jnp.arange(n_sel_pad) * SEL_BLOCK)[None, :]
    hit = (c0 < s0 + SEL_BLOCK) & (c0 + CMP_LEN > s0) & (jnp.arange(n_sel_pad)[None, :] < n_sel)
    return hit.astype(BF16)


def _expand_map(n_sel, t, kc_len):
    hit = jnp.arange(t)[None, :] // SEL_BLOCK == jnp.arange(n_sel)[:, None]
    return hit.astype(BF16).reshape(n_sel, t // kc_len, kc_len).transpose(1, 0, 2)


def kernel(x_prompt, x_sample, cache_kv_cmp, cache_kv_sel, cache_kv_win, state_ssm_re, state_ssm_im, state_ffn_conv, page_table, norm_mix, w_in, pe_cmp, w_cmp, ssm_a_re, ssm_a_im, ssm_log_dt, ssm_b_re, ssm_b_im, ssm_c_re, ssm_c_im, ssm_d, w_glu, b_glu, w_out, norm_ffn, w_up, conv_w, conv_b, w_down, norm_final):
    depth = w_in.shape[0]
    assert depth == 1, "single-layer trunk"
    b, t, _ = x_prompt.shape
    bd, s, _ = x_sample.shape
    assert s == 1, "one new position per sample sequence"
    n_pages = page_table.shape[1]
    l = 0
    w_pad = _pad_w_in(w_in[l])
    g_mix = norm_mix[l].reshape(1, D_MODEL)
    w_pos = _cmp_weight_pos(w_cmp[l])
    pe_sub = _pe_sub(pe_cmp[l])
    ab_re, ab_im, bb_re, bb_im = _ssm_params(ssm_log_dt[l], ssm_a_re[l], ssm_a_im[l], ssm_b_re[l], ssm_b_im[l])
    bb = jnp.concatenate([_block_diag_in(bb_re), _block_diag_in(bb_im)], axis=1).astype(BF16)
    cc = jnp.concatenate([_block_diag_out(ssm_c_re[l]), -_block_diag_out(ssm_c_im[l])], axis=0).astype(BF16)
    d_row = ssm_d[l].reshape(1, SSM_WIDTH)
    n_blk = SSM_WIDTH // LANES
    ch, st = LANES, SSM_N // n_blk
    bb_blk = jnp.stack([jnp.concatenate([bb[j * ch:(j + 1) * ch, j * st:(j + 1) * st],
                                         bb[j * ch:(j + 1) * ch, SSM_N + j * st:SSM_N + (j + 1) * st]], axis=1)
                        for j in range(n_blk)])
    cc_blk = jnp.stack([jnp.concatenate([cc[j * st:(j + 1) * st, j * ch:(j + 1) * ch],
                                         cc[SSM_N + j * st:SSM_N + (j + 1) * st, j * ch:(j + 1) * ch]], axis=0)
                        for j in range(n_blk)])
    tail_w = {"w_glu": w_glu[l].astype(BF16), "b_glu": b_glu[l].reshape(1, -1), "w_out": w_out[l].astype(BF16),
              "norm_ffn": norm_ffn[l].reshape(1, -1), "w_up": w_up[l].astype(BF16), "conv_w": conv_w[l],
              "conv_b": conv_b[l].reshape(1, -1), "w_down": w_down[l].astype(BF16),
              "norm_final": norm_final.reshape(1, -1)}

    kvc, _, _, kvct, kvst, kvwt, kstb, kwtb, gl, u, qb = _inproj(x_prompt, g_mix, w_pad, 512)
    n_sub = t // CMP_STRIDE
    kc, cmp_bias = _cmp_prompt(kvc.reshape(b, n_sub, SUB_W), pe_sub, w_pos)
    n_sel = t // SEL_BLOCK
    o_nsa = _attn_prompt(qb, gl, kc, kstb, kwtb, _sel_map(n_sub, n_sel, n_sel).T)
    y_ssm, p_hr, p_hi = _ssm_prompt(u, bb_blk, cc_blk, d_row, ab_re, ab_im)
    y_prompt, p_conv = _tail_prompt(x_prompt, o_nsa, y_ssm, tail_w)
    win_keep = min(WINDOW, t)
    kv_rows = lambda a: a.reshape(a.shape[0], 2, N_KV_HEADS, HEAD_DIM, a.shape[2]).transpose(0, 4, 1, 2, 3)[None]
    st_shape = (depth, b, SSM_GROUPS, SSM_STATE)

    kvc_n, kvs_n, kvw_n, kvct_n, kvst_n, _, _, _, gl_n, u_n, qb_n = _inproj(x_sample.reshape(1, bd, D_MODEL), g_mix, w_pad, bd)
    per_seq = lambda a: a.reshape(bd, 1, a.shape[-1])
    n_sub_s = n_pages * (PAGE_SIZE // CMP_STRIDE)
    n_sel_s = n_pages * (PAGE_SIZE // SEL_BLOCK) + 1
    n_sel_pad = -(-n_sel_s // LANES) * LANES
    k_sel = min(N_SELECT, n_sel_s)
    pos_minor = lambda a: a.transpose(0, 2, 3, 4, 1)
    o_c, picked = _cmp_sample(page_table, pos_minor(cache_kv_cmp[l]), per_seq(kvc_n), per_seq(qb_n), cmp_bias,
                              w_pos, _sel_map(n_sub_s, n_sel_s, n_sel_pad))
    idx = picked[:, :N_KV_HEADS, :k_sel].reshape(-1)
    n_buf = cache_kv_win.shape[2]
    assert n_buf == WINDOW, "window buffer holds exactly WINDOW rows"
    o_nsa_s, s_win = _attn_sample(idx, page_table, pos_minor(cache_kv_sel[l]), per_seq(qb_n), per_seq(gl_n), o_c,
                                  per_seq(kvs_n), per_seq(kvw_n), pos_minor(cache_kv_win[l]), k_sel)
    hist = state_ffn_conv[l]
    y_sample, s_hr, s_hi, s_conv = _tail_sample(
        x_sample.reshape(bd, D_MODEL), o_nsa_s.reshape(bd, -1), u_n.reshape(bd, -1),
        state_ssm_re[l].reshape(bd, SSM_N), state_ssm_im[l].reshape(bd, SSM_N), hist[:, 0], hist[:, 1], bb, cc, d_row,
        ab_re, ab_im, tail_w)
    kv_rows_s = lambda a: kv_rows(a).reshape(depth, bd, s, 2, N_KV_HEADS, HEAD_DIM)
    st_shape_s = (depth, bd, SSM_GROUPS, SSM_STATE)
    return (y_prompt, y_sample.reshape(bd, s, D_MODEL),
            kv_rows(kvct), kv_rows(kvst), kv_rows(kvwt[:, :, t - win_keep:]),
            p_hr.reshape(st_shape), p_hi.reshape(st_shape), p_conv.reshape(depth, b, CONV_W - 1, 2 * D_FF),
            kv_rows_s(kvct_n), kv_rows_s(kvst_n),
            s_win.transpose(0, 4, 1, 2, 3)[None],
            s_hr.reshape(st_shape_s), s_hi.reshape(st_shape_s), s_conv.reshape(depth, bd, CONV_W - 1, 2 * D_FF))
```
